```python
import math
import jax, jax.numpy as jnp
from jax import lax
import numpy as np

D_MODEL = 1024
BATCH = 1
SEQ = 16384
DEPTH = 1
DEC_BATCH = 2
DEC_SEQ = 8192
PAST_LEN = 128

DIFF_HEADS = 4
DIFF_DK = 64
DIFF_DV = 128
MLA_HEADS = 8
MLA_NOPE = 64
MLA_ROPE = 32
MLA_QK = MLA_NOPE + MLA_ROPE
MLA_V = 64
Q_LORA = 256
KV_LORA = 128
ROPE_BASE = 10000.0
DIFF_Q_COLS = DIFF_HEADS * 2 * DIFF_DK
DIFF_K_COLS = DIFF_HEADS * 2 * DIFF_DK
DIFF_V_COLS = DIFF_HEADS * DIFF_DV
IN_COLS = DIFF_Q_COLS + DIFF_K_COLS + DIFF_V_COLS + Q_LORA + KV_LORA + MLA_ROPE
MIX_WIDTH = DIFF_HEADS * DIFF_DV + MLA_HEADS * MLA_V
N_BUCKETS = 32
MAX_DISTANCE = 128
N_EXPERTS = 16
CAPACITY_FACTOR = 2
D_FF = 2816
Q_BLOCK = 128
EPS = 1e-6

kernel_name = "hybrid_diffattn_mla_ec_encoder"


def _rms(x, g):
    xf = x.astype(jnp.float32)
    y = xf * lax.rsqrt(jnp.mean(xf * xf, axis=-1, keepdims=True) + EPS)
    return (y * g.astype(jnp.float32)).astype(x.dtype)


def _t5_bucket(rel):
    half = N_BUCKETS // 2
    max_exact = half // 2
    bucket = (rel > 0).astype(jnp.int32) * half
    n = jnp.abs(rel)
    nf = jnp.maximum(n, 1).astype(jnp.float32)
    large = max_exact + (jnp.log(nf / max_exact) / math.log(MAX_DISTANCE / max_exact)
                         * (half - max_exact)).astype(jnp.int32)
    large = jnp.minimum(large, half - 1)
    return bucket + jnp.where(n < max_exact, n, large)


def _rope(x, pos):
    half = MLA_ROPE // 2
    inv = 1.0 / (ROPE_BASE ** (jnp.arange(half, dtype=jnp.float32) / half))
    ang = pos.astype(jnp.float32)[:, None] * inv[None, :]
    shape = (ang.shape[0],) + (1,) * (x.ndim - 3) + (half,)
    cos = jnp.cos(ang).reshape(shape)
    sin = jnp.sin(ang).reshape(shape)
    xf = x.astype(jnp.float32)
    x1, x2 = xf[..., :half], xf[..., half:]
    return jnp.concatenate([x1 * cos - x2 * sin, x1 * sin + x2 * cos], axis=-1).astype(x.dtype)


def _sweep_query_blocks(block_fn, qs):
    B, S = qs[0].shape[:2]
    nblk = S // Q_BLOCK
    blocked = tuple(jnp.moveaxis(q.reshape((B, nblk, Q_BLOCK) + q.shape[2:]), 1, 0) for q in qs)
    starts = jnp.arange(nblk, dtype=jnp.int32) * Q_BLOCK
    out = lax.map(lambda a: block_fn(a[0], *a[1]), (starts, blocked))
    return jnp.moveaxis(out, 0, 1).reshape((B, S) + out.shape[3:])


def _diff_attention(hq, rel_bias, q_norm, k_norm, lq1, lk1, lq2, lk2, subln, lam_init):
    B, S, _ = hq.shape
    q = hq[..., :DIFF_Q_COLS].reshape(B, S, DIFF_HEADS, 2, DIFF_DK)
    k = hq[..., DIFF_Q_COLS:DIFF_Q_COLS + DIFF_K_COLS].reshape(B, S, DIFF_HEADS, 2, DIFF_DK)
    v = hq[..., DIFF_Q_COLS + DIFF_K_COLS:DIFF_Q_COLS + DIFF_K_COLS + DIFF_V_COLS].reshape(B, S, DIFF_HEADS, DIFF_DV)
    q = _rms(q, q_norm)
    k = _rms(k, k_norm)
    q1, q2 = q[..., 0, :], q[..., 1, :]
    k1, k2 = k[..., 0, :], k[..., 1, :]
    f32 = jnp.float32
    lam = (jnp.exp(jnp.sum(lq1.astype(f32) * lk1.astype(f32)))
           - jnp.exp(jnp.sum(lq2.astype(f32) * lk2.astype(f32))) + lam_init)
    scale = DIFF_DK ** -0.5
    kpos = jnp.arange(S, dtype=jnp.int32)

    def block(start, q1b, q2b):
        qpos = start + jnp.arange(Q_BLOCK, dtype=jnp.int32)
        bias = rel_bias[_t5_bucket(kpos[None, :] - qpos[:, None])]
        bias = jnp.transpose(bias, (2, 0, 1)).astype(f32)[None]
        s1 = jnp.einsum('bqhd,bkhd->bhqk', q1b, k1).astype(f32) * scale + bias
        s2 = jnp.einsum('bqhd,bkhd->bhqk', q2b, k2).astype(f32) * scale + bias
        p = jax.nn.softmax(s1, axis=-1) - lam * jax.nn.softmax(s2, axis=-1)
        return jnp.einsum('bhqk,bkhd->bqhd', p.astype(v.dtype), v)

    o = _sweep_query_blocks(block, (q1, q2))
    o = _rms(o, subln) * (1.0 - lam_init)
    return o.reshape(B, S, DIFF_HEADS * DIFF_DV)


def _mla(hq, q_lat_norm, kv_lat_norm, w_uq, w_ukv, q_norm, k_norm):
    B, S, _ = hq.shape
    off = DIFF_Q_COLS + DIFF_K_COLS + DIFF_V_COLS
    c_q = _rms(hq[..., off:off + Q_LORA], q_lat_norm)
    c_kv = _rms(hq[..., off + Q_LORA:off + Q_LORA + KV_LORA], kv_lat_norm)
    k_pe = hq[..., off + Q_LORA + KV_LORA:]
    q = (c_q @ w_uq).reshape(B, S, MLA_HEADS, MLA_QK)
    kv = (c_kv @ w_ukv).reshape(B, S, MLA_HEADS, MLA_NOPE + MLA_V)
    k_nope, v = kv[..., :MLA_NOPE], kv[..., MLA_NOPE:]
    k = jnp.concatenate([k_nope, jnp.broadcast_to(k_pe[:, :, None, :], (B, S, MLA_HEADS, MLA_ROPE))], axis=-1)
    q = _rms(q, q_norm)
    k = _rms(k, k_norm)
    pos = jnp.arange(S, dtype=jnp.int32)
    q = jnp.concatenate([q[..., :MLA_NOPE], _rope(q[..., MLA_NOPE:], pos)], axis=-1)
    k = jnp.concatenate([k[..., :MLA_NOPE], _rope(k[..., MLA_NOPE:], pos)], axis=-1)
    scale = MLA_QK ** -0.5

    def block(start, qb):
        s = jnp.einsum('bqhd,bkhd->bhqk', qb, k).astype(jnp.float32) * scale
        p = jax.nn.softmax(s, axis=-1)
        return jnp.einsum('bhqk,bkhd->bqhd', p.astype(v.dtype), v)

    o = _sweep_query_blocks(block, (q,))
    return o.reshape(B, S, MLA_HEADS * MLA_V)


def _expert_choice_ffn(h, w_router, w_gate, w_up, w_down):
    B, S, D = h.shape
    T = B * S
    cap = CAPACITY_FACTOR * T // N_EXPERTS
    xf = h.reshape(T, D)
    aff = jax.nn.softmax((xf @ w_router).astype(jnp.float32), axis=-1)
    gates, idx = lax.top_k(aff.T, cap)
    xg = xf[idx]
    a = jnp.einsum('ecd,edf->ecf', xg, w_gate)
    b = jnp.einsum('ecd,edf->ecf', xg, w_up)
    y = jnp.einsum('ecf,efd->ecd', jax.nn.silu(a) * b, w_down)
    y = (y.astype(jnp.float32) * gates[..., None]).astype(h.dtype)
    out = jnp.zeros((T, D), h.dtype).at[idx.reshape(-1)].add(y.reshape(-1, D))
    return out.reshape(B, S, D)


def _trunk(x, rel_bias, attn_norm, w_in, diff_q_norm, diff_k_norm, lambda_q1, lambda_k1,
           lambda_q2, lambda_k2, diff_subln, mla_q_latent_norm, mla_kv_latent_norm, w_uq, w_ukv,
           mla_q_norm, mla_k_norm, w_out, ffn_norm, w_router, w_gate, w_up, w_down):
    for l in range(DEPTH):
        lam_init = 0.8 - 0.6 * math.exp(-0.3 * l)
        h = _rms(x, attn_norm[l])
        hq = h @ w_in[l]
        o_diff = _diff_attention(hq, rel_bias, diff_q_norm[l], diff_k_norm[l], lambda_q1[l], lambda_k1[l],
                                 lambda_q2[l], lambda_k2[l], diff_subln[l], lam_init)
        o_mla = _mla(hq, mla_q_latent_norm[l], mla_kv_latent_norm[l], w_uq[l], w_ukv[l],
                     mla_q_norm[l], mla_k_norm[l])
        x = x + jnp.concatenate([o_diff, o_mla], axis=-1) @ w_out[l]
        x = x + _expert_choice_ffn(_rms(x, ffn_norm[l]), w_router[l], w_gate[l], w_up[l], w_down[l])
    return x


def setup_inputs(seed: int = 0) -> dict:
    key = jax.random.key(seed)
    ks = jax.random.split(key, 24)
    f32 = jnp.float32
    nrm = lambda k, shape, s: jax.random.normal(k, shape, f32) * s
    gain = lambda k, shape: 1.0 + 0.02 * jax.random.normal(k, shape, f32)
    L = DEPTH
    return {
        "x_prompt": nrm(ks[0], (BATCH, SEQ, D_MODEL), 1.0),
        "x_sample": nrm(ks[1], (DEC_BATCH, DEC_SEQ, D_MODEL), 1.0),
        "rel_bias": nrm(ks[2], (N_BUCKETS, DIFF_HEADS), 0.5),
        "attn_norm": gain(ks[3], (L, D_MODEL)),
        "w_in": nrm(ks[4], (L, D_MODEL, IN_COLS), D_MODEL ** -0.5),
        "diff_q_norm": gain(ks[5], (L, DIFF_DK)),
        "diff_k_norm": gain(ks[6], (L, DIFF_DK)),
        "lambda_q1": nrm(ks[7], (L, DIFF_DK), 0.1),
        "lambda_k1": nrm(ks[8], (L, DIFF_DK), 0.1),
        "lambda_q2": nrm(ks[9], (L, DIFF_DK), 0.1),
        "lambda_k2": nrm(ks[10], (L, DIFF_DK), 0.1),
        "diff_subln": gain(ks[11], (L, DIFF_DV)),
        "mla_q_latent_norm": gain(ks[12], (L, Q_LORA)),
        "mla_kv_latent_norm": gain(ks[13], (L, KV_LORA)),
        "w_uq": nrm(ks[14], (L, Q_LORA, MLA_HEADS * MLA_QK), Q_LORA ** -0.5),
        "w_ukv": nrm(ks[15], (L, KV_LORA, MLA_HEADS * (MLA_NOPE + MLA_V)), KV_LORA ** -0.5),
        "mla_q_norm": gain(ks[16], (L, MLA_QK)),
        "mla_k_norm": gain(ks[17], (L, MLA_QK)),
        "w_out": nrm(ks[18], (L, MIX_WIDTH, D_MODEL), MIX_WIDTH ** -0.5),
        "ffn_norm": gain(ks[19], (L, D_MODEL)),
        "w_router": nrm(ks[20], (L, D_MODEL, N_EXPERTS), D_MODEL ** -0.5),
        "w_gate": nrm(ks[21], (L, N_EXPERTS, D_MODEL, D_FF), D_MODEL ** -0.5),
        "w_up": nrm(ks[22], (L, N_EXPERTS, D_MODEL, D_FF), D_MODEL ** -0.5),
        "w_down": nrm(ks[23], (L, N_EXPERTS, D_FF, D_MODEL), D_FF ** -0.5),
    }


def reference(x_prompt, x_sample, rel_bias, attn_norm, w_in, diff_q_norm, diff_k_norm, lambda_q1,
              lambda_k1, lambda_q2, lambda_k2, diff_subln, mla_q_latent_norm, mla_kv_latent_norm,
              w_uq, w_ukv, mla_q_norm, mla_k_norm, w_out, ffn_norm, w_router, w_gate, w_up, w_down):
    y_prompt = _trunk(x_prompt, rel_bias, attn_norm, w_in, diff_q_norm, diff_k_norm, lambda_q1, lambda_k1,
                      lambda_q2, lambda_k2, diff_subln, mla_q_latent_norm, mla_kv_latent_norm, w_uq, w_ukv,
                      mla_q_norm, mla_k_norm, w_out, ffn_norm, w_router, w_gate, w_up, w_down)
    y_sample = _trunk(x_sample, rel_bias, attn_norm, w_in, diff_q_norm, diff_k_norm, lambda_q1, lambda_k1,
                      lambda_q2, lambda_k2, diff_subln, mla_q_latent_norm, mla_kv_latent_norm, w_uq, w_ukv,
                      mla_q_norm, mla_k_norm, w_out, ffn_norm, w_router, w_gate, w_up, w_down)
    return (y_prompt, y_sample)
```

```python
import functools
import math

import jax
import jax.numpy as jnp
from jax import lax
from jax.experimental import pallas as pl
from jax.experimental.pallas import tpu as pltpu

F32 = jnp.float32
BF16 = jnp.bfloat16
I32 = jnp.int32

D_MODEL = 1024
DEPTH = 1
DIFF_HEADS = 4
DIFF_DK = 64
DIFF_DV = 128
MLA_HEADS = 8
MLA_NOPE = 64
MLA_ROPE = 32
MLA_QK = MLA_NOPE + MLA_ROPE
MLA_V = 64
Q_LORA = 256
KV_LORA = 128
ROPE_BASE = 10000.0
N_BUCKETS = 32
MAX_DISTANCE = 128
N_EXPERTS = 16
CAPACITY_FACTOR = 2
D_FF = 2816
EPS = 1e-6

LANES = 128
DIFF_W = DIFF_HEADS * DIFF_DV
MLA_W = MLA_HEADS * LANES
NEG_BIG = -1e30
VMEM_LIMIT = 56 * 1024 * 1024

PREP_TM = 512
ATT_T = 512
OUT_TM = 512
ROUTE_RB = 256
GATHER_SC = 256
GATHER_TB = 256
FFN_TF = 256
COMB_TT = 256
COMB_W = 64


def _cparams(sem, vmem=VMEM_LIMIT):
    return pltpu.CompilerParams(dimension_semantics=sem, vmem_limit_bytes=vmem)


def _rms_rows(x, gain):
    ms = jnp.mean(x * x, axis=-1, keepdims=True)
    return x * lax.rsqrt(ms + EPS) * gain


def _prep_kernel(x_ref, cos_ref, sin_ref, an_ref, wqkv_ref, wcq_ref, wckv_ref, wkpe_ref,
                 wuq_ref, wuk_ref, wuv_ref, dqg_ref, dkg_ref, cqg_ref, ckvg_ref, mqg_ref, mkg_ref,
                 qd_ref, kd_ref, vd_ref, qm_ref, km_ref, vm_ref):
    tm = x_ref.shape[0]
    hb = _rms_rows(x_ref[...], an_ref[...]).astype(BF16)

    lane = lax.broadcasted_iota(I32, (tm, LANES), 1)
    lo = lane < DIFF_DK

    qkv = jnp.dot(hb, wqkv_ref[...], preferred_element_type=F32)
    diff_scale = DIFF_DK ** -0.5
    for hd in range(DIFF_HEADS):
        for off, g_ref, o_ref, scale in ((0, dqg_ref, qd_ref, diff_scale), (DIFF_W, dkg_ref, kd_ref, None)):
            blk = qkv[:, off + hd * LANES: off + (hd + 1) * LANES]
            sq = blk * blk
            s_lo = jnp.sum(jnp.where(lo, sq, 0.0), axis=-1, keepdims=True)
            s_hi = jnp.sum(jnp.where(lo, 0.0, sq), axis=-1, keepdims=True)
            r = jnp.where(lo, lax.rsqrt(s_lo / DIFF_DK + EPS), lax.rsqrt(s_hi / DIFF_DK + EPS))
            y = blk * r * g_ref[...]
            if scale is not None:
                y = y * scale
            o_ref[:, hd * LANES:(hd + 1) * LANES] = y.astype(BF16)
    vd_ref[...] = qkv[:, 2 * DIFF_W:3 * DIFF_W].astype(BF16)

    cq = _rms_rows(jnp.dot(hb, wcq_ref[...], preferred_element_type=F32), cqg_ref[...]).astype(BF16)
    ckv = _rms_rows(jnp.dot(hb, wckv_ref[...], preferred_element_type=F32), ckvg_ref[...]).astype(BF16)
    kpe = jnp.dot(hb, wkpe_ref[...], preferred_element_type=F32)
    q = jnp.dot(cq, wuq_ref[...], preferred_element_type=F32)
    k = jnp.dot(ckv, wuk_ref[...], preferred_element_type=F32)
    vm_ref[...] = jnp.dot(ckv, wuv_ref[...], preferred_element_type=F32).astype(BF16)

    cosb = cos_ref[...]
    sinb = sin_ref[...]
    first_half = (lane >= MLA_NOPE) & (lane < MLA_NOPE + MLA_ROPE // 2)
    mla_scale = MLA_QK ** -0.5

    def norm_rope(blk, gain):
        ss = jnp.sum(blk * blk, axis=-1, keepdims=True)
        y = blk * lax.rsqrt(ss / MLA_QK + EPS) * gain
        partner = jnp.where(first_half,
                            pltpu.roll(y, LANES - MLA_ROPE // 2, 1),
                            pltpu.roll(y, MLA_ROPE // 2, 1))
        return y * cosb + partner * sinb

    for hd in range(MLA_HEADS):
        sl = slice(hd * LANES, (hd + 1) * LANES)
        qm_ref[:, sl] = (norm_rope(q[:, sl], mqg_ref[...]) * mla_scale).astype(BF16)
        km_ref[:, sl] = norm_rope(k[:, sl] + kpe, mkg_ref[...]).astype(BF16)


def _prep(x2d, seq, cos_t, sin_t, p):
    t = x2d.shape[0]
    tm = PREP_TM
    nseq = seq // tm
    row = lambda i: (i, 0)
    full = lambda i: (0, 0)
    wspec = lambda a: pl.BlockSpec(a.shape, full)
    weights = (p["attn_norm"], p["w_qkv"], p["w_cq"], p["w_ckv"], p["w_kpe"], p["w_uq"], p["w_uk"], p["w_uv"],
               p["dq_gain"], p["dk_gain"], p["cq_gain"], p["ckv_gain"], p["mq_gain"], p["mk_gain"])
    out_w = (DIFF_W, DIFF_W, DIFF_W, MLA_W, MLA_W, MLA_HEADS * MLA_V)
    return pl.pallas_call(
        _prep_kernel,
        grid=(t // tm,),
        in_specs=[pl.BlockSpec((tm, D_MODEL), row),
                  pl.BlockSpec((tm, LANES), lambda i: (i % nseq, 0)),
                  pl.BlockSpec((tm, LANES), lambda i: (i % nseq, 0))] + [wspec(w) for w in weights],
        out_specs=[pl.BlockSpec((tm, w), row) for w in out_w],
        out_shape=[jax.ShapeDtypeStruct((t, w), BF16) for w in out_w],
        compiler_params=_cparams(("parallel",)),
        name="prep",
    )(x2d, cos_t, sin_t, *weights)


def _bias_kernel(rb_ref, o_ref):
    hd = pl.program_id(0)
    d = pl.program_id(1)
    tq, tk = o_ref.shape
    row = lax.broadcasted_iota(I32, (tq, tk), 0)
    col = lax.broadcasted_iota(I32, (tq, tk), 1)
    rel = col - row + (d - 1) * tk
    half = N_BUCKETS // 2
    max_exact = half // 2
    n = jnp.abs(rel)
    nf = jnp.maximum(n, 1).astype(F32)
    large = max_exact + (jnp.log(nf / max_exact) / math.log(MAX_DISTANCE / max_exact)
                         * (half - max_exact)).astype(I32)
    large = jnp.minimum(large, half - 1)
    bucket = jnp.where(rel > 0, half, 0) + jnp.where(n < max_exact, n, large)
    acc = jnp.zeros((tq, tk), F32)
    for b in range(N_BUCKETS):
        acc = jnp.where(bucket == b, rb_ref[hd, b], acc)
    o_ref[...] = acc


def _bias_tiles(rel_bias_t, t):
    return pl.pallas_call(
        _bias_kernel,
        grid=(DIFF_HEADS, 3),
        in_specs=[pl.BlockSpec(memory_space=pltpu.SMEM)],
        out_specs=pl.BlockSpec((None, None, t, t), lambda h, d: (h, d, 0, 0)),
        out_shape=jax.ShapeDtypeStruct((DIFF_HEADS, 3, t, t), F32),
        compiler_params=_cparams(("parallel", "parallel")),
        name="bias_tiles",
    )(rel_bias_t)


def _online_update(mi, s, v, m_scr, l_scr, acc_scr):
    m_old = m_scr[mi]
    m_new = jnp.maximum(m_old, jnp.max(s, axis=-1, keepdims=True))
    alpha = jnp.exp(m_old - m_new)
    p = jnp.exp(s - m_new)
    l_scr[mi] = alpha * l_scr[mi] + jnp.sum(p, axis=-1, keepdims=True)
    acc_scr[mi] = alpha * acc_scr[mi] + jnp.dot(p.astype(BF16), v, preferred_element_type=F32)
    m_scr[mi] = m_new


def _qk(q, k):
    return lax.dot_general(q, k, (((1,), (1,)), ((), ())), preferred_element_type=F32)


def _diff_attn_kernel(rb_ref, q_ref, k_ref, v_ref, bias_ref, subln_ref, lq1_ref, lk1_ref, lq2_ref, lk2_ref,
                      o_ref, m_scr, l_scr, acc_scr, *, lam_init):
    hd = pl.program_id(1)
    qi = pl.program_id(2)
    t = q_ref.shape[0]
    n = k_ref.shape[0] // t

    q = q_ref[...]
    lane = lax.broadcasted_iota(I32, q.shape, 1)
    zero = jnp.zeros_like(q)
    qs = (jnp.where(lane < DIFF_DK, q, zero), jnp.where(lane < DIFF_DK, zero, q))

    m_scr[...] = jnp.full(m_scr.shape, NEG_BIG, F32)
    l_scr[...] = jnp.zeros(l_scr.shape, F32)
    acc_scr[...] = jnp.zeros(acc_scr.shape, F32)

    def make_step(near):
        def step(j, carry):
            off = pl.multiple_of(j * t, t)
            k = k_ref[pl.ds(off, t), :]
            v = v_ref[pl.ds(off, t), :]
            for mi in range(2):
                s = _qk(qs[mi], k)
                if near:
                    s = s + bias_ref[j - qi + 1]
                _online_update(mi, s, v, m_scr, l_scr, acc_scr)
            return carry
        return step

    lo = jnp.maximum(qi - 1, 0)
    hi = jnp.minimum(qi + 2, n)
    lax.fori_loop(0, lo, make_step(False), 0)
    m_scr[...] = m_scr[...] + rb_ref[hd, N_BUCKETS // 2 - 1]
    lax.fori_loop(lo, hi, make_step(True), 0)
    m_scr[...] = m_scr[...] - rb_ref[hd, N_BUCKETS - 1]
    lax.fori_loop(hi, n, make_step(False), 0)

    lam = (jnp.exp(jnp.sum(lq1_ref[...] * lk1_ref[...], axis=-1, keepdims=True))
           - jnp.exp(jnp.sum(lq2_ref[...] * lk2_ref[...], axis=-1, keepdims=True)) + lam_init)
    o = acc_scr[0] / l_scr[0] - lam * (acc_scr[1] / l_scr[1])
    o_ref[...] = (_rms_rows(o, subln_ref[...]) * (1.0 - lam_init)).astype(BF16)


def _diff_attention(qd, kd, vd, bias_tiles, rel_bias_t, p, lam_init):
    b, s, _ = qd.shape
    t = ATT_T
    vec = lambda a: pl.BlockSpec(a.shape, lambda bi, h, i: (0, 0))
    kv_spec = pl.BlockSpec((None, s, LANES), lambda bi, h, i: (bi, 0, h))
    return pl.pallas_call(
        functools.partial(_diff_attn_kernel, lam_init=lam_init),
        grid=(b, DIFF_HEADS, s // t),
        in_specs=[pl.BlockSpec(memory_space=pltpu.SMEM),
                  pl.BlockSpec((None, t, LANES), lambda bi, h, i: (bi, i, h)),
                  kv_spec, kv_spec,
                  pl.BlockSpec((None, 3, t, t), lambda bi, h, i: (h, 0, 0, 0)),
                  vec(p["subln"]), vec(p["lq1"]), vec(p["lk1"]), vec(p["lq2"]), vec(p["lk2"])],
        out_specs=pl.BlockSpec((None, t, LANES), lambda bi, h, i: (bi, i, h)),
        out_shape=jax.ShapeDtypeStruct((b, s, DIFF_W), BF16),
        scratch_shapes=[pltpu.VMEM((2, t, 1), F32), pltpu.VMEM((2, t, 1), F32), pltpu.VMEM((2, t, LANES), F32)],
        compiler_params=_cparams(("parallel", "parallel", "arbitrary")),
        name="diff_attn",
    )(rel_bias_t, qd, kd, vd, bias_tiles, p["subln"], p["lq1"], p["lk1"], p["lq2"], p["lk2"])


def _mla_attn_kernel(q_ref, k_ref, v_ref, o_ref, m_scr, l_scr, acc_scr):
    t = q_ref.shape[0]
    n = k_ref.shape[0] // t
    qs = (q_ref[:, :LANES], q_ref[:, LANES:])

    m_scr[...] = jnp.full(m_scr.shape, NEG_BIG, F32)
    l_scr[...] = jnp.zeros(l_scr.shape, F32)
    acc_scr[...] = jnp.zeros(acc_scr.shape, F32)

    def step(j, carry):
        off = pl.multiple_of(j * t, t)
        v = v_ref[pl.ds(off, t), :]
        for mi in range(2):
            k = k_ref[pl.ds(off, t), mi * LANES:(mi + 1) * LANES]
            _online_update(mi, _qk(qs[mi], k), v, m_scr, l_scr, acc_scr)
        return carry

    lax.fori_loop(0, n, step, 0)
    lane = lax.broadcasted_iota(I32, (t, LANES), 1)
    o = jnp.where(lane < MLA_V, acc_scr[0] / l_scr[0], acc_scr[1] / l_scr[1])
    o_ref[...] = o.astype(BF16)


def _mla_attention(qm, km, vm):
    b, s, _ = qm.shape
    t = ATT_T
    return pl.pallas_call(
        _mla_attn_kernel,
        grid=(b, MLA_HEADS // 2, s // t),
        in_specs=[pl.BlockSpec((None, t, 2 * LANES), lambda bi, h, i: (bi, i, h)),
                  pl.BlockSpec((None, s, 2 * LANES), lambda bi, h, i: (bi, 0, h)),
                  pl.BlockSpec((None, s, LANES), lambda bi, h, i: (bi, 0, h))],
        out_specs=pl.BlockSpec((None, t, LANES), lambda bi, h, i: (bi, i, h)),
        out_shape=jax.ShapeDtypeStruct((b, s, MLA_HEADS * MLA_V), BF16),
        scratch_shapes=[pltpu.VMEM((2, t, 1), F32), pltpu.VMEM((2, t, 1), F32), pltpu.VMEM((2, t, LANES), F32)],
        compiler_params=_cparams(("parallel", "parallel", "arbitrary")),
        name="mla_attn",
    )(qm, km, vm)


def _split_bf16(x):
    hi = x.astype(BF16)
    lo = (x - hi.astype(F32)).astype(BF16)
    return hi, lo


def _outproj_kernel(x_ref, od_ref, om_ref, wo1_ref, wo2_ref, fn_ref, wrh_ref, wrl_ref,
                    x1_ref, h2_ref, aff_ref):
    x1 = (x_ref[...]
          + jnp.dot(od_ref[...], wo1_ref[...], preferred_element_type=F32)
          + jnp.dot(om_ref[...], wo2_ref[...], preferred_element_type=F32))
    x1_ref[...] = x1
    h = _rms_rows(x1, fn_ref[...])
    hh, hl = _split_bf16(h)
    h2_ref[...] = hh
    logits = (jnp.dot(hh, wrh_ref[...], preferred_element_type=F32)
              + jnp.dot(hl, wrh_ref[...], preferred_element_type=F32)
              + jnp.dot(hh, wrl_ref[...], preferred_element_type=F32))
    lane = lax.broadcasted_iota(I32, logits.shape, 1)
    logits = jnp.where(lane < N_EXPERTS, logits, NEG_BIG)
    e = jnp.exp(logits - jnp.max(logits, axis=-1, keepdims=True))
    aff = e / jnp.sum(e, axis=-1, keepdims=True)
    aff_ref[...] = aff[:, :N_EXPERTS]


def _outproj(x2d, od, om, p):
    t = x2d.shape[0]
    tm = OUT_TM
    row = lambda i: (i, 0)
    wspec = lambda a: pl.BlockSpec(a.shape, lambda i: (0, 0))
    weights = (p["w_o1"], p["w_o2"], p["ffn_norm"], p["w_r_hi"], p["w_r_lo"])
    return pl.pallas_call(
        _outproj_kernel,
        grid=(t // tm,),
        in_specs=[pl.BlockSpec((tm, D_MODEL), row), pl.BlockSpec((tm, DIFF_W), row),
                  pl.BlockSpec((tm, MLA_HEADS * MLA_V), row)] + [wspec(w) for w in weights],
        out_specs=[pl.BlockSpec((tm, D_MODEL), row), pl.BlockSpec((tm, D_MODEL), row),
                   pl.BlockSpec((tm, N_EXPERTS), row)],
        out_shape=[jax.ShapeDtypeStruct((t, D_MODEL), F32), jax.ShapeDtypeStruct((t, D_MODEL), BF16),
                   jax.ShapeDtypeStruct((t, N_EXPERTS), F32)],
        compiler_params=_cparams(("parallel",)),
        name="outproj",
    )(x2d, od, om, *weights)


def _route_kernel(aff_ref, pos_ref, gate_ref, csx_ref, bnd_ref, *, cap, n_chunks, chunk):
    rows = aff_ref.shape[0]
    groups = LANES // N_EXPERTS
    aff = aff_ref[...]
    bits = pltpu.bitcast(aff, I32)

    def expert_total(v):
        for sh in (N_EXPERTS, 2 * N_EXPERTS, 4 * N_EXPERTS):
            v = v + pltpu.roll(v, sh, 1)
        return v

    def count(mask):
        return expert_total(jnp.sum(mask.astype(I32), axis=0, keepdims=True))

    def search(i, thr):
        cand = thr | jnp.left_shift(jnp.int32(1), 30 - i)
        return jnp.where(count(bits >= cand) >= cap, cand, thr)

    thr = lax.fori_loop(0, 31, search, jnp.zeros((1, LANES), I32))

    a = lax.broadcasted_iota(I32, (LANES, LANES), 0)
    b = lax.broadcasted_iota(I32, (LANES, LANES), 1)
    same = (a % N_EXPERTS) == (b % N_EXPERTS)
    q_all = same.astype(BF16)
    q_before = (same & (a // N_EXPERTS < b // N_EXPERTS)).astype(BF16)
    rb = ROUTE_RB
    ra = lax.broadcasted_iota(I32, (rb, rb), 0)
    ca = lax.broadcasted_iota(I32, (rb, rb), 1)
    tri = (ca < ra).astype(BF16)

    def prefix(mask):
        mb = mask.astype(BF16)
        outs = []
        offset = jnp.zeros((1, LANES), F32)
        for r0 in range(0, rows, rb):
            blk = mb[r0:r0 + rb]
            tot = jnp.dot(blk, q_all, preferred_element_type=F32)
            within = jnp.dot(blk, q_before, preferred_element_type=F32)
            above = jnp.dot(tri, tot.astype(BF16), preferred_element_type=F32)
            outs.append(above + within + offset)
            offset = offset + jnp.sum(tot, axis=0, keepdims=True)
        return jnp.concatenate(outs, axis=0)

    gt = bits > thr
    eq = bits == thr
    need = (cap - count(gt)).astype(F32)
    sel = gt | (eq & (prefix(eq) < need))
    csx = prefix(sel)
    pos_ref[...] = jnp.where(sel, csx.astype(I32), -1)
    gate_ref[...] = jnp.where(sel, aff, 0.0)
    csx_ref[...] = csx.astype(I32)
    cs_incl = csx + sel.astype(F32)
    for c in range(n_chunks):
        bnd_ref[pl.ds(c, 1), :] = count(cs_incl <= float(c * chunk))
        bnd_ref[pl.ds(n_chunks + c, 1), :] = count(cs_incl <= float((c + 1) * chunk - 1))


def _route(aff_packed, cap, n_chunks, chunk):
    rows = aff_packed.shape[0]
    vm = pl.BlockSpec(memory_space=pltpu.VMEM)
    return pl.pallas_call(
        functools.partial(_route_kernel, cap=cap, n_chunks=n_chunks, chunk=chunk),
        in_specs=[vm],
        out_specs=[vm, vm, vm, vm],
        out_shape=[jax.ShapeDtypeStruct((rows, LANES), I32), jax.ShapeDtypeStruct((rows, LANES), F32),
                   jax.ShapeDtypeStruct((rows, LANES), I32), jax.ShapeDtypeStruct((2 * n_chunks, LANES), I32)],
        compiler_params=pltpu.CompilerParams(vmem_limit_bytes=VMEM_LIMIT),
        name="route",
    )(aff_packed)


def _gather_kernel(tb0_ref, tb1_ref, pos_ref, h2_ref, xg_ref, acc_scr):
    e = pl.program_id(0)
    c = pl.program_id(1)
    sc = xg_ref.shape[0]
    tb = pos_ref.shape[1]
    slot = lax.broadcasted_iota(I32, (sc, tb), 0) + c * sc
    acc_scr[...] = jnp.zeros(acc_scr.shape, F32)

    def body(blk, carry):
        onehot = (pos_ref[pl.ds(blk, 1), :] == slot).astype(BF16)
        rows = h2_ref[pl.ds(pl.multiple_of(blk * tb, tb), tb), :]
        acc_scr[...] += jnp.dot(onehot, rows, preferred_element_type=F32)
        return carry

    lax.fori_loop(tb0_ref[e, c], tb1_ref[e, c] + 1, body, 0)
    xg_ref[...] = acc_scr[...].astype(BF16)


def _gather(tb0, tb1, pos_t, h2, cap):
    t = h2.shape[0]
    sc, tb = GATHER_SC, GATHER_TB
    return pl.pallas_call(
        _gather_kernel,
        grid_spec=pltpu.PrefetchScalarGridSpec(
            num_scalar_prefetch=2,
            grid=(N_EXPERTS, cap // sc),
            in_specs=[pl.BlockSpec((None, t // tb, tb), lambda e, c, *_: (e, 0, 0)),
                      pl.BlockSpec((t, D_MODEL), lambda e, c, *_: (0, 0), pipeline_mode=pl.Buffered(1))],
            out_specs=pl.BlockSpec((None, sc, D_MODEL), lambda e, c, *_: (e, c, 0)),
            scratch_shapes=[pltpu.VMEM((sc, D_MODEL), F32)]),
        out_shape=jax.ShapeDtypeStruct((N_EXPERTS, cap, D_MODEL), BF16),
        compiler_params=_cparams(("arbitrary", "arbitrary")),
        name="gather",
    )(tb0, tb1, pos_t, h2)


def _ffn_kernel(xg_ref, wg_ref, wu_ref, wd_ref, y_ref):
    f = pl.program_id(1)
    xg = xg_ref[...]
    a = jnp.dot(xg, wg_ref[...].astype(BF16), preferred_element_type=F32)
    b = jnp.dot(xg, wu_ref[...].astype(BF16), preferred_element_type=F32)
    act = (a * jax.nn.sigmoid(a) * b).astype(BF16)
    part = jnp.dot(act, wd_ref[...].astype(BF16), preferred_element_type=F32)

    @pl.when(f == 0)
    def _():
        y_ref[...] = part

    @pl.when(f != 0)
    def _():
        y_ref[...] += part


def _ffn(xg, w_gate, w_up, w_down):
    _, cap, _ = xg.shape
    tf = FFN_TF
    return pl.pallas_call(
        _ffn_kernel,
        grid=(N_EXPERTS, D_FF // tf),
        in_specs=[pl.BlockSpec((None, cap, D_MODEL), lambda e, f: (e, 0, 0)),
                  pl.BlockSpec((None, D_MODEL, tf), lambda e, f: (e, 0, f)),
                  pl.BlockSpec((None, D_MODEL, tf), lambda e, f: (e, 0, f)),
                  pl.BlockSpec((None, tf, D_MODEL), lambda e, f: (e, f, 0))],
        out_specs=pl.BlockSpec((None, cap, D_MODEL), lambda e, f: (e, 0, 0)),
        out_shape=jax.ShapeDtypeStruct((N_EXPERTS, cap, D_MODEL), F32),
        compiler_params=_cparams(("parallel", "arbitrary")),
        name="ffn",
    )(xg, w_gate, w_up, w_down)


def _combine_kernel(p0_ref, pn_ref, x1_ref, pos_ref, gate_ref, y_hbm, o_ref, win, xwin, sem, xsem, *, cap):
    i = pl.program_id(0)
    tt = x1_ref.shape[0]
    w = win.shape[1]

    def window_start(e):
        return jnp.minimum((p0_ref[i, e] // 8) * 8, cap - w)

    def window_copy(e):
        return pltpu.make_async_copy(y_hbm.at[e, pl.ds(window_start(e), w), :], win.at[e], sem.at[e])

    for e in range(N_EXPERTS):
        window_copy(e).start()

    lane = lax.broadcasted_iota(I32, (tt, w), 1)
    o_ref[...] = x1_ref[...]

    def add_rows(pcol, gcol, first_slot, lo_slot, rows):
        onehot = ((pcol - first_slot == lane) & (pcol >= lo_slot)).astype(BF16)
        hi, lo = _split_bf16(rows)
        z = jnp.dot(onehot, hi, preferred_element_type=F32) + jnp.dot(onehot, lo, preferred_element_type=F32)
        o_ref[...] += gcol * z

    for e in range(N_EXPERTS):
        window_copy(e).wait()
        pcol = pos_ref[:, e:e + 1]
        gcol = gate_ref[:, e:e + 1]
        w0 = window_start(e)
        add_rows(pcol, gcol, w0, w0, win[e])

        n_win = (pn_ref[i, e] - w0 + w - 1) // w

        def extra(k, carry, e=e, pcol=pcol, gcol=gcol, w0=w0):
            nominal = w0 + k * w
            start = jnp.minimum(nominal, cap - w)
            cp = pltpu.make_async_copy(y_hbm.at[e, pl.ds(start, w), :], xwin, xsem)
            cp.start()
            cp.wait()
            add_rows(pcol, gcol, start, nominal, xwin[...])
            return carry

        lax.fori_loop(1, n_win, extra, 0)


def _combine(p0, pn, x1, pos_tok, gate_tok, y, cap):
    t = x1.shape[0]
    tt, w = COMB_TT, COMB_W
    row = lambda i, *_: (i, 0)
    return pl.pallas_call(
        functools.partial(_combine_kernel, cap=cap),
        grid_spec=pltpu.PrefetchScalarGridSpec(
            num_scalar_prefetch=2,
            grid=(t // tt,),
            in_specs=[pl.BlockSpec((tt, D_MODEL), row), pl.BlockSpec((tt, N_EXPERTS), row),
                      pl.BlockSpec((tt, N_EXPERTS), row), pl.BlockSpec(memory_space=pl.ANY)],
            out_specs=pl.BlockSpec((tt, D_MODEL), row),
            scratch_shapes=[pltpu.VMEM((N_EXPERTS, w, D_MODEL), F32), pltpu.VMEM((w, D_MODEL), F32),
                            pltpu.SemaphoreType.DMA((N_EXPERTS,)), pltpu.SemaphoreType.DMA(())]),
        out_shape=jax.ShapeDtypeStruct((t, D_MODEL), F32),
        compiler_params=_cparams(("arbitrary",)),
        name="combine",
    )(p0, pn, x1, pos_tok, gate_tok, y)


def _layer_params(l, rel_bias, attn_norm, w_in, diff_q_norm, diff_k_norm, lambda_q1, lambda_k1, lambda_q2,
                  lambda_k2, diff_subln, mla_q_latent_norm, mla_kv_latent_norm, w_uq, w_ukv, mla_q_norm,
                  mla_k_norm, w_out, ffn_norm, w_router):
    row = lambda v: v.reshape(1, -1).astype(F32)
    w = w_in[l]
    o = 3 * DIFF_W
    pad_rope = MLA_NOPE, LANES - MLA_QK
    w_kpe = jnp.pad(w[:, o + Q_LORA + KV_LORA:], ((0, 0), pad_rope))
    uq = jnp.pad(w_uq[l].reshape(Q_LORA, MLA_HEADS, MLA_QK), ((0, 0), (0, 0), (0, LANES - MLA_QK)))
    ukv = w_ukv[l].reshape(KV_LORA, MLA_HEADS, MLA_NOPE + MLA_V)
    uk = jnp.pad(ukv[:, :, :MLA_NOPE], ((0, 0), (0, 0), (0, LANES - MLA_NOPE)))
    head_gain = lambda g: jnp.pad(g.astype(F32), (0, LANES - MLA_QK)).reshape(1, LANES)
    wr = jnp.pad(w_router[l].astype(F32), ((0, 0), (0, LANES - N_EXPERTS)))
    wr_hi = wr.astype(BF16)
    return dict(
        attn_norm=row(attn_norm[l]),
        w_qkv=w[:, :o].astype(BF16), w_cq=w[:, o:o + Q_LORA].astype(BF16),
        w_ckv=w[:, o + Q_LORA:o + Q_LORA + KV_LORA].astype(BF16), w_kpe=w_kpe.astype(BF16),
        w_uq=uq.reshape(Q_LORA, MLA_W).astype(BF16), w_uk=uk.reshape(KV_LORA, MLA_W).astype(BF16),
        w_uv=ukv[:, :, MLA_NOPE:].reshape(KV_LORA, MLA_HEADS * MLA_V).astype(BF16),
        dq_gain=row(jnp.tile(diff_q_norm[l], 2)), dk_gain=row(jnp.tile(diff_k_norm[l], 2)),
        cq_gain=row(mla_q_latent_norm[l]), ckv_gain=row(mla_kv_latent_norm[l]),
        mq_gain=head_gain(mla_q_norm[l]), mk_gain=head_gain(mla_k_norm[l]),
        subln=row(diff_subln[l]), lq1=row(lambda_q1[l]), lk1=row(lambda_k1[l]),
        lq2=row(lambda_q2[l]), lk2=row(lambda_k2[l]),
        rel_bias_t=rel_bias.astype(F32).T,
        w_o1=w_out[l][:DIFF_W].astype(BF16), w_o2=w_out[l][DIFF_W:].astype(BF16),
        ffn_norm=row(ffn_norm[l]), w_r_hi=wr_hi, w_r_lo=(wr - wr_hi.astype(F32)).astype(BF16),
    )


def _rope_tables(seq):
    half = MLA_ROPE // 2
    inv = 1.0 / (ROPE_BASE ** (jnp.arange(half, dtype=F32) / half))
    ang = jnp.arange(seq, dtype=jnp.int32).astype(F32)[:, None] * inv[None, :]
    cos, sin = jnp.cos(ang), jnp.sin(ang)
    ones = jnp.ones((seq, MLA_NOPE), F32)
    tail = LANES - MLA_QK
    cos_t = jnp.concatenate([ones, cos, cos, jnp.ones((seq, tail), F32)], axis=1)
    sin_t = jnp.concatenate([0 * ones, -sin, sin, jnp.zeros((seq, tail), F32)], axis=1)
    return cos_t, sin_t


def _layer(x, p, w_gate, w_up, w_down, lam_init):
    b, s, d = x.shape
    t = b * s
    cap = CAPACITY_FACTOR * t // N_EXPERTS
    x2d = x.reshape(t, d)

    cos_t, sin_t = _rope_tables(s)
    qd, kd, vd, qm, km, vm = _prep(x2d, s, cos_t, sin_t, p)
    shp = lambda a: a.reshape(b, s, a.shape[-1])
    bias = _bias_tiles(p["rel_bias_t"], ATT_T)
    od = _diff_attention(shp(qd), shp(kd), shp(vd), bias, p["rel_bias_t"], p, lam_init)
    om = _mla_attention(shp(qm), shp(km), shp(vm))
    x1, h2, aff = _outproj(x2d, od.reshape(t, -1), om.reshape(t, -1), p)

    groups = LANES // N_EXPERTS
    n_chunks = cap // GATHER_SC
    pos_p, gate_p, csx_p, bnd = _route(aff.reshape(t // groups, LANES), cap, n_chunks, GATHER_SC)
    pos_tok = pos_p.reshape(t, N_EXPERTS)
    gate_tok = gate_p.reshape(t, N_EXPERTS)
    csx_tok = csx_p.reshape(t, N_EXPERTS)

    bnd = bnd[:, :N_EXPERTS]
    tb0 = (bnd[:n_chunks] // GATHER_TB).T
    tb1 = (bnd[n_chunks:] // GATHER_TB).T
    pos_t = pos_tok.T.reshape(N_EXPERTS, t // GATHER_TB, GATHER_TB)
    p0 = csx_tok[::COMB_TT]
    pn = jnp.concatenate([p0[1:], jnp.full((1, N_EXPERTS), cap, I32)], axis=0)

    xg = _gather(tb0, tb1, pos_t, h2, cap)
    y = _ffn(xg, w_gate, w_up, w_down)
    out = _combine(p0, pn, x1, pos_tok, gate_tok, y, cap)
    return out.reshape(b, s, d)


def kernel(x_prompt, x_sample, rel_bias, attn_norm, w_in, diff_q_norm, diff_k_norm, lambda_q1, lambda_k1, lambda_q2, lambda_k2, diff_subln, mla_q_latent_norm, mla_kv_latent_norm, w_uq, w_ukv, mla_q_norm, mla_k_norm, w_out, ffn_norm, w_router, w_gate, w_up, w_down):
    outs = []
    for x in (x_prompt, x_sample):
        for l in range(DEPTH):
            p = _layer_params(l, rel_bias, attn_norm, w_in, diff_q_norm, diff_k_norm, lambda_q1, lambda_k1,
                              lambda_q2, lambda_k2, diff_subln, mla_q_latent_norm, mla_kv_latent_norm, w_uq,
                              w_ukv, mla_q_norm, mla_k_norm, w_out, ffn_norm, w_router)
            lam_init = 0.8 - 0.6 * math.exp(-0.3 * l)
            x = _layer(x, p, w_gate[l], w_up[l], w_down[l], lam_init)
        outs.append(x)
    return tuple(outs)
```

```python
import functools
import math

import jax
import jax.numpy as jnp
from jax import lax
from jax.experimental import pallas as pl
from jax.experimental.pallas import tpu as pltpu

F32 = jnp.float32
BF16 = jnp.bfloat16
I32 = jnp.int32

D_MODEL = 1024
DEPTH = 1
DIFF_HEADS = 4
DIFF_DK = 64
DIFF_DV = 128
MLA_HEADS = 8
MLA_NOPE = 64
MLA_ROPE = 32
MLA_QK = MLA_NOPE + MLA_ROPE
MLA_V = 64
Q_LORA = 256
KV_LORA = 128
ROPE_BASE = 10000.0
N_BUCKETS = 32
MAX_DISTANCE = 128
N_EXPERTS = 16
CAPACITY_FACTOR = 2
D_FF = 2816
EPS = 1e-6

LANES = 128
DIFF_W = DIFF_HEADS * DIFF_DV
MLA_W = MLA_HEADS * LANES
NEG_BIG = -1e30
LOG2E = math.log2(math.e)
VMEM_LIMIT = 56 * 1024 * 1024

PREP_TM = 512
ATT_T = 512
DIFF_TK = 2048
MLA_TQ = 512
MLA_TK = 2048
OUT_TM = 512
ROUTE_RB = 256
GATHER_SC = 256
GATHER_TB = 256
FFN_TF = 256
COMB_TT = 256
COMB_W = 64


def _cparams(sem, vmem=VMEM_LIMIT):
    return pltpu.CompilerParams(dimension_semantics=sem, vmem_limit_bytes=vmem)


def _rms_rows(x, gain):
    ms = jnp.mean(x * x, axis=-1, keepdims=True)
    return x * lax.rsqrt(ms + EPS) * gain


def _prep_kernel(x_ref, cos_ref, sin_ref, an_ref, wqkv_ref, wcq_ref, wckv_ref, wkpe_ref,
                 wuq_ref, wuk_ref, wuv_ref, dqg_ref, dkg_ref, cqg_ref, ckvg_ref, mqg_ref, mkg_ref,
                 qd_ref, kd_ref, vd_ref, qm_ref, km_ref, vm_ref):
    tm = x_ref.shape[0]
    hb = _rms_rows(x_ref[...], an_ref[...]).astype(BF16)

    lane = lax.broadcasted_iota(I32, (tm, LANES), 1)
    lo = lane < DIFF_DK

    qkv = jnp.dot(hb, wqkv_ref[...], preferred_element_type=F32)
    diff_scale = DIFF_DK ** -0.5 * LOG2E
    for hd in range(DIFF_HEADS):
        for off, g_ref, o_ref, scale in ((0, dqg_ref, qd_ref, diff_scale), (DIFF_W, dkg_ref, kd_ref, None)):
            blk = qkv[:, off + hd * LANES: off + (hd + 1) * LANES]
            sq = blk * blk
            s_lo = jnp.sum(jnp.where(lo, sq, 0.0), axis=-1, keepdims=True)
            s_hi = jnp.sum(jnp.where(lo, 0.0, sq), axis=-1, keepdims=True)
            r = jnp.where(lo, lax.rsqrt(s_lo / DIFF_DK + EPS), lax.rsqrt(s_hi / DIFF_DK + EPS))
            y = blk * r * g_ref[...]
            if scale is not None:
                y = y * scale
            o_ref[:, hd * LANES:(hd + 1) * LANES] = y.astype(BF16)
    vd_ref[...] = qkv[:, 2 * DIFF_W:3 * DIFF_W].astype(BF16)

    cq = _rms_rows(jnp.dot(hb, wcq_ref[...], preferred_element_type=F32), cqg_ref[...]).astype(BF16)
    ckv = _rms_rows(jnp.dot(hb, wckv_ref[...], preferred_element_type=F32), ckvg_ref[...]).astype(BF16)
    kpe = jnp.dot(hb, wkpe_ref[...], preferred_element_type=F32)
    q = jnp.dot(cq, wuq_ref[...], preferred_element_type=F32)
    k = jnp.dot(ckv, wuk_ref[...], preferred_element_type=F32)
    vm_ref[...] = jnp.dot(ckv, wuv_ref[...], preferred_element_type=F32).astype(BF16)

    cosb = cos_ref[...]
    sinb = sin_ref[...]
    first_half = (lane >= MLA_NOPE) & (lane < MLA_NOPE + MLA_ROPE // 2)
    mla_scale = MLA_QK ** -0.5 * LOG2E

    def norm_rope(blk, gain):
        ss = jnp.sum(blk * blk, axis=-1, keepdims=True)
        y = blk * lax.rsqrt(ss / MLA_QK + EPS) * gain
        partner = jnp.where(first_half,
                            pltpu.roll(y, LANES - MLA_ROPE // 2, 1),
                            pltpu.roll(y, MLA_ROPE // 2, 1))
        return y * cosb + partner * sinb

    for hd in range(MLA_HEADS):
        sl = slice(hd * LANES, (hd + 1) * LANES)
        qm_ref[:, sl] = (norm_rope(q[:, sl], mqg_ref[...]) * mla_scale).astype(BF16)
        km_ref[:, sl] = norm_rope(k[:, sl] + kpe, mkg_ref[...]).astype(BF16)


def _prep(x2d, seq, cos_t, sin_t, p):
    t = x2d.shape[0]
    tm = PREP_TM
    nseq = seq // tm
    row = lambda i: (i, 0)
    full = lambda i: (0, 0)
    wspec = lambda a: pl.BlockSpec(a.shape, full)
    weights = (p["attn_norm"], p["w_qkv"], p["w_cq"], p["w_ckv"], p["w_kpe"], p["w_uq"], p["w_uk"], p["w_uv"],
               p["dq_gain"], p["dk_gain"], p["cq_gain"], p["ckv_gain"], p["mq_gain"], p["mk_gain"])
    out_w = (DIFF_W, DIFF_W, DIFF_W, MLA_W, MLA_W, MLA_HEADS * MLA_V)
    return pl.pallas_call(
        _prep_kernel,
        grid=(t // tm,),
        in_specs=[pl.BlockSpec((tm, D_MODEL), row),
                  pl.BlockSpec((tm, LANES), lambda i: (i % nseq, 0)),
                  pl.BlockSpec((tm, LANES), lambda i: (i % nseq, 0))] + [wspec(w) for w in weights],
        out_specs=[pl.BlockSpec((tm, w), row) for w in out_w],
        out_shape=[jax.ShapeDtypeStruct((t, w), BF16) for w in out_w],
        compiler_params=_cparams(("parallel",)),
        name="prep",
    )(x2d, cos_t, sin_t, *weights)


def _bias_kernel(rb_ref, o_ref):
    hd = pl.program_id(0)
    d = pl.program_id(1)
    tq, tk = o_ref.shape
    row = lax.broadcasted_iota(I32, (tq, tk), 0)
    col = lax.broadcasted_iota(I32, (tq, tk), 1)
    rel = col - row + (d - 1) * tk
    half = N_BUCKETS // 2
    max_exact = half // 2
    n = jnp.abs(rel)
    nf = jnp.maximum(n, 1).astype(F32)
    large = max_exact + (jnp.log(nf / max_exact) / math.log(MAX_DISTANCE / max_exact)
                         * (half - max_exact)).astype(I32)
    large = jnp.minimum(large, half - 1)
    bucket = jnp.where(rel > 0, half, 0) + jnp.where(n < max_exact, n, large)
    acc = jnp.zeros((tq, tk), F32)
    for b in range(N_BUCKETS):
        acc = jnp.where(bucket == b, rb_ref[hd, b], acc)
    o_ref[...] = acc * LOG2E


def _bias_tiles(rel_bias_t, t):
    return pl.pallas_call(
        _bias_kernel,
        grid=(DIFF_HEADS, 3),
        in_specs=[pl.BlockSpec(memory_space=pltpu.SMEM)],
        out_specs=pl.BlockSpec((None, None, t, t), lambda h, d: (h, d, 0, 0)),
        out_shape=jax.ShapeDtypeStruct((DIFF_HEADS, 3, t, t), F32),
        compiler_params=_cparams(("parallel", "parallel")),
        name="bias_tiles",
    )(rel_bias_t)


def _online_update(mi, s, v, m_scr, l_scr, acc_scr):
    blocks = [s[:, i:i + LANES] for i in range(0, s.shape[1], LANES)]
    m_old = m_scr[mi]
    row_max = jnp.max(functools.reduce(jnp.maximum, blocks), axis=-1, keepdims=True)
    m_new = jnp.maximum(m_old, row_max)
    alpha = jnp.exp2(m_old - m_new)
    ps = [jnp.exp2(blk - m_new) for blk in blocks]
    l_scr[mi] = alpha * l_scr[mi] + functools.reduce(jnp.add, ps)
    p = jnp.concatenate([x.astype(BF16) for x in ps], axis=1)
    acc_scr[mi] = alpha * acc_scr[mi] + jnp.dot(p, v, preferred_element_type=F32)
    m_scr[mi] = m_new


def _normalised(mi, l_scr, acc_scr):
    return acc_scr[mi] / jnp.sum(l_scr[mi], axis=-1, keepdims=True)


def _qk(q, k):
    return lax.dot_general(q, k, (((1,), (1,)), ((), ())), preferred_element_type=F32)


def _diff_attn_kernel(rb_ref, q_ref, k_ref, v_ref, bias_ref, subln_ref, lq1_ref, lk1_ref, lq2_ref, lk2_ref,
                      o_ref, m_scr, l_scr, acc_scr, *, lam_init, tk):
    hd = pl.program_id(1)
    qi = pl.program_id(2)
    t = q_ref.shape[0]
    n = k_ref.shape[0] // t
    r = tk // t

    q = q_ref[...]
    lane = lax.broadcasted_iota(I32, q.shape, 1)
    zero = jnp.zeros_like(q)
    qs = (jnp.where(lane < DIFF_DK, q, zero), jnp.where(lane < DIFF_DK, zero, q))

    m_scr[...] = jnp.full(m_scr.shape, NEG_BIG, F32)
    l_scr[...] = jnp.zeros(l_scr.shape, F32)
    acc_scr[...] = jnp.zeros(acc_scr.shape, F32)

    def make_step(size, near):
        def step(j, carry):
            off = pl.multiple_of(j * size, size)
            k = k_ref[pl.ds(off, size), :]
            v = v_ref[pl.ds(off, size), :]
            for mi in range(2):
                s = _qk(qs[mi], k)
                if near:
                    s = s + bias_ref[j - qi + 1]
                _online_update(mi, s, v, m_scr, l_scr, acc_scr)
            return carry
        return step

    lo = jnp.maximum(qi - 1, 0)
    hi = jnp.minimum(qi + 2, n)
    lo_long = lo // r
    hi_long = (hi + r - 1) // r
    lax.fori_loop(0, lo_long, make_step(tk, False), 0)
    lax.fori_loop(lo_long * r, lo, make_step(t, False), 0)
    m_scr[...] = m_scr[...] + rb_ref[hd, N_BUCKETS // 2 - 1] * LOG2E
    lax.fori_loop(lo, hi, make_step(t, True), 0)
    m_scr[...] = m_scr[...] - rb_ref[hd, N_BUCKETS - 1] * LOG2E
    lax.fori_loop(hi, hi_long * r, make_step(t, False), 0)
    lax.fori_loop(hi_long, n // r, make_step(tk, False), 0)

    lam = (jnp.exp(jnp.sum(lq1_ref[...] * lk1_ref[...], axis=-1, keepdims=True))
           - jnp.exp(jnp.sum(lq2_ref[...] * lk2_ref[...], axis=-1, keepdims=True)) + lam_init)
    o = _normalised(0, l_scr, acc_scr) - lam * _normalised(1, l_scr, acc_scr)
    o_ref[...] = (_rms_rows(o, subln_ref[...]) * (1.0 - lam_init)).astype(BF16)


def _diff_attention(qd, kd, vd, bias_tiles, rel_bias_t, p, lam_init):
    b, s, _ = qd.shape
    t = ATT_T
    vec = lambda a: pl.BlockSpec(a.shape, lambda bi, h, i: (0, 0))
    kv_spec = pl.BlockSpec((None, s, LANES), lambda bi, h, i: (bi, 0, h))
    return pl.pallas_call(
        functools.partial(_diff_attn_kernel, lam_init=lam_init, tk=DIFF_TK),
        grid=(b, DIFF_HEADS, s // t),
        in_specs=[pl.BlockSpec(memory_space=pltpu.SMEM),
                  pl.BlockSpec((None, t, LANES), lambda bi, h, i: (bi, i, h)),
                  kv_spec, kv_spec,
                  pl.BlockSpec((None, 3, t, t), lambda bi, h, i: (h, 0, 0, 0)),
                  vec(p["subln"]), vec(p["lq1"]), vec(p["lk1"]), vec(p["lq2"]), vec(p["lk2"])],
        out_specs=pl.BlockSpec((None, t, LANES), lambda bi, h, i: (bi, i, h)),
        out_shape=jax.ShapeDtypeStruct((b, s, DIFF_W), BF16),
        scratch_shapes=[pltpu.VMEM((2, t, LANES), F32)] * 3,
        compiler_params=_cparams(("parallel", "parallel", "arbitrary")),
        name="diff_attn",
    )(rel_bias_t, qd, kd, vd, bias_tiles, p["subln"], p["lq1"], p["lk1"], p["lq2"], p["lk2"])


def _mla_attn_kernel(q_ref, k_ref, v_ref, o_ref, m_scr, l_scr, acc_scr, *, tk):
    t = q_ref.shape[0]
    n = k_ref.shape[0] // tk
    qs = (q_ref[:, :LANES], q_ref[:, LANES:])

    m_scr[...] = jnp.full(m_scr.shape, NEG_BIG, F32)
    l_scr[...] = jnp.zeros(l_scr.shape, F32)
    acc_scr[...] = jnp.zeros(acc_scr.shape, F32)

    def step(j, carry):
        off = pl.multiple_of(j * tk, tk)
        v = v_ref[pl.ds(off, tk), :]
        for mi in range(2):
            k = k_ref[pl.ds(off, tk), mi * LANES:(mi + 1) * LANES]
            _online_update(mi, _qk(qs[mi], k), v, m_scr, l_scr, acc_scr)
        return carry

    lax.fori_loop(0, n, step, 0)
    lane = lax.broadcasted_iota(I32, (t, LANES), 1)
    o = jnp.where(lane < MLA_V, _normalised(0, l_scr, acc_scr), _normalised(1, l_scr, acc_scr))
    o_ref[...] = o.astype(BF16)


def _mla_attention(qm, km, vm):
    b, s, _ = qm.shape
    t = MLA_TQ
    return pl.pallas_call(
        functools.partial(_mla_attn_kernel, tk=MLA_TK),
        grid=(b, MLA_HEADS // 2, s // t),
        in_specs=[pl.BlockSpec((None, t, 2 * LANES), lambda bi, h, i: (bi, i, h)),
                  pl.BlockSpec((None, s, 2 * LANES), lambda bi, h, i: (bi, 0, h)),
                  pl.BlockSpec((None, s, LANES), lambda bi, h, i: (bi, 0, h))],
        out_specs=pl.BlockSpec((None, t, LANES), lambda bi, h, i: (bi, i, h)),
        out_shape=jax.ShapeDtypeStruct((b, s, MLA_HEADS * MLA_V), BF16),
        scratch_shapes=[pltpu.VMEM((2, t, LANES), F32)] * 3,
        compiler_params=_cparams(("parallel", "parallel", "arbitrary")),
        name="mla_attn",
    )(qm, km, vm)


def _split_bf16(x):
    hi = x.astype(BF16)
    lo = (x - hi.astype(F32)).astype(BF16)
    return hi, lo


def _outproj_kernel(x_ref, od_ref, om_ref, wo1_ref, wo2_ref, fn_ref, wrh_ref, wrl_ref,
                    x1_ref, h2_ref, aff_ref):
    x1 = (x_ref[...]
          + jnp.dot(od_ref[...], wo1_ref[...], preferred_element_type=F32)
          + jnp.dot(om_ref[...], wo2_ref[...], preferred_element_type=F32))
    x1_ref[...] = x1
    h = _rms_rows(x1, fn_ref[...])
    hh, hl = _split_bf16(h)
    h2_ref[...] = hh
    logits = (jnp.dot(hh, wrh_ref[...], preferred_element_type=F32)
              + jnp.dot(hl, wrh_ref[...], preferred_element_type=F32)
              + jnp.dot(hh, wrl_ref[...], preferred_element_type=F32))
    lane = lax.broadcasted_iota(I32, logits.shape, 1)
    logits = jnp.where(lane < N_EXPERTS, logits, NEG_BIG)
    e = jnp.exp(logits - jnp.max(logits, axis=-1, keepdims=True))
    aff = e / jnp.sum(e, axis=-1, keepdims=True)
    aff_ref[...] = aff[:, :N_EXPERTS]


def _outproj(x2d, od, om, p):
    t = x2d.shape[0]
    tm = OUT_TM
    row = lambda i: (i, 0)
    wspec = lambda a: pl.BlockSpec(a.shape, lambda i: (0, 0))
    weights = (p["w_o1"], p["w_o2"], p["ffn_norm"], p["w_r_hi"], p["w_r_lo"])
    return pl.pallas_call(
        _outproj_kernel,
        grid=(t // tm,),
        in_specs=[pl.BlockSpec((tm, D_MODEL), row), pl.BlockSpec((tm, DIFF_W), row),
                  pl.BlockSpec((tm, MLA_HEADS * MLA_V), row)] + [wspec(w) for w in weights],
        out_specs=[pl.BlockSpec((tm, D_MODEL), row), pl.BlockSpec((tm, D_MODEL), row),
                   pl.BlockSpec((tm, N_EXPERTS), row)],
        out_shape=[jax.ShapeDtypeStruct((t, D_MODEL), F32), jax.ShapeDtypeStruct((t, D_MODEL), BF16),
                   jax.ShapeDtypeStruct((t, N_EXPERTS), F32)],
        compiler_params=_cparams(("parallel",)),
        name="outproj",
    )(x2d, od, om, *weights)


def _route_kernel(aff_ref, pos_ref, gate_ref, csx_ref, bnd_ref, *, cap, n_chunks, chunk):
    rows = aff_ref.shape[0]
    groups = LANES // N_EXPERTS
    aff = aff_ref[...]
    bits = pltpu.bitcast(aff, I32)

    def expert_total(v):
        for sh in (N_EXPERTS, 2 * N_EXPERTS, 4 * N_EXPERTS):
            v = v + pltpu.roll(v, sh, 1)
        return v

    def count(mask):
        return expert_total(jnp.sum(mask.astype(I32), axis=0, keepdims=True))

    def search(i, thr):
        cand = thr | jnp.left_shift(jnp.int32(1), 30 - i)
        return jnp.where(count(bits >= cand) >= cap, cand, thr)

    thr = lax.fori_loop(0, 31, search, jnp.zeros((1, LANES), I32))

    a = lax.broadcasted_iota(I32, (LANES, LANES), 0)
    b = lax.broadcasted_iota(I32, (LANES, LANES), 1)
    same = (a % N_EXPERTS) == (b % N_EXPERTS)
    q_all = same.astype(BF16)
    q_before = (same & (a // N_EXPERTS < b // N_EXPERTS)).astype(BF16)
    rb = ROUTE_RB
    ra = lax.broadcasted_iota(I32, (rb, rb), 0)
    ca = lax.broadcasted_iota(I32, (rb, rb), 1)
    tri = (ca < ra).astype(BF16)

    def prefix(mask):
        mb = mask.astype(BF16)
        outs = []
        offset = jnp.zeros((1, LANES), F32)
        for r0 in range(0, rows, rb):
            blk = mb[r0:r0 + rb]
            tot = jnp.dot(blk, q_all, preferred_element_type=F32)
            within = jnp.dot(blk, q_before, preferred_element_type=F32)
            above = jnp.dot(tri, tot.astype(BF16), preferred_element_type=F32)
            outs.append(above + within + offset)
            offset = offset + jnp.sum(tot, axis=0, keepdims=True)
        return jnp.concatenate(outs, axis=0)

    gt = bits > thr
    eq = bits == thr
    need = (cap - count(gt)).astype(F32)
    sel = gt | (eq & (prefix(eq) < need))
    csx = prefix(sel)
    pos_ref[...] = jnp.where(sel, csx.astype(I32), -1)
    gate_ref[...] = jnp.where(sel, aff, 0.0)
    csx_ref[...] = csx.astype(I32)
    cs_incl = csx + sel.astype(F32)
    for c in range(n_chunks):
        bnd_ref[pl.ds(c, 1), :] = count(cs_incl <= float(c * chunk))
        bnd_ref[pl.ds(n_chunks + c, 1), :] = count(cs_incl <= float((c + 1) * chunk - 1))


def _route(aff_packed, cap, n_chunks, chunk):
    rows = aff_packed.shape[0]
    vm = pl.BlockSpec(memory_space=pltpu.VMEM)
    return pl.pallas_call(
        functools.partial(_route_kernel, cap=cap, n_chunks=n_chunks, chunk=chunk),
        in_specs=[vm],
        out_specs=[vm, vm, vm, vm],
        out_shape=[jax.ShapeDtypeStruct((rows, LANES), I32), jax.ShapeDtypeStruct((rows, LANES), F32),
                   jax.ShapeDtypeStruct((rows, LANES), I32), jax.ShapeDtypeStruct((2 * n_chunks, LANES), I32)],
        compiler_params=pltpu.CompilerParams(vmem_limit_bytes=VMEM_LIMIT),
        name="route",
    )(aff_packed)


def _gather_kernel(tb0_ref, tb1_ref, pos_ref, h2_ref, xg_ref, acc_scr):
    e = pl.program_id(0)
    c = pl.program_id(1)
    sc = xg_ref.shape[0]
    tb = pos_ref.shape[1]
    slot = lax.broadcasted_iota(I32, (sc, tb), 0) + c * sc
    acc_scr[...] = jnp.zeros(acc_scr.shape, F32)

    def body(blk, carry):
        onehot = (pos_ref[pl.ds(blk, 1), :] == slot).astype(BF16)
        rows = h2_ref[pl.ds(pl.multiple_of(blk * tb, tb), tb), :]
        acc_scr[...] += jnp.dot(onehot, rows, preferred_element_type=F32)
        return carry

    lax.fori_loop(tb0_ref[e, c], tb1_ref[e, c] + 1, body, 0)
    xg_ref[...] = acc_scr[...].astype(BF16)


def _gather(tb0, tb1, pos_t, h2, cap):
    t = h2.shape[0]
    sc, tb = GATHER_SC, GATHER_TB
    return pl.pallas_call(
        _gather_kernel,
        grid_spec=pltpu.PrefetchScalarGridSpec(
            num_scalar_prefetch=2,
            grid=(N_EXPERTS, cap // sc),
            in_specs=[pl.BlockSpec((None, t // tb, tb), lambda e, c, *_: (e, 0, 0)),
                      pl.BlockSpec((t, D_MODEL), lambda e, c, *_: (0, 0), pipeline_mode=pl.Buffered(1))],
            out_specs=pl.BlockSpec((None, sc, D_MODEL), lambda e, c, *_: (e, c, 0)),
            scratch_shapes=[pltpu.VMEM((sc, D_MODEL), F32)]),
        out_shape=jax.ShapeDtypeStruct((N_EXPERTS, cap, D_MODEL), BF16),
        compiler_params=_cparams(("arbitrary", "arbitrary")),
        name="gather",
    )(tb0, tb1, pos_t, h2)


def _ffn_kernel(xg_ref, wg_ref, wu_ref, wd_ref, y_ref):
    f = pl.program_id(1)
    xg = xg_ref[...]
    a = jnp.dot(xg, wg_ref[...].astype(BF16), preferred_element_type=F32)
    b = jnp.dot(xg, wu_ref[...].astype(BF16), preferred_element_type=F32)
    act = (a * jax.nn.sigmoid(a) * b).astype(BF16)
    part = jnp.dot(act, wd_ref[...].astype(BF16), preferred_element_type=F32)

    @pl.when(f == 0)
    def _():
        y_ref[...] = part

    @pl.when(f != 0)
    def _():
        y_ref[...] += part


def _ffn(xg, w_gate, w_up, w_down):
    _, cap, _ = xg.shape
    tf = FFN_TF
    return pl.pallas_call(
        _ffn_kernel,
        grid=(N_EXPERTS, D_FF // tf),
        in_specs=[pl.BlockSpec((None, cap, D_MODEL), lambda e, f: (e, 0, 0)),
                  pl.BlockSpec((None, D_MODEL, tf), lambda e, f: (e, 0, f)),
                  pl.BlockSpec((None, D_MODEL, tf), lambda e, f: (e, 0, f)),
                  pl.BlockSpec((None, tf, D_MODEL), lambda e, f: (e, f, 0))],
        out_specs=pl.BlockSpec((None, cap, D_MODEL), lambda e, f: (e, 0, 0)),
        out_shape=jax.ShapeDtypeStruct((N_EXPERTS, cap, D_MODEL), F32),
        compiler_params=_cparams(("parallel", "arbitrary")),
        name="ffn",
    )(xg, w_gate, w_up, w_down)


def _combine_kernel(p0_ref, pn_ref, x1_ref, pos_ref, gate_ref, y_hbm, o_ref, win, xwin, sem, xsem, *, cap):
    i = pl.program_id(0)
    tt = x1_ref.shape[0]
    w = win.shape[1]

    def window_start(e):
        return jnp.minimum((p0_ref[i, e] // 8) * 8, cap - w)

    def window_copy(e):
        return pltpu.make_async_copy(y_hbm.at[e, pl.ds(window_start(e), w), :], win.at[e], sem.at[e])

    for e in range(N_EXPERTS):
        window_copy(e).start()

    lane = lax.broadcasted_iota(I32, (tt, w), 1)
    o_ref[...] = x1_ref[...]

    def add_rows(pcol, gcol, first_slot, lo_slot, rows):
        onehot = ((pcol - first_slot == lane) & (pcol >= lo_slot)).astype(BF16)
        hi, lo = _split_bf16(rows)
        z = jnp.dot(onehot, hi, preferred_element_type=F32) + jnp.dot(onehot, lo, preferred_element_type=F32)
        o_ref[...] += gcol * z

    for e in range(N_EXPERTS):
        window_copy(e).wait()
        pcol = pos_ref[:, e:e + 1]
        gcol = gate_ref[:, e:e + 1]
        w0 = window_start(e)
        add_rows(pcol, gcol, w0, w0, win[e])

        n_win = (pn_ref[i, e] - w0 + w - 1) // w

        def extra(k, carry, e=e, pcol=pcol, gcol=gcol, w0=w0):
            nominal = w0 + k * w
            start = jnp.minimum(nominal, cap - w)
            cp = pltpu.make_async_copy(y_hbm.at[e, pl.ds(start, w), :], xwin, xsem)
            cp.start()
            cp.wait()
            add_rows(pcol, gcol, start, nominal, xwin[...])
            return carry

        lax.fori_loop(1, n_win, extra, 0)


def _combine(p0, pn, x1, pos_tok, gate_tok, y, cap):
    t = x1.shape[0]
    tt, w = COMB_TT, COMB_W
    row = lambda i, *_: (i, 0)
    return pl.pallas_call(
        functools.partial(_combine_kernel, cap=cap),
        grid_spec=pltpu.PrefetchScalarGridSpec(
            num_scalar_prefetch=2,
            grid=(t // tt,),
            in_specs=[pl.BlockSpec((tt, D_MODEL), row), pl.BlockSpec((tt, N_EXPERTS), row),
                      pl.BlockSpec((tt, N_EXPERTS), row), pl.BlockSpec(memory_space=pl.ANY)],
            out_specs=pl.BlockSpec((tt, D_MODEL), row),
            scratch_shapes=[pltpu.VMEM((N_EXPERTS, w, D_MODEL), F32), pltpu.VMEM((w, D_MODEL), F32),
                            pltpu.SemaphoreType.DMA((N_EXPERTS,)), pltpu.SemaphoreType.DMA(())]),
        out_shape=jax.ShapeDtypeStruct((t, D_MODEL), F32),
        compiler_params=_cparams(("arbitrary",)),
        name="combine",
    )(p0, pn, x1, pos_tok, gate_tok, y)


def _layer_params(l, rel_bias, attn_norm, w_in, diff_q_norm, diff_k_norm, lambda_q1, lambda_k1, lambda_q2,
                  lambda_k2, diff_subln, mla_q_latent_norm, mla_kv_latent_norm, w_uq, w_ukv, mla_q_norm,
                  mla_k_norm, w_out, ffn_norm, w_router):
    row = lambda v: v.reshape(1, -1).astype(F32)
    w = w_in[l]
    o = 3 * DIFF_W
    pad_rope = MLA_NOPE, LANES - MLA_QK
    w_kpe = jnp.pad(w[:, o + Q_LORA + KV_LORA:], ((0, 0), pad_rope))
    uq = jnp.pad(w_uq[l].reshape(Q_LORA, MLA_HEADS, MLA_QK), ((0, 0), (0, 0), (0, LANES - MLA_QK)))
    ukv = w_ukv[l].reshape(KV_LORA, MLA_HEADS, MLA_NOPE + MLA_V)
    uk = jnp.pad(ukv[:, :, :MLA_NOPE], ((0, 0), (0, 0), (0, LANES - MLA_NOPE)))
    head_gain = lambda g: jnp.pad(g.astype(F32), (0, LANES - MLA_QK)).reshape(1, LANES)
    wr = jnp.pad(w_router[l].astype(F32), ((0, 0), (0, LANES - N_EXPERTS)))
    wr_hi = wr.astype(BF16)
    return dict(
        attn_norm=row(attn_norm[l]),
        w_qkv=w[:, :o].astype(BF16), w_cq=w[:, o:o + Q_LORA].astype(BF16),
        w_ckv=w[:, o + Q_LORA:o + Q_LORA + KV_LORA].astype(BF16), w_kpe=w_kpe.astype(BF16),
        w_uq=uq.reshape(Q_LORA, MLA_W).astype(BF16), w_uk=uk.reshape(KV_LORA, MLA_W).astype(BF16),
        w_uv=ukv[:, :, MLA_NOPE:].reshape(KV_LORA, MLA_HEADS * MLA_V).astype(BF16),
        dq_gain=row(jnp.tile(diff_q_norm[l], 2)), dk_gain=row(jnp.tile(diff_k_norm[l], 2)),
        cq_gain=row(mla_q_latent_norm[l]), ckv_gain=row(mla_kv_latent_norm[l]),
        mq_gain=head_gain(mla_q_norm[l]), mk_gain=head_gain(mla_k_norm[l]),
        subln=row(diff_subln[l]), lq1=row(lambda_q1[l]), lk1=row(lambda_k1[l]),
        lq2=row(lambda_q2[l]), lk2=row(lambda_k2[l]),
        rel_bias_t=rel_bias.astype(F32).T,
        w_o1=w_out[l][:DIFF_W].astype(BF16), w_o2=w_out[l][DIFF_W:].astype(BF16),
        ffn_norm=row(ffn_norm[l]), w_r_hi=wr_hi, w_r_lo=(wr - wr_hi.astype(F32)).astype(BF16),
    )


def _rope_tables(seq):
    half = MLA_ROPE // 2
    inv = 1.0 / (ROPE_BASE ** (jnp.arange(half, dtype=F32) / half))
    ang = jnp.arange(seq, dtype=jnp.int32).astype(F32)[:, None] * inv[None, :]
    cos, sin = jnp.cos(ang), jnp.sin(ang)
    ones = jnp.ones((seq, MLA_NOPE), F32)
    tail = LANES - MLA_QK
    cos_t = jnp.concatenate([ones, cos, cos, jnp.ones((seq, tail), F32)], axis=1)
    sin_t = jnp.concatenate([0 * ones, -sin, sin, jnp.zeros((seq, tail), F32)], axis=1)
    return cos_t, sin_t


def _layer(x, p, w_gate, w_up, w_down, lam_init):
    b, s, d = x.shape
    t = b * s
    cap = CAPACITY_FACTOR * t // N_EXPERTS
    x2d = x.reshape(t, d)

    cos_t, sin_t = _rope_tables(s)
    qd, kd, vd, qm, km, vm = _prep(x2d, s, cos_t, sin_t, p)
    shp = lambda a: a.reshape(b, s, a.shape[-1])
    bias = _bias_tiles(p["rel_bias_t"], ATT_T)
    od = _diff_attention(shp(qd), shp(kd), shp(vd), bias, p["rel_bias_t"], p, lam_init)
    om = _mla_attention(shp(qm), shp(km), shp(vm))
    x1, h2, aff = _outproj(x2d, od.reshape(t, -1), om.reshape(t, -1), p)

    groups = LANES // N_EXPERTS
    n_chunks = cap // GATHER_SC
    pos_p, gate_p, csx_p, bnd = _route(aff.reshape(t // groups, LANES), cap, n_chunks, GATHER_SC)
    pos_tok = pos_p.reshape(t, N_EXPERTS)
    gate_tok = gate_p.reshape(t, N_EXPERTS)
    csx_tok = csx_p.reshape(t, N_EXPERTS)

    bnd = bnd[:, :N_EXPERTS]
    tb0 = (bnd[:n_chunks] // GATHER_TB).T
    tb1 = (bnd[n_chunks:] // GATHER_TB).T
    pos_t = pos_tok.T.reshape(N_EXPERTS, t // GATHER_TB, GATHER_TB)
    p0 = csx_tok[::COMB_TT]
    pn = jnp.concatenate([p0[1:], jnp.full((1, N_EXPERTS), cap, I32)], axis=0)

    xg = _gather(tb0, tb1, pos_t, h2, cap)
    y = _ffn(xg, w_gate, w_up, w_down)
    out = _combine(p0, pn, x1, pos_tok, gate_tok, y, cap)
    return out.reshape(b, s, d)


def kernel(x_prompt, x_sample, rel_bias, attn_norm, w_in, diff_q_norm, diff_k_norm, lambda_q1, lambda_k1, lambda_q2, lambda_k2, diff_subln, mla_q_latent_norm, mla_kv_latent_norm, w_uq, w_ukv, mla_q_norm, mla_k_norm, w_out, ffn_norm, w_router, w_gate, w_up, w_down):
    outs = []
    for x in (x_prompt, x_sample):
        for l in range(DEPTH):
            p = _layer_params(l, rel_bias, attn_norm, w_in, diff_q_norm, diff_k_norm, lambda_q1, lambda_k1,
                              lambda_q2, lambda_k2, diff_subln, mla_q_latent_norm, mla_kv_latent_norm, w_uq,
                              w_ukv, mla_q_norm, mla_k_norm, w_out, ffn_norm, w_router)
            lam_init = 0.8 - 0.6 * math.exp(-0.3 * l)
            x = _layer(x, p, w_gate[l], w_up[l], w_down[l], lam_init)
        outs.append(x)
    return tuple(outs)
```

```python
import functools
import math

import jax
import jax.numpy as jnp
from jax import lax
from jax.experimental import pallas as pl
from jax.experimental.pallas import tpu as pltpu

F32 = jnp.float32
BF16 = jnp.bfloat16
I32 = jnp.int32

D_MODEL = 1024
DEPTH = 1
DIFF_HEADS = 4
DIFF_DK = 64
DIFF_DV = 128
MLA_HEADS = 8
MLA_NOPE = 64
MLA_ROPE = 32
MLA_QK = MLA_NOPE + MLA_ROPE
MLA_V = 64
Q_LORA = 256
KV_LORA = 128
ROPE_BASE = 10000.0
N_BUCKETS = 32
MAX_DISTANCE = 128
N_EXPERTS = 16
CAPACITY_FACTOR = 2
D_FF = 2816
EPS = 1e-6

LANES = 128
SUBLANES = 8
BF16_ROWS = 16
DIFF_W = DIFF_HEADS * DIFF_DV
MLA_W = MLA_HEADS * LANES
NEG_BIG = -1e30
LOG2E = math.log2(math.e)
VMEM_LIMIT = 56 * 1024 * 1024

PREP_TM = 512
ATT_T = 512
DIFF_TK = 2048
MLA_TQ = 512
MLA_TK = 2048
OUT_TM = 512
ROUTE_RB = 256
GATHER_TG = 1024
GATHER_W = 192
FFN_TF = 256
COMB_TT = 512
COMB_W = 128


def _cparams(sem, vmem=VMEM_LIMIT):
    return pltpu.CompilerParams(dimension_semantics=sem, vmem_limit_bytes=vmem)


def _rms_rows(x, gain):
    ms = jnp.mean(x * x, axis=-1, keepdims=True)
    return x * lax.rsqrt(ms + EPS) * gain


def _prep_kernel(x_ref, cos_ref, sin_ref, an_ref, wqkv_ref, wcq_ref, wckv_ref, wkpe_ref,
                 wuq_ref, wuk_ref, wuv_ref, dqg_ref, dkg_ref, cqg_ref, ckvg_ref, mqg_ref, mkg_ref,
                 qd_ref, kd_ref, vd_ref, qm_ref, km_ref, vm_ref):
    tm = x_ref.shape[0]
    hb = _rms_rows(x_ref[...], an_ref[...]).astype(BF16)

    lane = lax.broadcasted_iota(I32, (tm, LANES), 1)
    lo = lane < DIFF_DK

    qkv = jnp.dot(hb, wqkv_ref[...], preferred_element_type=F32)
    diff_scale = DIFF_DK ** -0.5 * LOG2E
    for hd in range(DIFF_HEADS):
        for off, g_ref, o_ref, scale in ((0, dqg_ref, qd_ref, diff_scale), (DIFF_W, dkg_ref, kd_ref, None)):
            blk = qkv[:, off + hd * LANES: off + (hd + 1) * LANES]
            sq = blk * blk
            s_lo = jnp.sum(jnp.where(lo, sq, 0.0), axis=-1, keepdims=True)
            s_hi = jnp.sum(jnp.where(lo, 0.0, sq), axis=-1, keepdims=True)
            r = jnp.where(lo, lax.rsqrt(s_lo / DIFF_DK + EPS), lax.rsqrt(s_hi / DIFF_DK + EPS))
            y = blk * r * g_ref[...]
            if scale is not None:
                y = y * scale
            o_ref[:, hd * LANES:(hd + 1) * LANES] = y.astype(BF16)
    vd_ref[...] = qkv[:, 2 * DIFF_W:3 * DIFF_W].astype(BF16)

    cq = _rms_rows(jnp.dot(hb, wcq_ref[...], preferred_element_type=F32), cqg_ref[...]).astype(BF16)
    ckv = _rms_rows(jnp.dot(hb, wckv_ref[...], preferred_element_type=F32), ckvg_ref[...]).astype(BF16)
    kpe = jnp.dot(hb, wkpe_ref[...], preferred_element_type=F32)
    q = jnp.dot(cq, wuq_ref[...], preferred_element_type=F32)
    k = jnp.dot(ckv, wuk_ref[...], preferred_element_type=F32)
    vm_ref[...] = jnp.dot(ckv, wuv_ref[...], preferred_element_type=F32).astype(BF16)

    cosb = cos_ref[...]
    sinb = sin_ref[...]
    first_half = (lane >= MLA_NOPE) & (lane < MLA_NOPE + MLA_ROPE // 2)
    mla_scale = MLA_QK ** -0.5 * LOG2E

    def norm_rope(blk, gain):
        ss = jnp.sum(blk * blk, axis=-1, keepdims=True)
        y = blk * lax.rsqrt(ss / MLA_QK + EPS) * gain
        partner = jnp.where(first_half,
                            pltpu.roll(y, LANES - MLA_ROPE // 2, 1),
                            pltpu.roll(y, MLA_ROPE // 2, 1))
        return y * cosb + partner * sinb

    for hd in range(MLA_HEADS):
        sl = slice(hd * LANES, (hd + 1) * LANES)
        qm_ref[:, sl] = (norm_rope(q[:, sl], mqg_ref[...]) * mla_scale).astype(BF16)
        km_ref[:, sl] = norm_rope(k[:, sl] + kpe, mkg_ref[...]).astype(BF16)


def _prep(x2d, seq, cos_t, sin_t, p):
    t = x2d.shape[0]
    tm = PREP_TM
    nseq = seq // tm
    row = lambda i: (i, 0)
    full = lambda i: (0, 0)
    wspec = lambda a: pl.BlockSpec(a.shape, full)
    weights = (p["attn_norm"], p["w_qkv"], p["w_cq"], p["w_ckv"], p["w_kpe"], p["w_uq"], p["w_uk"], p["w_uv"],
               p["dq_gain"], p["dk_gain"], p["cq_gain"], p["ckv_gain"], p["mq_gain"], p["mk_gain"])
    out_w = (DIFF_W, DIFF_W, DIFF_W, MLA_W, MLA_W, MLA_HEADS * MLA_V)
    return pl.pallas_call(
        _prep_kernel,
        grid=(t // tm,),
        in_specs=[pl.BlockSpec((tm, D_MODEL), row),
                  pl.BlockSpec((tm, LANES), lambda i: (i % nseq, 0)),
                  pl.BlockSpec((tm, LANES), lambda i: (i % nseq, 0))] + [wspec(w) for w in weights],
        out_specs=[pl.BlockSpec((tm, w), row) for w in out_w],
        out_shape=[jax.ShapeDtypeStruct((t, w), BF16) for w in out_w],
        compiler_params=_cparams(("parallel",)),
        name="prep",
    )(x2d, cos_t, sin_t, *weights)


def _bias_kernel(rb_ref, o_ref):
    hd = pl.program_id(0)
    d = pl.program_id(1)
    tq, tk = o_ref.shape
    row = lax.broadcasted_iota(I32, (tq, tk), 0)
    col = lax.broadcasted_iota(I32, (tq, tk), 1)
    rel = col - row + (d - 1) * tk
    half = N_BUCKETS // 2
    max_exact = half // 2
    n = jnp.abs(rel)
    nf = jnp.maximum(n, 1).astype(F32)
    large = max_exact + (jnp.log(nf / max_exact) / math.log(MAX_DISTANCE / max_exact)
                         * (half - max_exact)).astype(I32)
    large = jnp.minimum(large, half - 1)
    bucket = jnp.where(rel > 0, half, 0) + jnp.where(n < max_exact, n, large)
    acc = jnp.zeros((tq, tk), F32)
    for b in range(N_BUCKETS):
        acc = jnp.where(bucket == b, rb_ref[hd, b], acc)
    o_ref[...] = acc * LOG2E


def _bias_tiles(rel_bias_t, t):
    return pl.pallas_call(
        _bias_kernel,
        grid=(DIFF_HEADS, 3),
        in_specs=[pl.BlockSpec(memory_space=pltpu.SMEM)],
        out_specs=pl.BlockSpec((None, None, t, t), lambda h, d: (h, d, 0, 0)),
        out_shape=jax.ShapeDtypeStruct((DIFF_HEADS, 3, t, t), F32),
        compiler_params=_cparams(("parallel", "parallel")),
        name="bias_tiles",
    )(rel_bias_t)


def _online_update(mi, s, v, m_scr, l_scr, acc_scr):
    blocks = [s[:, i:i + LANES] for i in range(0, s.shape[1], LANES)]
    m_old = m_scr[mi]
    row_max = jnp.max(functools.reduce(jnp.maximum, blocks), axis=-1, keepdims=True)
    m_new = jnp.maximum(m_old, row_max)
    alpha = jnp.exp2(m_old - m_new)
    ps = [jnp.exp2(blk - m_new) for blk in blocks]
    l_scr[mi] = alpha * l_scr[mi] + functools.reduce(jnp.add, ps)
    p = jnp.concatenate([x.astype(BF16) for x in ps], axis=1)
    acc_scr[mi] = alpha * acc_scr[mi] + jnp.dot(p, v, preferred_element_type=F32)
    m_scr[mi] = m_new


def _normalised(mi, l_scr, acc_scr):
    return acc_scr[mi] / jnp.sum(l_scr[mi], axis=-1, keepdims=True)


def _qk(q, k):
    return lax.dot_general(q, k, (((1,), (1,)), ((), ())), preferred_element_type=F32)


def _diff_attn_kernel(rb_ref, q_ref, k_ref, v_ref, bias_ref, subln_ref, lq1_ref, lk1_ref, lq2_ref, lk2_ref,
                      o_ref, m_scr, l_scr, acc_scr, *, lam_init, tk):
    hd = pl.program_id(1)
    qi = pl.program_id(2)
    t = q_ref.shape[0]
    n = k_ref.shape[0] // t
    r = tk // t

    q = q_ref[...]
    lane = lax.broadcasted_iota(I32, q.shape, 1)
    zero = jnp.zeros_like(q)
    qs = (jnp.where(lane < DIFF_DK, q, zero), jnp.where(lane < DIFF_DK, zero, q))

    m_scr[...] = jnp.full(m_scr.shape, NEG_BIG, F32)
    l_scr[...] = jnp.zeros(l_scr.shape, F32)
    acc_scr[...] = jnp.zeros(acc_scr.shape, F32)

    def make_step(size, near):
        def step(j, carry):
            off = pl.multiple_of(j * size, size)
            k = k_ref[pl.ds(off, size), :]
            v = v_ref[pl.ds(off, size), :]
            for mi in range(2):
                s = _qk(qs[mi], k)
                if near:
                    s = s + bias_ref[j - qi + 1]
                _online_update(mi, s, v, m_scr, l_scr, acc_scr)
            return carry
        return step

    lo = jnp.maximum(qi - 1, 0)
    hi = jnp.minimum(qi + 2, n)
    lo_long = lo // r
    hi_long = (hi + r - 1) // r
    lax.fori_loop(0, lo_long, make_step(tk, False), 0)
    lax.fori_loop(lo_long * r, lo, make_step(t, False), 0)
    m_scr[...] = m_scr[...] + rb_ref[hd, N_BUCKETS // 2 - 1] * LOG2E
    lax.fori_loop(lo, hi, make_step(t, True), 0)
    m_scr[...] = m_scr[...] - rb_ref[hd, N_BUCKETS - 1] * LOG2E
    lax.fori_loop(hi, hi_long * r, make_step(t, False), 0)
    lax.fori_loop(hi_long, n // r, make_step(tk, False), 0)

    lam = (jnp.exp(jnp.sum(lq1_ref[...] * lk1_ref[...], axis=-1, keepdims=True))
           - jnp.exp(jnp.sum(lq2_ref[...] * lk2_ref[...], axis=-1, keepdims=True)) + lam_init)
    o = _normalised(0, l_scr, acc_scr) - lam * _normalised(1, l_scr, acc_scr)
    o_ref[...] = (_rms_rows(o, subln_ref[...]) * (1.0 - lam_init)).astype(BF16)


def _diff_attention(qd, kd, vd, bias_tiles, rel_bias_t, p, lam_init):
    b, s, _ = qd.shape
    t = ATT_T
    vec = lambda a: pl.BlockSpec(a.shape, lambda bi, h, i: (0, 0))
    kv_spec = pl.BlockSpec((None, s, LANES), lambda bi, h, i: (bi, 0, h))
    return pl.pallas_call(
        functools.partial(_diff_attn_kernel, lam_init=lam_init, tk=DIFF_TK),
        grid=(b, DIFF_HEADS, s // t),
        in_specs=[pl.BlockSpec(memory_space=pltpu.SMEM),
                  pl.BlockSpec((None, t, LANES), lambda bi, h, i: (bi, i, h)),
                  kv_spec, kv_spec,
                  pl.BlockSpec((None, 3, t, t), lambda bi, h, i: (h, 0, 0, 0)),
                  vec(p["subln"]), vec(p["lq1"]), vec(p["lk1"]), vec(p["lq2"]), vec(p["lk2"])],
        out_specs=pl.BlockSpec((None, t, LANES), lambda bi, h, i: (bi, i, h)),
        out_shape=jax.ShapeDtypeStruct((b, s, DIFF_W), BF16),
        scratch_shapes=[pltpu.VMEM((2, t, LANES), F32)] * 3,
        compiler_params=_cparams(("parallel", "parallel", "arbitrary")),
        name="diff_attn",
    )(rel_bias_t, qd, kd, vd, bias_tiles, p["subln"], p["lq1"], p["lk1"], p["lq2"], p["lk2"])


def _mla_attn_kernel(q_ref, k_ref, v_ref, o_ref, m_scr, l_scr, acc_scr, *, tk):
    t = q_ref.shape[0]
    n = k_ref.shape[0] // tk
    qs = (q_ref[:, :LANES], q_ref[:, LANES:])

    m_scr[...] = jnp.full(m_scr.shape, NEG_BIG, F32)
    l_scr[...] = jnp.zeros(l_scr.shape, F32)
    acc_scr[...] = jnp.zeros(acc_scr.shape, F32)

    def step(j, carry):
        off = pl.multiple_of(j * tk, tk)
        v = v_ref[pl.ds(off, tk), :]
        for mi in range(2):
            k = k_ref[pl.ds(off, tk), mi * LANES:(mi + 1) * LANES]
            _online_update(mi, _qk(qs[mi], k), v, m_scr, l_scr, acc_scr)
        return carry

    lax.fori_loop(0, n, step, 0)
    lane = lax.broadcasted_iota(I32, (t, LANES), 1)
    o = jnp.where(lane < MLA_V, _normalised(0, l_scr, acc_scr), _normalised(1, l_scr, acc_scr))
    o_ref[...] = o.astype(BF16)


def _mla_attention(qm, km, vm):
    b, s, _ = qm.shape
    t = MLA_TQ
    return pl.pallas_call(
        functools.partial(_mla_attn_kernel, tk=MLA_TK),
        grid=(b, MLA_HEADS // 2, s // t),
        in_specs=[pl.BlockSpec((None, t, 2 * LANES), lambda bi, h, i: (bi, i, h)),
                  pl.BlockSpec((None, s, 2 * LANES), lambda bi, h, i: (bi, 0, h)),
                  pl.BlockSpec((None, s, LANES), lambda bi, h, i: (bi, 0, h))],
        out_specs=pl.BlockSpec((None, t, LANES), lambda bi, h, i: (bi, i, h)),
        out_shape=jax.ShapeDtypeStruct((b, s, MLA_HEADS * MLA_V), BF16),
        scratch_shapes=[pltpu.VMEM((2, t, LANES), F32)] * 3,
        compiler_params=_cparams(("parallel", "parallel", "arbitrary")),
        name="mla_attn",
    )(qm, km, vm)


def _split_bf16(x):
    hi = x.astype(BF16)
    lo = (x - hi.astype(F32)).astype(BF16)
    return hi, lo


def _outproj_kernel(x_ref, od_ref, om_ref, wo1_ref, wo2_ref, fn_ref, wrh_ref, wrl_ref,
                    x1_ref, h2_ref, aff_ref):
    x1 = (x_ref[...]
          + jnp.dot(od_ref[...], wo1_ref[...], preferred_element_type=F32)
          + jnp.dot(om_ref[...], wo2_ref[...], preferred_element_type=F32))
    x1_ref[...] = x1
    h = _rms_rows(x1, fn_ref[...])
    hh, hl = _split_bf16(h)
    h2_ref[...] = hh
    logits = (jnp.dot(hh, wrh_ref[...], preferred_element_type=F32)
              + jnp.dot(hl, wrh_ref[...], preferred_element_type=F32)
              + jnp.dot(hh, wrl_ref[...], preferred_element_type=F32))
    lane = lax.broadcasted_iota(I32, logits.shape, 1)
    logits = jnp.where(lane < N_EXPERTS, logits, NEG_BIG)
    e = jnp.exp(logits - jnp.max(logits, axis=-1, keepdims=True))
    aff = e / jnp.sum(e, axis=-1, keepdims=True)
    aff_ref[...] = aff[:, :N_EXPERTS]


def _outproj(x2d, od, om, p):
    t = x2d.shape[0]
    tm = OUT_TM
    row = lambda i: (i, 0)
    wspec = lambda a: pl.BlockSpec(a.shape, lambda i: (0, 0))
    weights = (p["w_o1"], p["w_o2"], p["ffn_norm"], p["w_r_hi"], p["w_r_lo"])
    return pl.pallas_call(
        _outproj_kernel,
        grid=(t // tm,),
        in_specs=[pl.BlockSpec((tm, D_MODEL), row), pl.BlockSpec((tm, DIFF_W), row),
                  pl.BlockSpec((tm, MLA_HEADS * MLA_V), row)] + [wspec(w) for w in weights],
        out_specs=[pl.BlockSpec((tm, D_MODEL), row), pl.BlockSpec((tm, D_MODEL), row),
                   pl.BlockSpec((tm, N_EXPERTS), row)],
        out_shape=[jax.ShapeDtypeStruct((t, D_MODEL), F32), jax.ShapeDtypeStruct((t, D_MODEL), BF16),
                   jax.ShapeDtypeStruct((t, N_EXPERTS), F32)],
        compiler_params=_cparams(("parallel",)),
        name="outproj",
    )(x2d, od, om, *weights)


def _route_kernel(aff_ref, pos_ref, gate_ref, csx_ref, *, cap):
    rows = aff_ref.shape[0]
    groups = LANES // N_EXPERTS
    aff = aff_ref[...]
    bits = pltpu.bitcast(aff, I32)

    def expert_total(v):
        for sh in (N_EXPERTS, 2 * N_EXPERTS, 4 * N_EXPERTS):
            v = v + pltpu.roll(v, sh, 1)
        return v

    def count(mask):
        return expert_total(jnp.sum(mask.astype(I32), axis=0, keepdims=True))

    def search(i, thr):
        cand = thr | jnp.left_shift(jnp.int32(1), 30 - i)
        return jnp.where(count(bits >= cand) >= cap, cand, thr)

    thr = lax.fori_loop(0, 31, search, jnp.zeros((1, LANES), I32))

    a = lax.broadcasted_iota(I32, (LANES, LANES), 0)
    b = lax.broadcasted_iota(I32, (LANES, LANES), 1)
    same = (a % N_EXPERTS) == (b % N_EXPERTS)
    q_all = same.astype(BF16)
    q_before = (same & (a // N_EXPERTS < b // N_EXPERTS)).astype(BF16)
    rb = ROUTE_RB
    ra = lax.broadcasted_iota(I32, (rb, rb), 0)
    ca = lax.broadcasted_iota(I32, (rb, rb), 1)
    tri = (ca < ra).astype(BF16)

    def prefix(mask):
        mb = mask.astype(BF16)
        outs = []
        offset = jnp.zeros((1, LANES), F32)
        for r0 in range(0, rows, rb):
            blk = mb[r0:r0 + rb]
            tot = jnp.dot(blk, q_all, preferred_element_type=F32)
            within = jnp.dot(blk, q_before, preferred_element_type=F32)
            above = jnp.dot(tri, tot.astype(BF16), preferred_element_type=F32)
            outs.append(above + within + offset)
            offset = offset + jnp.sum(tot, axis=0, keepdims=True)
        return jnp.concatenate(outs, axis=0)

    gt = bits > thr
    eq = bits == thr
    need = (cap - count(gt)).astype(F32)
    sel = gt | (eq & (prefix(eq) < need))
    csx = prefix(sel)
    pos_ref[...] = jnp.where(sel, csx.astype(I32), -1)
    gate_ref[...] = jnp.where(sel, aff, 0.0)
    csx_ref[...] = csx.astype(I32)


def _route(aff_packed, cap):
    rows = aff_packed.shape[0]
    vm = pl.BlockSpec(memory_space=pltpu.VMEM)
    return pl.pallas_call(
        functools.partial(_route_kernel, cap=cap),
        in_specs=[vm],
        out_specs=[vm, vm, vm],
        out_shape=[jax.ShapeDtypeStruct((rows, LANES), I32), jax.ShapeDtypeStruct((rows, LANES), F32),
                   jax.ShapeDtypeStruct((rows, LANES), I32)],
        compiler_params=pltpu.CompilerParams(vmem_limit_bytes=VMEM_LIMIT),
        name="route",
    )(aff_packed)


def _window_start(first_slot, align, w, cap):
    return jnp.minimum((first_slot // align) * align, cap - w)


def _gather_kernel(p0_ref, pn_ref, pos_ref, h2_ref, xg_ref, acc_scr, *, cap, w):
    e = pl.program_id(0)
    i = pl.program_id(1)
    tg = h2_ref.shape[0]

    @pl.when(i == 0)
    def _():
        acc_scr[...] = jnp.zeros(acc_scr.shape, F32)

    w0 = _window_start(p0_ref[e, i], SUBLANES, w, cap)
    n_win = (pn_ref[e, i] - w0 + w - 1) // w
    prow = pos_ref[pl.ds(i, 1), :]

    def window(k, carry):
        nominal = w0 + k * w
        start = pl.multiple_of(jnp.minimum(nominal, cap - w), SUBLANES)
        slot = lax.broadcasted_iota(I32, (w, tg), 0) + start
        onehot = ((prow == slot) & (prow >= nominal)).astype(BF16)
        acc_scr[pl.ds(start, w), :] += jnp.dot(onehot, h2_ref[...], preferred_element_type=F32)
        return carry

    lax.fori_loop(0, n_win, window, 0)

    @pl.when(i == pl.num_programs(1) - 1)
    def _():
        xg_ref[...] = acc_scr[...].astype(BF16)


def _gather(p0, pn, pos_t, h2, cap):
    t = h2.shape[0]
    tg = GATHER_TG
    return pl.pallas_call(
        functools.partial(_gather_kernel, cap=cap, w=GATHER_W),
        grid_spec=pltpu.PrefetchScalarGridSpec(
            num_scalar_prefetch=2,
            grid=(N_EXPERTS, t // tg),
            in_specs=[pl.BlockSpec((None, t // tg, tg), lambda e, i, *_: (e, 0, 0)),
                      pl.BlockSpec((tg, D_MODEL), lambda e, i, *_: (i, 0))],
            out_specs=pl.BlockSpec((None, cap, D_MODEL), lambda e, i, *_: (e, 0, 0)),
            scratch_shapes=[pltpu.VMEM((cap, D_MODEL), F32)]),
        out_shape=jax.ShapeDtypeStruct((N_EXPERTS, cap, D_MODEL), BF16),
        compiler_params=_cparams(("parallel", "arbitrary")),
        name="gather",
    )(p0, pn, pos_t, h2)


def _ffn_kernel(xg_ref, wg_ref, wu_ref, wd_ref, y_ref, acc_scr):
    f = pl.program_id(1)
    xg = xg_ref[...]
    a = jnp.dot(xg, wg_ref[...].astype(BF16), preferred_element_type=F32)
    b = jnp.dot(xg, wu_ref[...].astype(BF16), preferred_element_type=F32)
    act = (a * jax.nn.sigmoid(a) * b).astype(BF16)
    part = jnp.dot(act, wd_ref[...].astype(BF16), preferred_element_type=F32)

    @pl.when(f == 0)
    def _():
        acc_scr[...] = part

    @pl.when(f != 0)
    def _():
        acc_scr[...] += part

    @pl.when(f == pl.num_programs(1) - 1)
    def _():
        y_ref[...] = acc_scr[...].astype(BF16)


def _ffn(xg, w_gate, w_up, w_down):
    _, cap, _ = xg.shape
    tf = FFN_TF
    return pl.pallas_call(
        _ffn_kernel,
        grid=(N_EXPERTS, D_FF // tf),
        in_specs=[pl.BlockSpec((None, cap, D_MODEL), lambda e, f: (e, 0, 0)),
                  pl.BlockSpec((None, D_MODEL, tf), lambda e, f: (e, 0, f)),
                  pl.BlockSpec((None, D_MODEL, tf), lambda e, f: (e, 0, f)),
                  pl.BlockSpec((None, tf, D_MODEL), lambda e, f: (e, f, 0))],
        out_specs=pl.BlockSpec((None, cap, D_MODEL), lambda e, f: (e, 0, 0)),
        out_shape=jax.ShapeDtypeStruct((N_EXPERTS, cap, D_MODEL), BF16),
        scratch_shapes=[pltpu.VMEM((cap, D_MODEL), F32)],
        compiler_params=_cparams(("parallel", "arbitrary")),
        name="ffn",
    )(xg, w_gate, w_up, w_down)


def _combine_kernel(p0_ref, pn_ref, x1_ref, pos_ref, gate_ref, y_hbm, o_ref, win, xwin, sem, xsem, *, cap):
    i = pl.program_id(0)
    tt = x1_ref.shape[0]
    w = xwin.shape[0]

    def first_window(e):
        return _window_start(p0_ref[i, e], BF16_ROWS, w, cap)

    def window_copy(e):
        return pltpu.make_async_copy(y_hbm.at[e, pl.ds(first_window(e), w), :],
                                     win.at[pl.ds(e * w, w), :], sem.at[e])

    for e in range(N_EXPERTS):
        window_copy(e).start()

    lane = lax.broadcasted_iota(I32, (tt, w), 1)

    def weights(e, first_slot, lo_slot):
        pcol = pos_ref[:, e:e + 1]
        hit = (pcol - first_slot == lane) & (pcol >= lo_slot)
        return jnp.where(hit, gate_ref[:, e:e + 1], 0.0).astype(BF16)

    c = jnp.concatenate([weights(e, first_window(e), 0) for e in range(N_EXPERTS)], axis=1)
    for e in range(N_EXPERTS):
        window_copy(e).wait()
    o_ref[...] = x1_ref[...] + jnp.dot(c, win[...], preferred_element_type=F32)

    for e in range(N_EXPERTS):
        w0 = first_window(e)
        n_win = (pn_ref[i, e] - w0 + w - 1) // w

        def extra(k, carry, e=e, w0=w0):
            nominal = w0 + k * w
            start = jnp.minimum(nominal, cap - w)
            cp = pltpu.make_async_copy(y_hbm.at[e, pl.ds(start, w), :], xwin, xsem)
            cp.start()
            cp.wait()
            o_ref[...] += jnp.dot(weights(e, start, nominal), xwin[...], preferred_element_type=F32)
            return carry

        lax.fori_loop(1, n_win, extra, 0)


def _combine(p0, pn, x1, pos_tok, gate_tok, y, cap):
    t = x1.shape[0]
    tt, w = COMB_TT, COMB_W
    row = lambda i, *_: (i, 0)
    return pl.pallas_call(
        functools.partial(_combine_kernel, cap=cap),
        grid_spec=pltpu.PrefetchScalarGridSpec(
            num_scalar_prefetch=2,
            grid=(t // tt,),
            in_specs=[pl.BlockSpec((tt, D_MODEL), row), pl.BlockSpec((tt, N_EXPERTS), row),
                      pl.BlockSpec((tt, N_EXPERTS), row), pl.BlockSpec(memory_space=pl.ANY)],
            out_specs=pl.BlockSpec((tt, D_MODEL), row),
            scratch_shapes=[pltpu.VMEM((N_EXPERTS * w, D_MODEL), BF16), pltpu.VMEM((w, D_MODEL), BF16),
                            pltpu.SemaphoreType.DMA((N_EXPERTS,)), pltpu.SemaphoreType.DMA(())]),
        out_shape=jax.ShapeDtypeStruct((t, D_MODEL), F32),
        compiler_params=_cparams(("arbitrary",)),
        name="combine",
    )(p0, pn, x1, pos_tok, gate_tok, y)


def _layer_params(l, rel_bias, attn_norm, w_in, diff_q_norm, diff_k_norm, lambda_q1, lambda_k1, lambda_q2,
                  lambda_k2, diff_subln, mla_q_latent_norm, mla_kv_latent_norm, w_uq, w_ukv, mla_q_norm,
                  mla_k_norm, w_out, ffn_norm, w_router):
    row = lambda v: v.reshape(1, -1).astype(F32)
    w = w_in[l]
    o = 3 * DIFF_W
    pad_rope = MLA_NOPE, LANES - MLA_QK
    w_kpe = jnp.pad(w[:, o + Q_LORA + KV_LORA:], ((0, 0), pad_rope))
    uq = jnp.pad(w_uq[l].reshape(Q_LORA, MLA_HEADS, MLA_QK), ((0, 0), (0, 0), (0, LANES - MLA_QK)))
    ukv = w_ukv[l].reshape(KV_LORA, MLA_HEADS, MLA_NOPE + MLA_V)
    uk = jnp.pad(ukv[:, :, :MLA_NOPE], ((0, 0), (0, 0), (0, LANES - MLA_NOPE)))
    head_gain = lambda g: jnp.pad(g.astype(F32), (0, LANES - MLA_QK)).reshape(1, LANES)
    wr = jnp.pad(w_router[l].astype(F32), ((0, 0), (0, LANES - N_EXPERTS)))
    wr_hi = wr.astype(BF16)
    return dict(
        attn_norm=row(attn_norm[l]),
        w_qkv=w[:, :o].astype(BF16), w_cq=w[:, o:o + Q_LORA].astype(BF16),
        w_ckv=w[:, o + Q_LORA:o + Q_LORA + KV_LORA].astype(BF16), w_kpe=w_kpe.astype(BF16),
        w_uq=uq.reshape(Q_LORA, MLA_W).astype(BF16), w_uk=uk.reshape(KV_LORA, MLA_W).astype(BF16),
        w_uv=ukv[:, :, MLA_NOPE:].reshape(KV_LORA, MLA_HEADS * MLA_V).astype(BF16),
        dq_gain=row(jnp.tile(diff_q_norm[l], 2)), dk_gain=row(jnp.tile(diff_k_norm[l], 2)),
        cq_gain=row(mla_q_latent_norm[l]), ckv_gain=row(mla_kv_latent_norm[l]),
        mq_gain=head_gain(mla_q_norm[l]), mk_gain=head_gain(mla_k_norm[l]),
        subln=row(diff_subln[l]), lq1=row(lambda_q1[l]), lk1=row(lambda_k1[l]),
        lq2=row(lambda_q2[l]), lk2=row(lambda_k2[l]),
        rel_bias_t=rel_bias.astype(F32).T,
        w_o1=w_out[l][:DIFF_W].astype(BF16), w_o2=w_out[l][DIFF_W:].astype(BF16),
        ffn_norm=row(ffn_norm[l]), w_r_hi=wr_hi, w_r_lo=(wr - wr_hi.astype(F32)).astype(BF16),
    )


def _rope_tables(seq):
    half = MLA_ROPE // 2
    inv = 1.0 / (ROPE_BASE ** (jnp.arange(half, dtype=F32) / half))
    ang = jnp.arange(seq, dtype=jnp.int32).astype(F32)[:, None] * inv[None, :]
    cos, sin = jnp.cos(ang), jnp.sin(ang)
    ones = jnp.ones((seq, MLA_NOPE), F32)
    tail = LANES - MLA_QK
    cos_t = jnp.concatenate([ones, cos, cos, jnp.ones((seq, tail), F32)], axis=1)
    sin_t = jnp.concatenate([0 * ones, -sin, sin, jnp.zeros((seq, tail), F32)], axis=1)
    return cos_t, sin_t


def _layer(x, p, w_gate, w_up, w_down, lam_init):
    b, s, d = x.shape
    t = b * s
    cap = CAPACITY_FACTOR * t // N_EXPERTS
    x2d = x.reshape(t, d)

    cos_t, sin_t = _rope_tables(s)
    qd, kd, vd, qm, km, vm = _prep(x2d, s, cos_t, sin_t, p)
    shp = lambda a: a.reshape(b, s, a.shape[-1])
    bias = _bias_tiles(p["rel_bias_t"], ATT_T)
    od = _diff_attention(shp(qd), shp(kd), shp(vd), bias, p["rel_bias_t"], p, lam_init)
    om = _mla_attention(shp(qm), shp(km), shp(vm))
    x1, h2, aff = _outproj(x2d, od.reshape(t, -1), om.reshape(t, -1), p)

    groups = LANES // N_EXPERTS
    pos_p, gate_p, csx_p = _route(aff.reshape(t // groups, LANES), cap)
    pos_tok = pos_p.reshape(t, N_EXPERTS)
    gate_tok = gate_p.reshape(t, N_EXPERTS)
    csx_tok = csx_p.reshape(t, N_EXPERTS)

    def tile_slots(tile):
        first = csx_tok[::tile]
        return first, jnp.concatenate([first[1:], jnp.full((1, N_EXPERTS), cap, I32)], axis=0)

    g0, gn = tile_slots(GATHER_TG)
    pos_t = pos_tok.T.reshape(N_EXPERTS, t // GATHER_TG, GATHER_TG)
    xg = _gather(g0.T, gn.T, pos_t, h2, cap)
    y = _ffn(xg, w_gate, w_up, w_down)
    c0, cn = tile_slots(COMB_TT)
    out = _combine(c0, cn, x1, pos_tok, gate_tok, y, cap)
    return out.reshape(b, s, d)


def kernel(x_prompt, x_sample, rel_bias, attn_norm, w_in, diff_q_norm, diff_k_norm, lambda_q1, lambda_k1, lambda_q2, lambda_k2, diff_subln, mla_q_latent_norm, mla_kv_latent_norm, w_uq, w_ukv, mla_q_norm, mla_k_norm, w_out, ffn_norm, w_router, w_gate, w_up, w_down):
    outs = []
    for x in (x_prompt, x_sample):
        for l in range(DEPTH):
            p = _layer_params(l, rel_bias, attn_norm, w_in, diff_q_norm, diff_k_norm, lambda_q1, lambda_k1,
                              lambda_q2, lambda_k2, diff_subln, mla_q_latent_norm, mla_kv_latent_norm, w_uq,
                              w_ukv, mla_q_norm, mla_k_norm, w_out, ffn_norm, w_router)
            lam_init = 0.8 - 0.6 * math.exp(-0.3 * l)
            x = _layer(x, p, w_gate[l], w_up[l], w_down[l], lam_init)
        outs.append(x)
    return tuple(outs)
```

```python
import functools
import math

import jax
import jax.numpy as jnp
from jax import lax
from jax.experimental import pallas as pl
from jax.experimental.pallas import tpu as pltpu

F32 = jnp.float32
BF16 = jnp.bfloat16
I32 = jnp.int32

D_MODEL = 1024
DEPTH = 1
DIFF_HEADS = 4
DIFF_DK = 64
DIFF_DV = 128
MLA_HEADS = 8
MLA_NOPE = 64
MLA_ROPE = 32
MLA_QK = MLA_NOPE + MLA_ROPE
MLA_V = 64
Q_LORA = 256
KV_LORA = 128
ROPE_BASE = 10000.0
N_BUCKETS = 32
MAX_DISTANCE = 128
N_EXPERTS = 16
CAPACITY_FACTOR = 2
D_FF = 2816
EPS = 1e-6

LANES = 128
SUBLANES = 8
BF16_ROWS = 16
DIFF_W = DIFF_HEADS * DIFF_DV
MLA_W = MLA_HEADS * LANES
NEG_BIG = -1e30
LOG2E = math.log2(math.e)
VMEM_LIMIT = 56 * 1024 * 1024
F32_EXP_RANGE = 120.0
BOUND_SLACK = 1.02

PREP_TM = 512
ATT_T = 512
DIFF_TK = 2048
MLA_TQ = 512
MLA_TK = 2048
OUT_TM = 512
ROUTE_RB = 256
GATHER_TG = 1024
GATHER_W = 192
FFN_TF = 256
COMB_TT = 512
COMB_W = 128


def _cparams(sem, vmem=VMEM_LIMIT):
    return pltpu.CompilerParams(dimension_semantics=sem, vmem_limit_bytes=vmem)


def _rms_rows(x, gain):
    ms = jnp.mean(x * x, axis=-1, keepdims=True)
    return x * lax.rsqrt(ms + EPS) * gain


def _prep_kernel(x_ref, cos_ref, sin_ref, an_ref, wqkv_ref, wcq_ref, wckv_ref, wkpe_ref,
                 wuq_ref, wuk_ref, wuv_ref, dqg_ref, dkg_ref, cqg_ref, ckvg_ref, mqg_ref, mkg_ref,
                 qd_ref, kd_ref, vd_ref, qm_ref, km_ref, vm_ref):
    tm = x_ref.shape[0]
    hb = _rms_rows(x_ref[...], an_ref[...]).astype(BF16)

    lane = lax.broadcasted_iota(I32, (tm, LANES), 1)
    lo = lane < DIFF_DK

    qkv = jnp.dot(hb, wqkv_ref[...], preferred_element_type=F32)
    diff_scale = DIFF_DK ** -0.5 * LOG2E
    for hd in range(DIFF_HEADS):
        for off, g_ref, o_ref, scale in ((0, dqg_ref, qd_ref, diff_scale), (DIFF_W, dkg_ref, kd_ref, None)):
            blk = qkv[:, off + hd * LANES: off + (hd + 1) * LANES]
            sq = blk * blk
            s_lo = jnp.sum(jnp.where(lo, sq, 0.0), axis=-1, keepdims=True)
            s_hi = jnp.sum(jnp.where(lo, 0.0, sq), axis=-1, keepdims=True)
            r = jnp.where(lo, lax.rsqrt(s_lo / DIFF_DK + EPS), lax.rsqrt(s_hi / DIFF_DK + EPS))
            y = blk * r * g_ref[...]
            if scale is not None:
                y = y * scale
            o_ref[:, hd * LANES:(hd + 1) * LANES] = y.astype(BF16)
    vd_ref[...] = qkv[:, 2 * DIFF_W:3 * DIFF_W].astype(BF16)

    cq = _rms_rows(jnp.dot(hb, wcq_ref[...], preferred_element_type=F32), cqg_ref[...]).astype(BF16)
    ckv = _rms_rows(jnp.dot(hb, wckv_ref[...], preferred_element_type=F32), ckvg_ref[...]).astype(BF16)
    kpe = jnp.dot(hb, wkpe_ref[...], preferred_element_type=F32)
    q = jnp.dot(cq, wuq_ref[...], preferred_element_type=F32)
    k = jnp.dot(ckv, wuk_ref[...], preferred_element_type=F32)
    vm_ref[...] = jnp.dot(ckv, wuv_ref[...], preferred_element_type=F32).astype(BF16)

    cosb = cos_ref[...]
    sinb = sin_ref[...]
    first_half = (lane >= MLA_NOPE) & (lane < MLA_NOPE + MLA_ROPE // 2)
    mla_scale = MLA_QK ** -0.5 * LOG2E

    def norm_rope(blk, gain):
        ss = jnp.sum(blk * blk, axis=-1, keepdims=True)
        y = blk * lax.rsqrt(ss / MLA_QK + EPS) * gain
        partner = jnp.where(first_half,
                            pltpu.roll(y, LANES - MLA_ROPE // 2, 1),
                            pltpu.roll(y, MLA_ROPE // 2, 1))
        return y * cosb + partner * sinb

    for hd in range(MLA_HEADS):
        sl = slice(hd * LANES, (hd + 1) * LANES)
        qm_ref[:, sl] = (norm_rope(q[:, sl], mqg_ref[...]) * mla_scale).astype(BF16)
        kn = norm_rope(k[:, sl] + kpe, mkg_ref[...])
        km_ref[:, sl] = jnp.where(lane == MLA_QK, 1.0, kn).astype(BF16)


def _prep(x2d, seq, cos_t, sin_t, p):
    t = x2d.shape[0]
    tm = PREP_TM
    nseq = seq // tm
    row = lambda i: (i, 0)
    full = lambda i: (0, 0)
    wspec = lambda a: pl.BlockSpec(a.shape, full)
    weights = (p["attn_norm"], p["w_qkv"], p["w_cq"], p["w_ckv"], p["w_kpe"], p["w_uq"], p["w_uk"], p["w_uv"],
               p["dq_gain"], p["dk_gain"], p["cq_gain"], p["ckv_gain"], p["mq_gain"], p["mk_gain"])
    out_w = (DIFF_W, DIFF_W, DIFF_W, MLA_W, MLA_W, MLA_HEADS * MLA_V)
    return pl.pallas_call(
        _prep_kernel,
        grid=(t // tm,),
        in_specs=[pl.BlockSpec((tm, D_MODEL), row),
                  pl.BlockSpec((tm, LANES), lambda i: (i % nseq, 0)),
                  pl.BlockSpec((tm, LANES), lambda i: (i % nseq, 0))] + [wspec(w) for w in weights],
        out_specs=[pl.BlockSpec((tm, w), row) for w in out_w],
        out_shape=[jax.ShapeDtypeStruct((t, w), BF16) for w in out_w],
        compiler_params=_cparams(("parallel",)),
        name="prep",
    )(x2d, cos_t, sin_t, *weights)


def _bias_kernel(rb_ref, o_ref):
    hd = pl.program_id(0)
    d = pl.program_id(1)
    tq, tk = o_ref.shape
    row = lax.broadcasted_iota(I32, (tq, tk), 0)
    col = lax.broadcasted_iota(I32, (tq, tk), 1)
    rel = col - row + (d - 1) * tk
    half = N_BUCKETS // 2
    max_exact = half // 2
    n = jnp.abs(rel)
    nf = jnp.maximum(n, 1).astype(F32)
    large = max_exact + (jnp.log(nf / max_exact) / math.log(MAX_DISTANCE / max_exact)
                         * (half - max_exact)).astype(I32)
    large = jnp.minimum(large, half - 1)
    bucket = jnp.where(rel > 0, half, 0) + jnp.where(n < max_exact, n, large)
    acc = jnp.zeros((tq, tk), F32)
    for b in range(N_BUCKETS):
        acc = jnp.where(bucket == b, rb_ref[hd, b], acc)
    o_ref[...] = acc * LOG2E


def _bias_tiles(rel_bias_t, t):
    return pl.pallas_call(
        _bias_kernel,
        grid=(DIFF_HEADS, 3),
        in_specs=[pl.BlockSpec(memory_space=pltpu.SMEM)],
        out_specs=pl.BlockSpec((None, None, t, t), lambda h, d: (h, d, 0, 0)),
        out_shape=jax.ShapeDtypeStruct((DIFF_HEADS, 3, t, t), F32),
        compiler_params=_cparams(("parallel", "parallel")),
        name="bias_tiles",
    )(rel_bias_t)


def _online_update(mi, s, v, m_scr, l_scr, acc_scr):
    blocks = [s[:, i:i + LANES] for i in range(0, s.shape[1], LANES)]
    m_old = m_scr[mi]
    row_max = jnp.max(functools.reduce(jnp.maximum, blocks), axis=-1, keepdims=True)
    m_new = jnp.maximum(m_old, row_max)
    alpha = jnp.exp2(m_old - m_new)
    ps = [jnp.exp2(blk - m_new) for blk in blocks]
    l_scr[mi] = alpha * l_scr[mi] + functools.reduce(jnp.add, ps)
    p = jnp.concatenate([x.astype(BF16) for x in ps], axis=1)
    acc_scr[mi] = alpha * acc_scr[mi] + jnp.dot(p, v, preferred_element_type=F32)
    m_scr[mi] = m_new


def _fixed_ref_update(mi, s, v, l_scr, acc_scr):
    ps = [jnp.exp2(s[:, i:i + LANES]) for i in range(0, s.shape[1], LANES)]
    l_scr[mi] += functools.reduce(jnp.add, ps)
    p = jnp.concatenate([x.astype(BF16) for x in ps], axis=1)
    acc_scr[mi] += jnp.dot(p, v, preferred_element_type=F32)


def _normalised(mi, l_scr, acc_scr):
    return acc_scr[mi] / jnp.sum(l_scr[mi], axis=-1, keepdims=True)


def _qk(q, k):
    return lax.dot_general(q, k, (((1,), (1,)), ((), ())), preferred_element_type=F32)


def _diff_attn_kernel(rb_ref, bound_ref, q_ref, k_ref, v_ref, bias_ref, subln_ref, lq1_ref, lk1_ref, lq2_ref,
                      lk2_ref, o_ref, m_scr, l_scr, acc_scr, *, lam_init, tk):
    hd = pl.program_id(1)
    qi = pl.program_id(2)
    t = q_ref.shape[0]
    n = k_ref.shape[0] // t
    r = tk // t
    bound = bound_ref[0]
    fixed_ref = bound_ref[1] > 0.5

    q = q_ref[...]
    lane = lax.broadcasted_iota(I32, q.shape, 1)
    zero = jnp.zeros_like(q)
    qs = (jnp.where(lane < DIFF_DK, q, zero), jnp.where(lane < DIFF_DK, zero, q))

    l_scr[...] = jnp.zeros(l_scr.shape, F32)
    acc_scr[...] = jnp.zeros(acc_scr.shape, F32)

    lo = jnp.maximum(qi - 1, 0)
    hi = jnp.minimum(qi + 2, n)
    lo_long = lo // r
    hi_long = (hi + r - 1) // r
    c_left = rb_ref[hd, N_BUCKETS // 2 - 1] * LOG2E
    c_right = rb_ref[hd, N_BUCKETS - 1] * LOG2E

    def sweep(update, after_left, after_near):
        def make_step(size, near):
            def step(j, carry):
                off = pl.multiple_of(j * size, size)
                k = k_ref[pl.ds(off, size), :]
                v = v_ref[pl.ds(off, size), :]
                for mi in range(2):
                    s = _qk(qs[mi], k)
                    if near:
                        s = s + bias_ref[j - qi + 1]
                    update(mi, s, v)
                return carry
            return step

        lax.fori_loop(0, lo_long, make_step(tk, False), 0)
        lax.fori_loop(lo_long * r, lo, make_step(t, False), 0)
        after_left()
        lax.fori_loop(lo, hi, make_step(t, True), 0)
        after_near()
        lax.fori_loop(hi, hi_long * r, make_step(t, False), 0)
        lax.fori_loop(hi_long, n // r, make_step(tk, False), 0)

    @pl.when(fixed_ref)
    def _():
        def rescale(c):
            factor = jnp.exp2(jnp.full((1, LANES), c, F32))
            l_scr[...] = l_scr[...] * factor
            acc_scr[...] = acc_scr[...] * factor

        sweep(lambda mi, s, v: _fixed_ref_update(mi, s - bound, v, l_scr, acc_scr),
              lambda: rescale(c_left), lambda: rescale(-c_right))

    @pl.when(jnp.logical_not(fixed_ref))
    def _():
        m_scr[...] = jnp.full(m_scr.shape, NEG_BIG, F32)

        def shift(c):
            m_scr[...] = m_scr[...] + c

        sweep(lambda mi, s, v: _online_update(mi, s, v, m_scr, l_scr, acc_scr),
              lambda: shift(c_left), lambda: shift(-c_right))

    lam = (jnp.exp(jnp.sum(lq1_ref[...] * lk1_ref[...], axis=-1, keepdims=True))
           - jnp.exp(jnp.sum(lq2_ref[...] * lk2_ref[...], axis=-1, keepdims=True)) + lam_init)
    o = _normalised(0, l_scr, acc_scr) - lam * _normalised(1, l_scr, acc_scr)
    o_ref[...] = (_rms_rows(o, subln_ref[...]) * (1.0 - lam_init)).astype(BF16)


def _diff_attention(qd, kd, vd, bias_tiles, rel_bias_t, p, lam_init):
    b, s, _ = qd.shape
    t = ATT_T
    vec = lambda a: pl.BlockSpec(a.shape, lambda bi, h, i: (0, 0))
    kv_spec = pl.BlockSpec((None, s, LANES), lambda bi, h, i: (bi, 0, h))
    return pl.pallas_call(
        functools.partial(_diff_attn_kernel, lam_init=lam_init, tk=DIFF_TK),
        grid=(b, DIFF_HEADS, s // t),
        in_specs=[pl.BlockSpec(memory_space=pltpu.SMEM), pl.BlockSpec(memory_space=pltpu.SMEM),
                  pl.BlockSpec((None, t, LANES), lambda bi, h, i: (bi, i, h)),
                  kv_spec, kv_spec,
                  pl.BlockSpec((None, 3, t, t), lambda bi, h, i: (h, 0, 0, 0)),
                  vec(p["subln"]), vec(p["lq1"]), vec(p["lk1"]), vec(p["lq2"]), vec(p["lk2"])],
        out_specs=pl.BlockSpec((None, t, LANES), lambda bi, h, i: (bi, i, h)),
        out_shape=jax.ShapeDtypeStruct((b, s, DIFF_W), BF16),
        scratch_shapes=[pltpu.VMEM((2, t, LANES), F32)] * 3,
        compiler_params=_cparams(("parallel", "parallel", "arbitrary")),
        name="diff_attn",
    )(rel_bias_t, p["diff_bound"], qd, kd, vd, bias_tiles, p["subln"], p["lq1"], p["lk1"], p["lq2"], p["lk2"])


def _mla_attn_kernel(bound_ref, q_ref, k_ref, v_ref, o_ref, m_scr, l_scr, acc_scr, *, tk):
    t = q_ref.shape[0]
    n = k_ref.shape[0] // tk
    bound = bound_ref[0]
    fixed_ref = bound_ref[1] > 0.5

    l_scr[...] = jnp.zeros(l_scr.shape, F32)
    acc_scr[...] = jnp.zeros(acc_scr.shape, F32)

    def sweep(q, update):
        qs = (q[:, :LANES], q[:, LANES:])

        def step(j, carry):
            off = pl.multiple_of(j * tk, tk)
            v = v_ref[pl.ds(off, tk), :]
            for mi in range(2):
                k = k_ref[pl.ds(off, tk), mi * LANES:(mi + 1) * LANES]
                update(mi, _qk(qs[mi], k), v)
            return carry

        lax.fori_loop(0, n, step, 0)

    @pl.when(fixed_ref)
    def _():
        q = q_ref[...]
        lane = lax.broadcasted_iota(I32, q.shape, 1)
        q = jnp.where(lane % LANES == MLA_QK, jnp.full(q.shape, -bound, F32).astype(BF16), q)
        sweep(q, lambda mi, s, v: _fixed_ref_update(mi, s, v, l_scr, acc_scr))

    @pl.when(jnp.logical_not(fixed_ref))
    def _():
        m_scr[...] = jnp.full(m_scr.shape, NEG_BIG, F32)
        sweep(q_ref[...], lambda mi, s, v: _online_update(mi, s, v, m_scr, l_scr, acc_scr))

    lane = lax.broadcasted_iota(I32, (t, LANES), 1)
    o = jnp.where(lane < MLA_V, _normalised(0, l_scr, acc_scr), _normalised(1, l_scr, acc_scr))
    o_ref[...] = o.astype(BF16)


def _mla_attention(qm, km, vm, bound):
    b, s, _ = qm.shape
    t = MLA_TQ
    return pl.pallas_call(
        functools.partial(_mla_attn_kernel, tk=MLA_TK),
        grid=(b, MLA_HEADS // 2, s // t),
        in_specs=[pl.BlockSpec(memory_space=pltpu.SMEM),
                  pl.BlockSpec((None, t, 2 * LANES), lambda bi, h, i: (bi, i, h)),
                  pl.BlockSpec((None, s, 2 * LANES), lambda bi, h, i: (bi, 0, h)),
                  pl.BlockSpec((None, s, LANES), lambda bi, h, i: (bi, 0, h))],
        out_specs=pl.BlockSpec((None, t, LANES), lambda bi, h, i: (bi, i, h)),
        out_shape=jax.ShapeDtypeStruct((b, s, MLA_HEADS * MLA_V), BF16),
        scratch_shapes=[pltpu.VMEM((2, t, LANES), F32)] * 3,
        compiler_params=_cparams(("parallel", "parallel", "arbitrary")),
        name="mla_attn",
    )(bound, qm, km, vm)


def _split_bf16(x):
    hi = x.astype(BF16)
    lo = (x - hi.astype(F32)).astype(BF16)
    return hi, lo


def _outproj_kernel(x_ref, od_ref, om_ref, wo1_ref, wo2_ref, fn_ref, wrh_ref, wrl_ref,
                    x1_ref, h2_ref, aff_ref):
    x1 = (x_ref[...]
          + jnp.dot(od_ref[...], wo1_ref[...], preferred_element_type=F32)
          + jnp.dot(om_ref[...], wo2_ref[...], preferred_element_type=F32))
    x1_ref[...] = x1
    h = _rms_rows(x1, fn_ref[...])
    hh, hl = _split_bf16(h)
    h2_ref[...] = hh
    logits = (jnp.dot(hh, wrh_ref[...], preferred_element_type=F32)
              + jnp.dot(hl, wrh_ref[...], preferred_element_type=F32)
              + jnp.dot(hh, wrl_ref[...], preferred_element_type=F32))
    lane = lax.broadcasted_iota(I32, logits.shape, 1)
    logits = jnp.where(lane < N_EXPERTS, logits, NEG_BIG)
    e = jnp.exp(logits - jnp.max(logits, axis=-1, keepdims=True))
    aff = e / jnp.sum(e, axis=-1, keepdims=True)
    aff_ref[...] = aff[:, :N_EXPERTS]


def _outproj(x2d, od, om, p):
    t = x2d.shape[0]
    tm = OUT_TM
    row = lambda i: (i, 0)
    wspec = lambda a: pl.BlockSpec(a.shape, lambda i: (0, 0))
    weights = (p["w_o1"], p["w_o2"], p["ffn_norm"], p["w_r_hi"], p["w_r_lo"])
    return pl.pallas_call(
        _outproj_kernel,
        grid=(t // tm,),
        in_specs=[pl.BlockSpec((tm, D_MODEL), row), pl.BlockSpec((tm, DIFF_W), row),
                  pl.BlockSpec((tm, MLA_HEADS * MLA_V), row)] + [wspec(w) for w in weights],
        out_specs=[pl.BlockSpec((tm, D_MODEL), row), pl.BlockSpec((tm, D_MODEL), row),
                   pl.BlockSpec((tm, N_EXPERTS), row)],
        out_shape=[jax.ShapeDtypeStruct((t, D_MODEL), F32), jax.ShapeDtypeStruct((t, D_MODEL), BF16),
                   jax.ShapeDtypeStruct((t, N_EXPERTS), F32)],
        compiler_params=_cparams(("parallel",)),
        name="outproj",
    )(x2d, od, om, *weights)


def _route_kernel(aff_ref, pos_ref, gate_ref, csx_ref, *, cap):
    rows = aff_ref.shape[0]
    groups = LANES // N_EXPERTS
    aff = aff_ref[...]
    bits = pltpu.bitcast(aff, I32)

    def expert_total(v):
        for sh in (N_EXPERTS, 2 * N_EXPERTS, 4 * N_EXPERTS):
            v = v + pltpu.roll(v, sh, 1)
        return v

    def count(mask):
        return expert_total(jnp.sum(mask.astype(I32), axis=0, keepdims=True))

    def search(i, thr):
        cand = thr | jnp.left_shift(jnp.int32(1), 30 - i)
        return jnp.where(count(bits >= cand) >= cap, cand, thr)

    thr = lax.fori_loop(0, 31, search, jnp.zeros((1, LANES), I32))

    a = lax.broadcasted_iota(I32, (LANES, LANES), 0)
    b = lax.broadcasted_iota(I32, (LANES, LANES), 1)
    same = (a % N_EXPERTS) == (b % N_EXPERTS)
    q_all = same.astype(BF16)
    q_before = (same & (a // N_EXPERTS < b // N_EXPERTS)).astype(BF16)
    rb = ROUTE_RB
    ra = lax.broadcasted_iota(I32, (rb, rb), 0)
    ca = lax.broadcasted_iota(I32, (rb, rb), 1)
    tri = (ca < ra).astype(BF16)

    def prefix(mask):
        mb = mask.astype(BF16)
        outs = []
        offset = jnp.zeros((1, LANES), F32)
        for r0 in range(0, rows, rb):
            blk = mb[r0:r0 + rb]
            tot = jnp.dot(blk, q_all, preferred_element_type=F32)
            within = jnp.dot(blk, q_before, preferred_element_type=F32)
            above = jnp.dot(tri, tot.astype(BF16), preferred_element_type=F32)
            outs.append(above + within + offset)
            offset = offset + jnp.sum(tot, axis=0, keepdims=True)
        return jnp.concatenate(outs, axis=0)

    gt = bits > thr
    eq = bits == thr
    need = (cap - count(gt)).astype(F32)
    sel = gt | (eq & (prefix(eq) < need))
    csx = prefix(sel)
    pos_ref[...] = jnp.where(sel, csx.astype(I32), -1)
    gate_ref[...] = jnp.where(sel, aff, 0.0)
    csx_ref[...] = csx.astype(I32)


def _route(aff_packed, cap):
    rows = aff_packed.shape[0]
    vm = pl.BlockSpec(memory_space=pltpu.VMEM)
    return pl.pallas_call(
        functools.partial(_route_kernel, cap=cap),
        in_specs=[vm],
        out_specs=[vm, vm, vm],
        out_shape=[jax.ShapeDtypeStruct((rows, LANES), I32), jax.ShapeDtypeStruct((rows, LANES), F32),
                   jax.ShapeDtypeStruct((rows, LANES), I32)],
        compiler_params=pltpu.CompilerParams(vmem_limit_bytes=VMEM_LIMIT),
        name="route",
    )(aff_packed)


def _window_start(first_slot, align, w, cap):
    return jnp.minimum((first_slot // align) * align, cap - w)


def _gather_kernel(p0_ref, pn_ref, pos_ref, h2_ref, xg_ref, acc_scr, *, cap, w):
    e = pl.program_id(0)
    i = pl.program_id(1)
    tg = h2_ref.shape[0]

    @pl.when(i == 0)
    def _():
        acc_scr[...] = jnp.zeros(acc_scr.shape, F32)

    w0 = _window_start(p0_ref[e, i], SUBLANES, w, cap)
    n_win = (pn_ref[e, i] - w0 + w - 1) // w
    prow = pos_ref[pl.ds(i, 1), :]

    def window(k, carry):
        nominal = w0 + k * w
        start = pl.multiple_of(jnp.minimum(nominal, cap - w), SUBLANES)
        slot = lax.broadcasted_iota(I32, (w, tg), 0) + start
        onehot = ((prow == slot) & (prow >= nominal)).astype(BF16)
        acc_scr[pl.ds(start, w), :] += jnp.dot(onehot, h2_ref[...], preferred_element_type=F32)
        return carry

    lax.fori_loop(0, n_win, window, 0)

    @pl.when(i == pl.num_programs(1) - 1)
    def _():
        xg_ref[...] = acc_scr[...].astype(BF16)


def _gather(p0, pn, pos_t, h2, cap):
    t = h2.shape[0]
    tg = GATHER_TG
    return pl.pallas_call(
        functools.partial(_gather_kernel, cap=cap, w=GATHER_W),
        grid_spec=pltpu.PrefetchScalarGridSpec(
            num_scalar_prefetch=2,
            grid=(N_EXPERTS, t // tg),
            in_specs=[pl.BlockSpec((None, t // tg, tg), lambda e, i, *_: (e, 0, 0)),
                      pl.BlockSpec((tg, D_MODEL), lambda e, i, *_: (i, 0))],
            out_specs=pl.BlockSpec((None, cap, D_MODEL), lambda e, i, *_: (e, 0, 0)),
            scratch_shapes=[pltpu.VMEM((cap, D_MODEL), F32)]),
        out_shape=jax.ShapeDtypeStruct((N_EXPERTS, cap, D_MODEL), BF16),
        compiler_params=_cparams(("parallel", "arbitrary")),
        name="gather",
    )(p0, pn, pos_t, h2)


def _ffn_kernel(xg_ref, wg_ref, wu_ref, wd_ref, y_ref, acc_scr):
    f = pl.program_id(1)
    xg = xg_ref[...]
    a = jnp.dot(xg, wg_ref[...].astype(BF16), preferred_element_type=F32)
    b = jnp.dot(xg, wu_ref[...].astype(BF16), preferred_element_type=F32)
    act = (a * jax.nn.sigmoid(a) * b).astype(BF16)
    part = jnp.dot(act, wd_ref[...].astype(BF16), preferred_element_type=F32)

    @pl.when(f == 0)
    def _():
        acc_scr[...] = part

    @pl.when(f != 0)
    def _():
        acc_scr[...] += part

    @pl.when(f == pl.num_programs(1) - 1)
    def _():
        y_ref[...] = acc_scr[...].astype(BF16)


def _ffn(xg, w_gate, w_up, w_down):
    _, cap, _ = xg.shape
    tf = FFN_TF
    return pl.pallas_call(
        _ffn_kernel,
        grid=(N_EXPERTS, D_FF // tf),
        in_specs=[pl.BlockSpec((None, cap, D_MODEL), lambda e, f: (e, 0, 0)),
                  pl.BlockSpec((None, D_MODEL, tf), lambda e, f: (e, 0, f)),
                  pl.BlockSpec((None, D_MODEL, tf), lambda e, f: (e, 0, f)),
                  pl.BlockSpec((None, tf, D_MODEL), lambda e, f: (e, f, 0))],
        out_specs=pl.BlockSpec((None, cap, D_MODEL), lambda e, f: (e, 0, 0)),
        out_shape=jax.ShapeDtypeStruct((N_EXPERTS, cap, D_MODEL), BF16),
        scratch_shapes=[pltpu.VMEM((cap, D_MODEL), F32)],
        compiler_params=_cparams(("parallel", "arbitrary")),
        name="ffn",
    )(xg, w_gate, w_up, w_down)


def _combine_kernel(p0_ref, pn_ref, x1_ref, pos_ref, gate_ref, y_hbm, o_ref, win, xwin, sem, xsem, *, cap):
    i = pl.program_id(0)
    tt = x1_ref.shape[0]
    w = xwin.shape[0]

    def first_window(e):
        return _window_start(p0_ref[i, e], BF16_ROWS, w, cap)

    def window_copy(e):
        return pltpu.make_async_copy(y_hbm.at[e, pl.ds(first_window(e), w), :],
                                     win.at[pl.ds(e * w, w), :], sem.at[e])

    for e in range(N_EXPERTS):
        window_copy(e).start()

    lane = lax.broadcasted_iota(I32, (tt, w), 1)

    def weights(e, first_slot, lo_slot):
        pcol = pos_ref[:, e:e + 1]
        hit = (pcol - first_slot == lane) & (pcol >= lo_slot)
        return jnp.where(hit, gate_ref[:, e:e + 1], 0.0).astype(BF16)

    c = jnp.concatenate([weights(e, first_window(e), 0) for e in range(N_EXPERTS)], axis=1)
    for e in range(N_EXPERTS):
        window_copy(e).wait()
    o_ref[...] = x1_ref[...] + jnp.dot(c, win[...], preferred_element_type=F32)

    for e in range(N_EXPERTS):
        w0 = first_window(e)
        n_win = (pn_ref[i, e] - w0 + w - 1) // w

        def extra(k, carry, e=e, w0=w0):
            nominal = w0 + k * w
            start = jnp.minimum(nominal, cap - w)
            cp = pltpu.make_async_copy(y_hbm.at[e, pl.ds(start, w), :], xwin, xsem)
            cp.start()
            cp.wait()
            o_ref[...] += jnp.dot(weights(e, start, nominal), xwin[...], preferred_element_type=F32)
            return carry

        lax.fori_loop(1, n_win, extra, 0)


def _combine(p0, pn, x1, pos_tok, gate_tok, y, cap):
    t = x1.shape[0]
    tt, w = COMB_TT, COMB_W
    row = lambda i, *_: (i, 0)
    return pl.pallas_call(
        functools.partial(_combine_kernel, cap=cap),
        grid_spec=pltpu.PrefetchScalarGridSpec(
            num_scalar_prefetch=2,
            grid=(t // tt,),
            in_specs=[pl.BlockSpec((tt, D_MODEL), row), pl.BlockSpec((tt, N_EXPERTS), row),
                      pl.BlockSpec((tt, N_EXPERTS), row), pl.BlockSpec(memory_space=pl.ANY)],
            out_specs=pl.BlockSpec((tt, D_MODEL), row),
            scratch_shapes=[pltpu.VMEM((N_EXPERTS * w, D_MODEL), BF16), pltpu.VMEM((w, D_MODEL), BF16),
                            pltpu.SemaphoreType.DMA((N_EXPERTS,)), pltpu.SemaphoreType.DMA(())]),
        out_shape=jax.ShapeDtypeStruct((t, D_MODEL), F32),
        compiler_params=_cparams(("arbitrary",)),
        name="combine",
    )(p0, pn, x1, pos_tok, gate_tok, y)


def _score_bound(width, q_gain, k_gain, max_bias):
    gmax = lambda g: jnp.max(jnp.abs(g.astype(F32)))
    bound = BOUND_SLACK * (width ** 0.5 * LOG2E * gmax(q_gain) * gmax(k_gain) + LOG2E * max_bias)
    return jnp.stack([bound, (3.0 * bound < F32_EXP_RANGE).astype(F32)]).astype(F32)


def _layer_params(l, rel_bias, attn_norm, w_in, diff_q_norm, diff_k_norm, lambda_q1, lambda_k1, lambda_q2,
                  lambda_k2, diff_subln, mla_q_latent_norm, mla_kv_latent_norm, w_uq, w_ukv, mla_q_norm,
                  mla_k_norm, w_out, ffn_norm, w_router):
    row = lambda v: v.reshape(1, -1).astype(F32)
    w = w_in[l]
    o = 3 * DIFF_W
    pad_rope = MLA_NOPE, LANES - MLA_QK
    w_kpe = jnp.pad(w[:, o + Q_LORA + KV_LORA:], ((0, 0), pad_rope))
    uq = jnp.pad(w_uq[l].reshape(Q_LORA, MLA_HEADS, MLA_QK), ((0, 0), (0, 0), (0, LANES - MLA_QK)))
    ukv = w_ukv[l].reshape(KV_LORA, MLA_HEADS, MLA_NOPE + MLA_V)
    uk = jnp.pad(ukv[:, :, :MLA_NOPE], ((0, 0), (0, 0), (0, LANES - MLA_NOPE)))
    head_gain = lambda g: jnp.pad(g.astype(F32), (0, LANES - MLA_QK)).reshape(1, LANES)
    wr = jnp.pad(w_router[l].astype(F32), ((0, 0), (0, LANES - N_EXPERTS)))
    wr_hi = wr.astype(BF16)
    return dict(
        attn_norm=row(attn_norm[l]),
        w_qkv=w[:, :o].astype(BF16), w_cq=w[:, o:o + Q_LORA].astype(BF16),
        w_ckv=w[:, o + Q_LORA:o + Q_LORA + KV_LORA].astype(BF16), w_kpe=w_kpe.astype(BF16),
        w_uq=uq.reshape(Q_LORA, MLA_W).astype(BF16), w_uk=uk.reshape(KV_LORA, MLA_W).astype(BF16),
        w_uv=ukv[:, :, MLA_NOPE:].reshape(KV_LORA, MLA_HEADS * MLA_V).astype(BF16),
        dq_gain=row(jnp.tile(diff_q_norm[l], 2)), dk_gain=row(jnp.tile(diff_k_norm[l], 2)),
        cq_gain=row(mla_q_latent_norm[l]), ckv_gain=row(mla_kv_latent_norm[l]),
        mq_gain=head_gain(mla_q_norm[l]), mk_gain=head_gain(mla_k_norm[l]),
        subln=row(diff_subln[l]), lq1=row(lambda_q1[l]), lk1=row(lambda_k1[l]),
        lq2=row(lambda_q2[l]), lk2=row(lambda_k2[l]),
        rel_bias_t=rel_bias.astype(F32).T,
        diff_bound=_score_bound(DIFF_DK, diff_q_norm[l], diff_k_norm[l], jnp.max(jnp.abs(rel_bias))),
        mla_bound=_score_bound(MLA_QK, mla_q_norm[l], mla_k_norm[l], 0.0),
        w_o1=w_out[l][:DIFF_W].astype(BF16), w_o2=w_out[l][DIFF_W:].astype(BF16),
        ffn_norm=row(ffn_norm[l]), w_r_hi=wr_hi, w_r_lo=(wr - wr_hi.astype(F32)).astype(BF16),
    )


def _rope_tables(seq):
    half = MLA_ROPE // 2
    inv = 1.0 / (ROPE_BASE ** (jnp.arange(half, dtype=F32) / half))
    ang = jnp.arange(seq, dtype=jnp.int32).astype(F32)[:, None] * inv[None, :]
    cos, sin = jnp.cos(ang), jnp.sin(ang)
    ones = jnp.ones((seq, MLA_NOPE), F32)
    tail = LANES - MLA_QK
    cos_t = jnp.concatenate([ones, cos, cos, jnp.ones((seq, tail), F32)], axis=1)
    sin_t = jnp.concatenate([0 * ones, -sin, sin, jnp.zeros((seq, tail), F32)], axis=1)
    return cos_t, sin_t


def _layer(x, p, w_gate, w_up, w_down, lam_init):
    b, s, d = x.shape
    t = b * s
    cap = CAPACITY_FACTOR * t // N_EXPERTS
    x2d = x.reshape(t, d)

    cos_t, sin_t = _rope_tables(s)
    qd, kd, vd, qm, km, vm = _prep(x2d, s, cos_t, sin_t, p)
    shp = lambda a: a.reshape(b, s, a.shape[-1])
    bias = _bias_tiles(p["rel_bias_t"], ATT_T)
    od = _diff_attention(shp(qd), shp(kd), shp(vd), bias, p["rel_bias_t"], p, lam_init)
    om = _mla_attention(shp(qm), shp(km), shp(vm), p["mla_bound"])
    x1, h2, aff = _outproj(x2d, od.reshape(t, -1), om.reshape(t, -1), p)

    groups = LANES // N_EXPERTS
    pos_p, gate_p, csx_p = _route(aff.reshape(t // groups, LANES), cap)
    pos_tok = pos_p.reshape(t, N_EXPERTS)
    gate_tok = gate_p.reshape(t, N_EXPERTS)
    csx_tok = csx_p.reshape(t, N_EXPERTS)

    def tile_slots(tile):
        first = csx_tok[::tile]
        return first, jnp.concatenate([first[1:], jnp.full((1, N_EXPERTS), cap, I32)], axis=0)

    g0, gn = tile_slots(GATHER_TG)
    pos_t = pos_tok.T.reshape(N_EXPERTS, t // GATHER_TG, GATHER_TG)
    xg = _gather(g0.T, gn.T, pos_t, h2, cap)
    y = _ffn(xg, w_gate, w_up, w_down)
    c0, cn = tile_slots(COMB_TT)
    out = _combine(c0, cn, x1, pos_tok, gate_tok, y, cap)
    return out.reshape(b, s, d)


def kernel(x_prompt, x_sample, rel_bias, attn_norm, w_in, diff_q_norm, diff_k_norm, lambda_q1, lambda_k1, lambda_q2, lambda_k2, diff_subln, mla_q_latent_norm, mla_kv_latent_norm, w_uq, w_ukv, mla_q_norm, mla_k_norm, w_out, ffn_norm, w_router, w_gate, w_up, w_down):
    outs = []
    for x in (x_prompt, x_sample):
        for l in range(DEPTH):
            p = _layer_params(l, rel_bias, attn_norm, w_in, diff_q_norm, diff_k_norm, lambda_q1, lambda_k1,
                              lambda_q2, lambda_k2, diff_subln, mla_q_latent_norm, mla_kv_latent_norm, w_uq,
                              w_ukv, mla_q_norm, mla_k_norm, w_out, ffn_norm, w_router)
            lam_init = 0.8 - 0.6 * math.exp(-0.3 * l)
            x = _layer(x, p, w_gate[l], w_up[l], w_down[l], lam_init)
        outs.append(x)
    return tuple(outs)
```

```python
import functools
import math

import jax
import jax.numpy as jnp
from jax import lax
from jax.experimental import pallas as pl
from jax.experimental.pallas import tpu as pltpu

F32 = jnp.float32
BF16 = jnp.bfloat16
I32 = jnp.int32

D_MODEL = 1024
DEPTH = 1
DIFF_HEADS = 4
DIFF_DK = 64
DIFF_DV = 128
MLA_HEADS = 8
MLA_NOPE = 64
MLA_ROPE = 32
MLA_QK = MLA_NOPE + MLA_ROPE
MLA_V = 64
Q_LORA = 256
KV_LORA = 128
ROPE_BASE = 10000.0
N_BUCKETS = 32
MAX_DISTANCE = 128
N_EXPERTS = 16
CAPACITY_FACTOR = 2
D_FF = 2816
EPS = 1e-6

LANES = 128
SUBLANES = 8
BF16_ROWS = 16
DIFF_W = DIFF_HEADS * DIFF_DV
MLA_W = MLA_HEADS * LANES
NEG_BIG = -1e30
LOG2E = math.log2(math.e)
VMEM_LIMIT = 56 * 1024 * 1024
F32_EXP_RANGE = 120.0
BOUND_SLACK = 1.02

PREP_TM = 512
ATT_T = 512
DIFF_TK = 2048
MLA_TQ = 512
MLA_TK = 2048
OUT_TM = 512
ROUTE_RB = 256
GATHER_TG = 1024
GATHER_W = 192
GATHER_NE = 4
FFN_TF = 256
COMB_TT = 512
COMB_W = 128


def _cparams(sem, vmem=VMEM_LIMIT):
    return pltpu.CompilerParams(dimension_semantics=sem, vmem_limit_bytes=vmem)


def _rms_rows(x, gain):
    ms = jnp.mean(x * x, axis=-1, keepdims=True)
    return x * lax.rsqrt(ms + EPS) * gain


def _prep_kernel(x_ref, cos_ref, sin_ref, an_ref, wqkv_ref, wcq_ref, wckv_ref, wkpe_ref,
                 wuq_ref, wuqs_ref, wuk_ref, wuv_ref, dqg_ref, dkg_ref, cqg_ref, ckvg_ref, mqg_ref, mqgs_ref,
                 mkg_ref, qd_ref, kd_ref, vd_ref, qm_ref, km_ref, vm_ref):
    tm = x_ref.shape[0]
    hb = _rms_rows(x_ref[...], an_ref[...]).astype(BF16)

    lane = lax.broadcasted_iota(I32, (tm, LANES), 1)
    lo = lane < DIFF_DK

    qkv = jnp.dot(hb, wqkv_ref[...], preferred_element_type=F32)
    diff_scale = DIFF_DK ** -0.5 * LOG2E
    for hd in range(DIFF_HEADS):
        for off, g_ref, o_ref, scale in ((0, dqg_ref, qd_ref, diff_scale), (DIFF_W, dkg_ref, kd_ref, None)):
            blk = qkv[:, off + hd * LANES: off + (hd + 1) * LANES]
            sq = blk * blk
            s_lo = jnp.sum(jnp.where(lo, sq, 0.0), axis=-1, keepdims=True)
            s_hi = jnp.sum(jnp.where(lo, 0.0, sq), axis=-1, keepdims=True)
            r = jnp.where(lo, lax.rsqrt(s_lo / DIFF_DK + EPS), lax.rsqrt(s_hi / DIFF_DK + EPS))
            y = blk * r * g_ref[...]
            if scale is not None:
                y = y * scale
            o_ref[:, hd * LANES:(hd + 1) * LANES] = y.astype(BF16)
    vd_ref[...] = qkv[:, 2 * DIFF_W:3 * DIFF_W].astype(BF16)

    cq = _rms_rows(jnp.dot(hb, wcq_ref[...], preferred_element_type=F32), cqg_ref[...]).astype(BF16)
    ckv = _rms_rows(jnp.dot(hb, wckv_ref[...], preferred_element_type=F32), ckvg_ref[...]).astype(BF16)
    kpe = jnp.dot(hb, wkpe_ref[...], preferred_element_type=F32)
    q = jnp.dot(cq, wuq_ref[...], preferred_element_type=F32)
    q_sw = jnp.dot(cq, wuqs_ref[...], preferred_element_type=F32)
    k = jnp.dot(ckv, wuk_ref[...], preferred_element_type=F32)
    vm_ref[...] = jnp.dot(ckv, wuv_ref[...], preferred_element_type=F32).astype(BF16)

    cosb = cos_ref[...]
    sinb = sin_ref[...]
    mla_scale = MLA_QK ** -0.5 * LOG2E
    first_half = (lane >= MLA_NOPE) & (lane < MLA_NOPE + MLA_ROPE // 2)
    kpe_g = kpe * mkg_ref[...]
    kpe_rot = kpe_g * cosb + sinb * jnp.where(first_half,
                                              pltpu.roll(kpe_g, LANES - MLA_ROPE // 2, 1),
                                              pltpu.roll(kpe_g, MLA_ROPE // 2, 1))
    kpe_ss = jnp.sum(kpe * kpe, axis=-1, keepdims=True)

    for hd in range(MLA_HEADS):
        sl = slice(hd * LANES, (hd + 1) * LANES)
        qb = q[:, sl]
        rq = lax.rsqrt(jnp.sum(qb * qb, axis=-1, keepdims=True) / MLA_QK + EPS) * mla_scale
        qm_ref[:, sl] = (rq * (qb * mqg_ref[...] * cosb + q_sw[:, sl] * mqgs_ref[...] * sinb)).astype(BF16)
        kb = k[:, sl]
        rk = lax.rsqrt((jnp.sum(kb * kb, axis=-1, keepdims=True) + kpe_ss) / MLA_QK + EPS)
        kn = rk * (kb * mkg_ref[...] + kpe_rot)
        km_ref[:, sl] = jnp.where(lane == MLA_QK, 1.0, kn).astype(BF16)


def _prep(x2d, seq, cos_t, sin_t, p):
    t = x2d.shape[0]
    tm = PREP_TM
    nseq = seq // tm
    row = lambda i: (i, 0)
    full = lambda i: (0, 0)
    wspec = lambda a: pl.BlockSpec(a.shape, full)
    weights = (p["attn_norm"], p["w_qkv"], p["w_cq"], p["w_ckv"], p["w_kpe"], p["w_uq"], p["w_uq_sw"], p["w_uk"],
               p["w_uv"], p["dq_gain"], p["dk_gain"], p["cq_gain"], p["ckv_gain"], p["mq_gain"], p["mq_gain_sw"],
               p["mk_gain"])
    out_w = (DIFF_W, DIFF_W, DIFF_W, MLA_W, MLA_W, MLA_HEADS * MLA_V)
    return pl.pallas_call(
        _prep_kernel,
        grid=(t // tm,),
        in_specs=[pl.BlockSpec((tm, D_MODEL), row),
                  pl.BlockSpec((tm, LANES), lambda i: (i % nseq, 0)),
                  pl.BlockSpec((tm, LANES), lambda i: (i % nseq, 0))] + [wspec(w) for w in weights],
        out_specs=[pl.BlockSpec((tm, w), row) for w in out_w],
        out_shape=[jax.ShapeDtypeStruct((t, w), BF16) for w in out_w],
        compiler_params=_cparams(("parallel",)),
        name="prep",
    )(x2d, cos_t, sin_t, *weights)


def _bias_kernel(rb_ref, o_ref):
    hd = pl.program_id(0)
    d = pl.program_id(1)
    tq, tk = o_ref.shape
    row = lax.broadcasted_iota(I32, (tq, tk), 0)
    col = lax.broadcasted_iota(I32, (tq, tk), 1)
    rel = col - row + (d - 1) * tk
    half = N_BUCKETS // 2
    max_exact = half // 2
    n = jnp.abs(rel)
    nf = jnp.maximum(n, 1).astype(F32)
    large = max_exact + (jnp.log(nf / max_exact) / math.log(MAX_DISTANCE / max_exact)
                         * (half - max_exact)).astype(I32)
    large = jnp.minimum(large, half - 1)
    bucket = jnp.where(rel > 0, half, 0) + jnp.where(n < max_exact, n, large)
    acc = jnp.zeros((tq, tk), F32)
    for b in range(N_BUCKETS):
        acc = jnp.where(bucket == b, rb_ref[hd, b], acc)
    o_ref[...] = acc * LOG2E


def _bias_tiles(rel_bias_t, t):
    return pl.pallas_call(
        _bias_kernel,
        grid=(DIFF_HEADS, 3),
        in_specs=[pl.BlockSpec(memory_space=pltpu.SMEM)],
        out_specs=pl.BlockSpec((None, None, t, t), lambda h, d: (h, d, 0, 0)),
        out_shape=jax.ShapeDtypeStruct((DIFF_HEADS, 3, t, t), F32),
        compiler_params=_cparams(("parallel", "parallel")),
        name="bias_tiles",
    )(rel_bias_t)


def _online_update(mi, s, v, m_scr, l_scr, acc_scr):
    blocks = [s[:, i:i + LANES] for i in range(0, s.shape[1], LANES)]
    m_old = m_scr[mi]
    row_max = jnp.max(functools.reduce(jnp.maximum, blocks), axis=-1, keepdims=True)
    m_new = jnp.maximum(m_old, row_max)
    alpha = jnp.exp2(m_old - m_new)
    ps = [jnp.exp2(blk - m_new) for blk in blocks]
    l_scr[mi] = alpha * l_scr[mi] + functools.reduce(jnp.add, ps)
    p = jnp.concatenate([x.astype(BF16) for x in ps], axis=1)
    acc_scr[mi] = alpha * acc_scr[mi] + jnp.dot(p, v, preferred_element_type=F32)
    m_scr[mi] = m_new


def _fixed_ref_update(mi, s, v, l_scr, acc_scr):
    ps = [jnp.exp2(s[:, i:i + LANES]) for i in range(0, s.shape[1], LANES)]
    l_scr[mi] += functools.reduce(jnp.add, ps)
    p = jnp.concatenate([x.astype(BF16) for x in ps], axis=1)
    acc_scr[mi] += jnp.dot(p, v, preferred_element_type=F32)


def _normalised(mi, l_scr, acc_scr):
    return acc_scr[mi] / jnp.sum(l_scr[mi], axis=-1, keepdims=True)


def _qk(q, k):
    return lax.dot_general(q, k, (((1,), (1,)), ((), ())), preferred_element_type=F32)


def _diff_attn_kernel(rb_ref, bound_ref, q_ref, k_ref, v_ref, bias_ref, subln_ref, lq1_ref, lk1_ref, lq2_ref,
                      lk2_ref, o_ref, m_scr, l_scr, acc_scr, *, lam_init, tk):
    hd = pl.program_id(1)
    qi = pl.program_id(2)
    t = q_ref.shape[0]
    n = k_ref.shape[0] // t
    r = tk // t
    bound = bound_ref[0]
    fixed_ref = bound_ref[1] > 0.5

    q = q_ref[...]
    lane = lax.broadcasted_iota(I32, q.shape, 1)
    zero = jnp.zeros_like(q)
    qs = (jnp.where(lane < DIFF_DK, q, zero), jnp.where(lane < DIFF_DK, zero, q))

    l_scr[...] = jnp.zeros(l_scr.shape, F32)
    acc_scr[...] = jnp.zeros(acc_scr.shape, F32)

    lo = jnp.maximum(qi - 1, 0)
    hi = jnp.minimum(qi + 2, n)
    lo_long = lo // r
    hi_long = (hi + r - 1) // r
    c_left = rb_ref[hd, N_BUCKETS // 2 - 1] * LOG2E
    c_right = rb_ref[hd, N_BUCKETS - 1] * LOG2E

    def sweep(update, after_left, after_near):
        def make_step(size, near):
            def step(j, carry):
                off = pl.multiple_of(j * size, size)
                k = k_ref[pl.ds(off, size), :]
                v = v_ref[pl.ds(off, size), :]
                for mi in range(2):
                    s = _qk(qs[mi], k)
                    if near:
                        s = s + bias_ref[j - qi + 1]
                    update(mi, s, v)
                return carry
            return step

        lax.fori_loop(0, lo_long, make_step(tk, False), 0)
        lax.fori_loop(lo_long * r, lo, make_step(t, False), 0)
        after_left()
        lax.fori_loop(lo, hi, make_step(t, True), 0)
        after_near()
        lax.fori_loop(hi, hi_long * r, make_step(t, False), 0)
        lax.fori_loop(hi_long, n // r, make_step(tk, False), 0)

    @pl.when(fixed_ref)
    def _():
        def rescale(c):
            factor = jnp.exp2(jnp.full((1, LANES), c, F32))
            l_scr[...] = l_scr[...] * factor
            acc_scr[...] = acc_scr[...] * factor

        sweep(lambda mi, s, v: _fixed_ref_update(mi, s - bound, v, l_scr, acc_scr),
              lambda: rescale(c_left), lambda: rescale(-c_right))

    @pl.when(jnp.logical_not(fixed_ref))
    def _():
        m_scr[...] = jnp.full(m_scr.shape, NEG_BIG, F32)

        def shift(c):
            m_scr[...] = m_scr[...] + c

        sweep(lambda mi, s, v: _online_update(mi, s, v, m_scr, l_scr, acc_scr),
              lambda: shift(c_left), lambda: shift(-c_right))

    lam = (jnp.exp(jnp.sum(lq1_ref[...] * lk1_ref[...], axis=-1, keepdims=True))
           - jnp.exp(jnp.sum(lq2_ref[...] * lk2_ref[...], axis=-1, keepdims=True)) + lam_init)
    o = _normalised(0, l_scr, acc_scr) - lam * _normalised(1, l_scr, acc_scr)
    o_ref[...] = (_rms_rows(o, subln_ref[...]) * (1.0 - lam_init)).astype(BF16)


def _diff_attention(qd, kd, vd, bias_tiles, rel_bias_t, p, lam_init):
    b, s, _ = qd.shape
    t = ATT_T
    vec = lambda a: pl.BlockSpec(a.shape, lambda bi, h, i: (0, 0))
    kv_spec = pl.BlockSpec((None, s, LANES), lambda bi, h, i: (bi, 0, h))
    return pl.pallas_call(
        functools.partial(_diff_attn_kernel, lam_init=lam_init, tk=DIFF_TK),
        grid=(b, DIFF_HEADS, s // t),
        in_specs=[pl.BlockSpec(memory_space=pltpu.SMEM), pl.BlockSpec(memory_space=pltpu.SMEM),
                  pl.BlockSpec((None, t, LANES), lambda bi, h, i: (bi, i, h)),
                  kv_spec, kv_spec,
                  pl.BlockSpec((None, 3, t, t), lambda bi, h, i: (h, 0, 0, 0)),
                  vec(p["subln"]), vec(p["lq1"]), vec(p["lk1"]), vec(p["lq2"]), vec(p["lk2"])],
        out_specs=pl.BlockSpec((None, t, LANES), lambda bi, h, i: (bi, i, h)),
        out_shape=jax.ShapeDtypeStruct((b, s, DIFF_W), BF16),
        scratch_shapes=[pltpu.VMEM((2, t, LANES), F32)] * 3,
        compiler_params=_cparams(("parallel", "parallel", "arbitrary")),
        name="diff_attn",
    )(rel_bias_t, p["diff_bound"], qd, kd, vd, bias_tiles, p["subln"], p["lq1"], p["lk1"], p["lq2"], p["lk2"])


def _mla_attn_kernel(bound_ref, q_ref, k_ref, v_ref, o_ref, m_scr, l_scr, acc_scr, *, tk):
    t = q_ref.shape[0]
    n = k_ref.shape[0] // tk
    bound = bound_ref[0]
    fixed_ref = bound_ref[1] > 0.5

    l_scr[...] = jnp.zeros(l_scr.shape, F32)
    acc_scr[...] = jnp.zeros(acc_scr.shape, F32)

    def sweep(q, update):
        qs = (q[:, :LANES], q[:, LANES:])

        def step(j, carry):
            off = pl.multiple_of(j * tk, tk)
            v = v_ref[pl.ds(off, tk), :]
            for mi in range(2):
                k = k_ref[pl.ds(off, tk), mi * LANES:(mi + 1) * LANES]
                update(mi, _qk(qs[mi], k), v)
            return carry

        lax.fori_loop(0, n, step, 0)

    @pl.when(fixed_ref)
    def _():
        q = q_ref[...]
        lane = lax.broadcasted_iota(I32, q.shape, 1)
        q = jnp.where(lane % LANES == MLA_QK, jnp.full(q.shape, -bound, F32).astype(BF16), q)
        sweep(q, lambda mi, s, v: _fixed_ref_update(mi, s, v, l_scr, acc_scr))

    @pl.when(jnp.logical_not(fixed_ref))
    def _():
        m_scr[...] = jnp.full(m_scr.shape, NEG_BIG, F32)
        sweep(q_ref[...], lambda mi, s, v: _online_update(mi, s, v, m_scr, l_scr, acc_scr))

    lane = lax.broadcasted_iota(I32, (t, LANES), 1)
    o = jnp.where(lane < MLA_V, _normalised(0, l_scr, acc_scr), _normalised(1, l_scr, acc_scr))
    o_ref[...] = o.astype(BF16)


def _mla_attention(qm, km, vm, bound):
    b, s, _ = qm.shape
    t = MLA_TQ
    return pl.pallas_call(
        functools.partial(_mla_attn_kernel, tk=MLA_TK),
        grid=(b, MLA_HEADS // 2, s // t),
        in_specs=[pl.BlockSpec(memory_space=pltpu.SMEM),
                  pl.BlockSpec((None, t, 2 * LANES), lambda bi, h, i: (bi, i, h)),
                  pl.BlockSpec((None, s, 2 * LANES), lambda bi, h, i: (bi, 0, h)),
                  pl.BlockSpec((None, s, LANES), lambda bi, h, i: (bi, 0, h))],
        out_specs=pl.BlockSpec((None, t, LANES), lambda bi, h, i: (bi, i, h)),
        out_shape=jax.ShapeDtypeStruct((b, s, MLA_HEADS * MLA_V), BF16),
        scratch_shapes=[pltpu.VMEM((2, t, LANES), F32)] * 3,
        compiler_params=_cparams(("parallel", "parallel", "arbitrary")),
        name="mla_attn",
    )(bound, qm, km, vm)


def _split_bf16(x):
    hi = x.astype(BF16)
    lo = (x - hi.astype(F32)).astype(BF16)
    return hi, lo


def _outproj_kernel(x_ref, od_ref, om_ref, wo1_ref, wo2_ref, fn_ref, wrh_ref, wrl_ref,
                    x1_ref, h2_ref, aff_ref):
    x1 = (x_ref[...]
          + jnp.dot(od_ref[...], wo1_ref[...], preferred_element_type=F32)
          + jnp.dot(om_ref[...], wo2_ref[...], preferred_element_type=F32))
    x1_ref[...] = x1
    h = _rms_rows(x1, fn_ref[...])
    hh, hl = _split_bf16(h)
    h2_ref[...] = hh
    logits = (jnp.dot(hh, wrh_ref[...], preferred_element_type=F32)
              + jnp.dot(hl, wrh_ref[...], preferred_element_type=F32)
              + jnp.dot(hh, wrl_ref[...], preferred_element_type=F32))
    lane = lax.broadcasted_iota(I32, logits.shape, 1)
    logits = jnp.where(lane < N_EXPERTS, logits, NEG_BIG)
    e = jnp.exp(logits - jnp.max(logits, axis=-1, keepdims=True))
    aff = e / jnp.sum(e, axis=-1, keepdims=True)
    aff_ref[...] = aff[:, :N_EXPERTS]


def _outproj(x2d, od, om, p):
    t = x2d.shape[0]
    tm = OUT_TM
    row = lambda i: (i, 0)
    wspec = lambda a: pl.BlockSpec(a.shape, lambda i: (0, 0))
    weights = (p["w_o1"], p["w_o2"], p["ffn_norm"], p["w_r_hi"], p["w_r_lo"])
    return pl.pallas_call(
        _outproj_kernel,
        grid=(t // tm,),
        in_specs=[pl.BlockSpec((tm, D_MODEL), row), pl.BlockSpec((tm, DIFF_W), row),
                  pl.BlockSpec((tm, MLA_HEADS * MLA_V), row)] + [wspec(w) for w in weights],
        out_specs=[pl.BlockSpec((tm, D_MODEL), row), pl.BlockSpec((tm, D_MODEL), row),
                   pl.BlockSpec((tm, N_EXPERTS), row)],
        out_shape=[jax.ShapeDtypeStruct((t, D_MODEL), F32), jax.ShapeDtypeStruct((t, D_MODEL), BF16),
                   jax.ShapeDtypeStruct((t, N_EXPERTS), F32)],
        compiler_params=_cparams(("parallel",)),
        name="outproj",
    )(x2d, od, om, *weights)


def _route_kernel(aff_ref, pos_ref, gate_ref, csx_ref, *, cap):
    rows = aff_ref.shape[0]
    groups = LANES // N_EXPERTS
    aff = aff_ref[...]
    bits = pltpu.bitcast(aff, I32)

    def expert_total(v):
        for sh in (N_EXPERTS, 2 * N_EXPERTS, 4 * N_EXPERTS):
            v = v + pltpu.roll(v, sh, 1)
        return v

    def count(mask):
        return expert_total(jnp.sum(mask.astype(I32), axis=0, keepdims=True))

    def search(i, thr):
        cand = thr | jnp.left_shift(jnp.int32(1), 30 - i)
        return jnp.where(count(bits >= cand) >= cap, cand, thr)

    thr = lax.fori_loop(0, 31, search, jnp.zeros((1, LANES), I32))

    a = lax.broadcasted_iota(I32, (LANES, LANES), 0)
    b = lax.broadcasted_iota(I32, (LANES, LANES), 1)
    same = (a % N_EXPERTS) == (b % N_EXPERTS)
    q_all = same.astype(BF16)
    q_before = (same & (a // N_EXPERTS < b // N_EXPERTS)).astype(BF16)
    rb = ROUTE_RB
    ra = lax.broadcasted_iota(I32, (rb, rb), 0)
    ca = lax.broadcasted_iota(I32, (rb, rb), 1)
    tri = (ca < ra).astype(BF16)

    def prefix(mask):
        mb = mask.astype(BF16)
        outs = []
        offset = jnp.zeros((1, LANES), F32)
        for r0 in range(0, rows, rb):
            blk = mb[r0:r0 + rb]
            tot = jnp.dot(blk, q_all, preferred_element_type=F32)
            within = jnp.dot(blk, q_before, preferred_element_type=F32)
            above = jnp.dot(tri, tot.astype(BF16), preferred_element_type=F32)
            outs.append(above + within + offset)
            offset = offset + jnp.sum(tot, axis=0, keepdims=True)
        return jnp.concatenate(outs, axis=0)

    gt = bits > thr
    eq = bits == thr
    need = (cap - count(gt)).astype(F32)
    sel = gt | (eq & (prefix(eq) < need))
    csx = prefix(sel)
    pos_ref[...] = jnp.where(sel, csx.astype(I32), -1)
    gate_ref[...] = jnp.where(sel, aff, 0.0)
    csx_ref[...] = csx.astype(I32)


def _route(aff_packed, cap):
    rows = aff_packed.shape[0]
    vm = pl.BlockSpec(memory_space=pltpu.VMEM)
    return pl.pallas_call(
        functools.partial(_route_kernel, cap=cap),
        in_specs=[vm],
        out_specs=[vm, vm, vm],
        out_shape=[jax.ShapeDtypeStruct((rows, LANES), I32), jax.ShapeDtypeStruct((rows, LANES), F32),
                   jax.ShapeDtypeStruct((rows, LANES), I32)],
        compiler_params=pltpu.CompilerParams(vmem_limit_bytes=VMEM_LIMIT),
        name="route",
    )(aff_packed)


def _window_start(first_slot, align, w, cap):
    return jnp.minimum((first_slot // align) * align, cap - w)


def _gather_kernel(p0_ref, pn_ref, pos_ref, h2_ref, xg_ref, *, cap, w):
    ne = xg_ref.shape[0]
    e0 = pl.program_id(0) * ne
    i = pl.program_id(1)
    tg = h2_ref.shape[0]

    @pl.when(i == 0)
    def _():
        xg_ref[...] = jnp.zeros(xg_ref.shape, BF16)

    for j in range(ne):
        w0 = _window_start(p0_ref[e0 + j, i], BF16_ROWS, w, cap)
        n_win = (pn_ref[e0 + j, i] - w0 + w - 1) // w
        prow = pos_ref[j, pl.ds(i, 1), :]

        def window(k, carry, j=j, w0=w0, prow=prow):
            nominal = w0 + k * w
            start = pl.multiple_of(jnp.minimum(nominal, cap - w), BF16_ROWS)
            slot = lax.broadcasted_iota(I32, (w, tg), 0) + start
            onehot = ((prow == slot) & (prow >= nominal)).astype(BF16)
            rows = jnp.dot(onehot, h2_ref[...], preferred_element_type=F32).astype(BF16)
            xg_ref[j, pl.ds(start, w), :] += rows
            return carry

        lax.fori_loop(0, n_win, window, 0)


def _gather(p0, pn, pos_t, h2, cap):
    t = h2.shape[0]
    tg, ne = GATHER_TG, GATHER_NE
    return pl.pallas_call(
        functools.partial(_gather_kernel, cap=cap, w=GATHER_W),
        grid_spec=pltpu.PrefetchScalarGridSpec(
            num_scalar_prefetch=2,
            grid=(N_EXPERTS // ne, t // tg),
            in_specs=[pl.BlockSpec((ne, t // tg, tg), lambda e, i, *_: (e, 0, 0)),
                      pl.BlockSpec((tg, D_MODEL), lambda e, i, *_: (i, 0))],
            out_specs=pl.BlockSpec((ne, cap, D_MODEL), lambda e, i, *_: (e, 0, 0))),
        out_shape=jax.ShapeDtypeStruct((N_EXPERTS, cap, D_MODEL), BF16),
        compiler_params=_cparams(("parallel", "arbitrary")),
        name="gather",
    )(p0, pn, pos_t, h2)


def _ffn_kernel(xg_ref, wg_ref, wu_ref, wd_ref, y_ref, acc_scr):
    f = pl.program_id(1)

    @pl.when((pl.program_id(0) == 0) & (f == 0))
    def _():
        acc_scr[...] = jnp.zeros(acc_scr.shape, F32)

    xg = xg_ref[...]
    a = jnp.dot(xg, wg_ref[...].astype(BF16), preferred_element_type=F32)
    b = jnp.dot(xg, wu_ref[...].astype(BF16), preferred_element_type=F32)
    act = (a * jax.nn.sigmoid(a) * b).astype(BF16)
    part = jnp.dot(act, wd_ref[...].astype(BF16), preferred_element_type=F32)
    acc_scr[...] = jnp.where(f == 0, part, acc_scr[...] + part)

    @pl.when(f == pl.num_programs(1) - 1)
    def _():
        y_ref[...] = acc_scr[...].astype(BF16)


def _ffn(xg, w_gate, w_up, w_down):
    _, cap, _ = xg.shape
    tf = FFN_TF
    return pl.pallas_call(
        _ffn_kernel,
        grid=(N_EXPERTS, D_FF // tf),
        in_specs=[pl.BlockSpec((None, cap, D_MODEL), lambda e, f: (e, 0, 0)),
                  pl.BlockSpec((None, D_MODEL, tf), lambda e, f: (e, 0, f)),
                  pl.BlockSpec((None, D_MODEL, tf), lambda e, f: (e, 0, f)),
                  pl.BlockSpec((None, tf, D_MODEL), lambda e, f: (e, f, 0))],
        out_specs=pl.BlockSpec((None, cap, D_MODEL), lambda e, f: (e, 0, 0)),
        out_shape=jax.ShapeDtypeStruct((N_EXPERTS, cap, D_MODEL), BF16),
        scratch_shapes=[pltpu.VMEM((cap, D_MODEL), F32)],
        compiler_params=_cparams(("arbitrary", "arbitrary")),
        name="ffn",
    )(xg, w_gate, w_up, w_down)


def _combine_kernel(p0_ref, pn_ref, x1_ref, pos_ref, gate_ref, y_hbm, o_ref, win, xwin, sem, xsem, *, cap):
    i = pl.program_id(0)
    tt = x1_ref.shape[0]
    w = xwin.shape[0]

    def first_window(e):
        return _window_start(p0_ref[i, e], BF16_ROWS, w, cap)

    def window_copy(e):
        return pltpu.make_async_copy(y_hbm.at[e, pl.ds(first_window(e), w), :],
                                     win.at[pl.ds(e * w, w), :], sem.at[e])

    for e in range(N_EXPERTS):
        window_copy(e).start()

    lane = lax.broadcasted_iota(I32, (tt, w), 1)

    def weights(e, first_slot, lo_slot):
        pcol = pos_ref[:, e:e + 1]
        hit = (pcol - first_slot == lane) & (pcol >= lo_slot)
        return jnp.where(hit, gate_ref[:, e:e + 1], 0.0).astype(BF16)

    c = jnp.concatenate([weights(e, first_window(e), 0) for e in range(N_EXPERTS)], axis=1)
    for e in range(N_EXPERTS):
        window_copy(e).wait()
    o_ref[...] = x1_ref[...] + jnp.dot(c, win[...], preferred_element_type=F32)

    for e in range(N_EXPERTS):
        w0 = first_window(e)
        n_win = (pn_ref[i, e] - w0 + w - 1) // w

        def extra(k, carry, e=e, w0=w0):
            nominal = w0 + k * w
            start = jnp.minimum(nominal, cap - w)
            cp = pltpu.make_async_copy(y_hbm.at[e, pl.ds(start, w), :], xwin, xsem)
            cp.start()
            cp.wait()
            o_ref[...] += jnp.dot(weights(e, start, nominal), xwin[...], preferred_element_type=F32)
            return carry

        lax.fori_loop(1, n_win, extra, 0)


def _combine(p0, pn, x1, pos_tok, gate_tok, y, cap):
    t = x1.shape[0]
    tt, w = COMB_TT, COMB_W
    row = lambda i, *_: (i, 0)
    return pl.pallas_call(
        functools.partial(_combine_kernel, cap=cap),
        grid_spec=pltpu.PrefetchScalarGridSpec(
            num_scalar_prefetch=2,
            grid=(t // tt,),
            in_specs=[pl.BlockSpec((tt, D_MODEL), row), pl.BlockSpec((tt, N_EXPERTS), row),
                      pl.BlockSpec((tt, N_EXPERTS), row), pl.BlockSpec(memory_space=pl.ANY)],
            out_specs=pl.BlockSpec((tt, D_MODEL), row),
            scratch_shapes=[pltpu.VMEM((N_EXPERTS * w, D_MODEL), BF16), pltpu.VMEM((w, D_MODEL), BF16),
                            pltpu.SemaphoreType.DMA((N_EXPERTS,)), pltpu.SemaphoreType.DMA(())]),
        out_shape=jax.ShapeDtypeStruct((t, D_MODEL), F32),
        compiler_params=_cparams(("arbitrary",)),
        name="combine",
    )(p0, pn, x1, pos_tok, gate_tok, y)


def _score_bound(width, q_gain, k_gain, max_bias):
    gmax = lambda g: jnp.max(jnp.abs(g.astype(F32)))
    bound = BOUND_SLACK * (width ** 0.5 * LOG2E * gmax(q_gain) * gmax(k_gain) + LOG2E * max_bias)
    return jnp.stack([bound, (3.0 * bound < F32_EXP_RANGE).astype(F32)]).astype(F32)


def _layer_params(l, rel_bias, attn_norm, w_in, diff_q_norm, diff_k_norm, lambda_q1, lambda_k1, lambda_q2,
                  lambda_k2, diff_subln, mla_q_latent_norm, mla_kv_latent_norm, w_uq, w_ukv, mla_q_norm,
                  mla_k_norm, w_out, ffn_norm, w_router):
    row = lambda v: v.reshape(1, -1).astype(F32)
    w = w_in[l]
    o = 3 * DIFF_W
    pad_rope = MLA_NOPE, LANES - MLA_QK
    w_kpe = jnp.pad(w[:, o + Q_LORA + KV_LORA:], ((0, 0), pad_rope))
    uq = jnp.pad(w_uq[l].reshape(Q_LORA, MLA_HEADS, MLA_QK), ((0, 0), (0, 0), (0, LANES - MLA_QK)))
    ukv = w_ukv[l].reshape(KV_LORA, MLA_HEADS, MLA_NOPE + MLA_V)
    uk = jnp.pad(ukv[:, :, :MLA_NOPE], ((0, 0), (0, 0), (0, LANES - MLA_NOPE)))
    head_gain = lambda g: jnp.pad(g.astype(F32), (0, LANES - MLA_QK)).reshape(1, LANES)
    half = MLA_ROPE // 2

    def swap_rope_halves(a):
        z = jnp.zeros_like(a)
        x1, x2 = a[..., MLA_NOPE:MLA_NOPE + half], a[..., MLA_NOPE + half:MLA_QK]
        return jnp.concatenate([z[..., :MLA_NOPE], x2, x1, z[..., MLA_QK:]], axis=-1)

    wr = jnp.pad(w_router[l].astype(F32), ((0, 0), (0, LANES - N_EXPERTS)))
    wr_hi = wr.astype(BF16)
    return dict(
        attn_norm=row(attn_norm[l]),
        w_qkv=w[:, :o].astype(BF16), w_cq=w[:, o:o + Q_LORA].astype(BF16),
        w_ckv=w[:, o + Q_LORA:o + Q_LORA + KV_LORA].astype(BF16), w_kpe=w_kpe.astype(BF16),
        w_uq=uq.reshape(Q_LORA, MLA_W).astype(BF16), w_uk=uk.reshape(KV_LORA, MLA_W).astype(BF16),
        w_uq_sw=swap_rope_halves(uq).reshape(Q_LORA, MLA_W).astype(BF16),
        mq_gain_sw=swap_rope_halves(head_gain(mla_q_norm[l])),
        w_uv=ukv[:, :, MLA_NOPE:].reshape(KV_LORA, MLA_HEADS * MLA_V).astype(BF16),
        dq_gain=row(jnp.tile(diff_q_norm[l], 2)), dk_gain=row(jnp.tile(diff_k_norm[l], 2)),
        cq_gain=row(mla_q_latent_norm[l]), ckv_gain=row(mla_kv_latent_norm[l]),
        mq_gain=head_gain(mla_q_norm[l]), mk_gain=head_gain(mla_k_norm[l]),
        subln=row(diff_subln[l]), lq1=row(lambda_q1[l]), lk1=row(lambda_k1[l]),
        lq2=row(lambda_q2[l]), lk2=row(lambda_k2[l]),
        rel_bias_t=rel_bias.astype(F32).T,
        diff_bound=_score_bound(DIFF_DK, diff_q_norm[l], diff_k_norm[l], jnp.max(jnp.abs(rel_bias))),
        mla_bound=_score_bound(MLA_QK, mla_q_norm[l], mla_k_norm[l], 0.0),
        w_o1=w_out[l][:DIFF_W].astype(BF16), w_o2=w_out[l][DIFF_W:].astype(BF16),
        ffn_norm=row(ffn_norm[l]), w_r_hi=wr_hi, w_r_lo=(wr - wr_hi.astype(F32)).astype(BF16),
    )


def _rope_tables(seq):
    half = MLA_ROPE // 2
    inv = 1.0 / (ROPE_BASE ** (jnp.arange(half, dtype=F32) / half))
    ang = jnp.arange(seq, dtype=jnp.int32).astype(F32)[:, None] * inv[None, :]
    cos, sin = jnp.cos(ang), jnp.sin(ang)
    ones = jnp.ones((seq, MLA_NOPE), F32)
    tail = LANES - MLA_QK
    cos_t = jnp.concatenate([ones, cos, cos, jnp.ones((seq, tail), F32)], axis=1)
    sin_t = jnp.concatenate([0 * ones, -sin, sin, jnp.zeros((seq, tail), F32)], axis=1)
    return cos_t, sin_t


def _layer(x, p, bias, w_gate, w_up, w_down, lam_init):
    b, s, d = x.shape
    t = b * s
    cap = CAPACITY_FACTOR * t // N_EXPERTS
    x2d = x.reshape(t, d)

    cos_t, sin_t = _rope_tables(s)
    qd, kd, vd, qm, km, vm = _prep(x2d, s, cos_t, sin_t, p)
    shp = lambda a: a.reshape(b, s, a.shape[-1])
    od = _diff_attention(shp(qd), shp(kd), shp(vd), bias, p["rel_bias_t"], p, lam_init)
    om = _mla_attention(shp(qm), shp(km), shp(vm), p["mla_bound"])
    x1, h2, aff = _outproj(x2d, od.reshape(t, -1), om.reshape(t, -1), p)

    groups = LANES // N_EXPERTS
    pos_p, gate_p, csx_p = _route(aff.reshape(t // groups, LANES), cap)
    pos_tok = pos_p.reshape(t, N_EXPERTS)
    gate_tok = gate_p.reshape(t, N_EXPERTS)
    csx_tok = csx_p.reshape(t, N_EXPERTS)

    def tile_slots(tile):
        first = csx_tok[::tile]
        return first, jnp.concatenate([first[1:], jnp.full((1, N_EXPERTS), cap, I32)], axis=0)

    g0, gn = tile_slots(GATHER_TG)
    pos_t = pos_tok.T.reshape(N_EXPERTS, t // GATHER_TG, GATHER_TG)
    xg = _gather(g0.T, gn.T, pos_t, h2, cap)
    y = _ffn(xg, w_gate, w_up, w_down)
    c0, cn = tile_slots(COMB_TT)
    out = _combine(c0, cn, x1, pos_tok, gate_tok, y, cap)
    return out.reshape(b, s, d)


def kernel(x_prompt, x_sample, rel_bias, attn_norm, w_in, diff_q_norm, diff_k_norm, lambda_q1, lambda_k1, lambda_q2, lambda_k2, diff_subln, mla_q_latent_norm, mla_kv_latent_norm, w_uq, w_ukv, mla_q_norm, mla_k_norm, w_out, ffn_norm, w_router, w_gate, w_up, w_down):
    params = [_layer_params(l, rel_bias, attn_norm, w_in, diff_q_norm, diff_k_norm, lambda_q1, lambda_k1,
                            lambda_q2, lambda_k2, diff_subln, mla_q_latent_norm, mla_kv_latent_norm, w_uq,
                            w_ukv, mla_q_norm, mla_k_norm, w_out, ffn_norm, w_router) for l in range(DEPTH)]
    biases = [_bias_tiles(p["rel_bias_t"], ATT_T) for p in params]
    outs = []
    for x in (x_prompt, x_sample):
        for l in range(DEPTH):
            lam_init = 0.8 - 0.6 * math.exp(-0.3 * l)
            x = _layer(x, params[l], biases[l], w_gate[l], w_up[l], w_down[l], lam_init)
        outs.append(x)
    return tuple(outs)
```

```python
import functools
import math

import jax
import jax.numpy as jnp
from jax import lax
from jax.experimental import pallas as pl
from jax.experimental.pallas import tpu as pltpu

F32 = jnp.float32
BF16 = jnp.bfloat16
I32 = jnp.int32

D_MODEL = 1024
DEPTH = 1
DIFF_HEADS = 4
DIFF_DK = 64
DIFF_DV = 128
MLA_HEADS = 8
MLA_NOPE = 64
MLA_ROPE = 32
MLA_QK = MLA_NOPE + MLA_ROPE
MLA_V = 64
Q_LORA = 256
KV_LORA = 128
ROPE_BASE = 10000.0
N_BUCKETS = 32
MAX_DISTANCE = 128
N_EXPERTS = 16
CAPACITY_FACTOR = 2
D_FF = 2816
EPS = 1e-6

LANES = 128
SUBLANES = 8
BF16_ROWS = 16
DIFF_W = DIFF_HEADS * DIFF_DV
MLA_W = MLA_HEADS * LANES
NEG_BIG = -1e30
LOG2E = math.log2(math.e)
VMEM_LIMIT = 56 * 1024 * 1024
F32_EXP_RANGE = 120.0
BOUND_SLACK = 1.02

PREP_TM = 512
ATT_T = 512
DIFF_TK = 2048
MLA_TQ = 512
MLA_TK = 2048
OUT_TM = 512
ROUTE_RB = 256
GATHER_TG = 1024
GATHER_W = 192
GATHER_NE = 4
FFN_TF = 256
COMB_TT = 512
COMB_W = 128


def _cparams(sem, vmem=VMEM_LIMIT):
    return pltpu.CompilerParams(dimension_semantics=sem, vmem_limit_bytes=vmem)


def _rms_rows(x, gain):
    ms = jnp.mean(x * x, axis=-1, keepdims=True)
    return x * lax.rsqrt(ms + EPS) * gain


def _prep_kernel(x_ref, cos_ref, sin_ref, an_ref, wqkv_ref, wcq_ref, wckv_ref, wkpe_ref,
                 wuq_ref, wuqs_ref, wuk_ref, wuv_ref, dqg_ref, dkg_ref, cqg_ref, ckvg_ref, mqg_ref, mqgs_ref,
                 mkg_ref, qd_ref, kd_ref, vd_ref, qm_ref, km_ref, vm_ref):
    tm = x_ref.shape[0]
    hb = _rms_rows(x_ref[...], an_ref[...]).astype(BF16)

    lane = lax.broadcasted_iota(I32, (tm, LANES), 1)
    lo = lane < DIFF_DK

    qkv = jnp.dot(hb, wqkv_ref[...], preferred_element_type=F32)
    diff_scale = DIFF_DK ** -0.5 * LOG2E
    for hd in range(DIFF_HEADS):
        for off, g_ref, o_ref, scale in ((0, dqg_ref, qd_ref, diff_scale), (DIFF_W, dkg_ref, kd_ref, None)):
            blk = qkv[:, off + hd * LANES: off + (hd + 1) * LANES]
            sq = blk * blk
            s_lo = jnp.sum(jnp.where(lo, sq, 0.0), axis=-1, keepdims=True)
            s_hi = jnp.sum(jnp.where(lo, 0.0, sq), axis=-1, keepdims=True)
            r = jnp.where(lo, lax.rsqrt(s_lo / DIFF_DK + EPS), lax.rsqrt(s_hi / DIFF_DK + EPS))
            y = blk * r * g_ref[...]
            if scale is not None:
                y = y * scale
            o_ref[:, hd * LANES:(hd + 1) * LANES] = y.astype(BF16)
    vd_ref[...] = qkv[:, 2 * DIFF_W:3 * DIFF_W].astype(BF16)

    cq = _rms_rows(jnp.dot(hb, wcq_ref[...], preferred_element_type=F32), cqg_ref[...]).astype(BF16)
    ckv = _rms_rows(jnp.dot(hb, wckv_ref[...], preferred_element_type=F32), ckvg_ref[...]).astype(BF16)
    kpe = jnp.dot(hb, wkpe_ref[...], preferred_element_type=F32)
    q = jnp.dot(cq, wuq_ref[...], preferred_element_type=F32)
    q_sw = jnp.dot(cq, wuqs_ref[...], preferred_element_type=F32)
    k = jnp.dot(ckv, wuk_ref[...], preferred_element_type=F32)
    vm_ref[...] = jnp.dot(ckv, wuv_ref[...], preferred_element_type=F32).astype(BF16)

    cosb = cos_ref[...]
    sinb = sin_ref[...]
    mla_scale = MLA_QK ** -0.5 * LOG2E
    first_half = (lane >= MLA_NOPE) & (lane < MLA_NOPE + MLA_ROPE // 2)
    kpe_g = kpe * mkg_ref[...]
    kpe_rot = kpe_g * cosb + sinb * jnp.where(first_half,
                                              pltpu.roll(kpe_g, LANES - MLA_ROPE // 2, 1),
                                              pltpu.roll(kpe_g, MLA_ROPE // 2, 1))
    kpe_ss = jnp.sum(kpe * kpe, axis=-1, keepdims=True)

    for hd in range(MLA_HEADS):
        sl = slice(hd * LANES, (hd + 1) * LANES)
        qb = q[:, sl]
        rq = lax.rsqrt(jnp.sum(qb * qb, axis=-1, keepdims=True) / MLA_QK + EPS) * mla_scale
        qm_ref[:, sl] = (rq * (qb * mqg_ref[...] * cosb + q_sw[:, sl] * mqgs_ref[...] * sinb)).astype(BF16)
        kb = k[:, sl]
        rk = lax.rsqrt((jnp.sum(kb * kb, axis=-1, keepdims=True) + kpe_ss) / MLA_QK + EPS)
        kn = rk * (kb * mkg_ref[...] + kpe_rot)
        km_ref[:, sl] = jnp.where(lane == MLA_QK, 1.0, kn).astype(BF16)


def _prep(x2d, seq, cos_t, sin_t, p):
    t = x2d.shape[0]
    tm = PREP_TM
    nseq = seq // tm
    row = lambda i: (i, 0)
    full = lambda i: (0, 0)
    wspec = lambda a: pl.BlockSpec(a.shape, full)
    weights = (p["attn_norm"], p["w_qkv"], p["w_cq"], p["w_ckv"], p["w_kpe"], p["w_uq"], p["w_uq_sw"], p["w_uk"],
               p["w_uv"], p["dq_gain"], p["dk_gain"], p["cq_gain"], p["ckv_gain"], p["mq_gain"], p["mq_gain_sw"],
               p["mk_gain"])
    out_w = (DIFF_W, DIFF_W, DIFF_W, MLA_W, MLA_W, MLA_HEADS * MLA_V)
    return pl.pallas_call(
        _prep_kernel,
        grid=(t // tm,),
        in_specs=[pl.BlockSpec((tm, D_MODEL), row),
                  pl.BlockSpec((tm, LANES), lambda i: (i % nseq, 0)),
                  pl.BlockSpec((tm, LANES), lambda i: (i % nseq, 0))] + [wspec(w) for w in weights],
        out_specs=[pl.BlockSpec((tm, w), row) for w in out_w],
        out_shape=[jax.ShapeDtypeStruct((t, w), BF16) for w in out_w],
        compiler_params=_cparams(("parallel",)),
        name="prep",
    )(x2d, cos_t, sin_t, *weights)


def _bias_kernel(rb_ref, o_ref):
    hd = pl.program_id(0)
    d = pl.program_id(1)
    tq, tk = o_ref.shape
    row = lax.broadcasted_iota(I32, (tq, tk), 0)
    col = lax.broadcasted_iota(I32, (tq, tk), 1)
    rel = col - row + (d - 1) * tk
    half = N_BUCKETS // 2
    max_exact = half // 2
    n = jnp.abs(rel)
    nf = jnp.maximum(n, 1).astype(F32)
    large = max_exact + (jnp.log(nf / max_exact) / math.log(MAX_DISTANCE / max_exact)
                         * (half - max_exact)).astype(I32)
    large = jnp.minimum(large, half - 1)
    bucket = jnp.where(rel > 0, half, 0) + jnp.where(n < max_exact, n, large)
    acc = jnp.zeros((tq, tk), F32)
    for b in range(N_BUCKETS):
        acc = jnp.where(bucket == b, rb_ref[hd, b], acc)
    o_ref[...] = acc * LOG2E


def _bias_tiles(rel_bias_t, t):
    return pl.pallas_call(
        _bias_kernel,
        grid=(DIFF_HEADS, 3),
        in_specs=[pl.BlockSpec(memory_space=pltpu.SMEM)],
        out_specs=pl.BlockSpec((None, None, t, t), lambda h, d: (h, d, 0, 0)),
        out_shape=jax.ShapeDtypeStruct((DIFF_HEADS, 3, t, t), F32),
        compiler_params=_cparams(("parallel", "parallel")),
        name="bias_tiles",
    )(rel_bias_t)


def _online_update(mi, s, v, m_scr, l_scr, acc_scr):
    blocks = [s[:, i:i + LANES] for i in range(0, s.shape[1], LANES)]
    m_old = m_scr[mi]
    row_max = jnp.max(functools.reduce(jnp.maximum, blocks), axis=-1, keepdims=True)
    m_new = jnp.maximum(m_old, row_max)
    alpha = jnp.exp2(m_old - m_new)
    ps = [jnp.exp2(blk - m_new) for blk in blocks]
    l_scr[mi] = alpha * l_scr[mi] + functools.reduce(jnp.add, ps)
    p = jnp.concatenate([x.astype(BF16) for x in ps], axis=1)
    acc_scr[mi] = alpha * acc_scr[mi] + jnp.dot(p, v, preferred_element_type=F32)
    m_scr[mi] = m_new


def _fixed_ref_update(mi, s, v, l_scr, acc_scr):
    ps = [jnp.exp2(s[:, i:i + LANES]) for i in range(0, s.shape[1], LANES)]
    l_scr[mi] += functools.reduce(jnp.add, ps)
    p = jnp.concatenate([x.astype(BF16) for x in ps], axis=1)
    acc_scr[mi] += jnp.dot(p, v, preferred_element_type=F32)


def _normalised(mi, l_scr, acc_scr):
    return acc_scr[mi] / jnp.sum(l_scr[mi], axis=-1, keepdims=True)


def _diff_attn_kernel(rb_ref, bound_ref, q_ref, k_ref, v_ref, bias_ref, subln_ref, lq1_ref, lk1_ref, lq2_ref,
                      lk2_ref, o_ref, m_scr, l_scr, acc_scr, *, lam_init, tk):
    hd = pl.program_id(1)
    qi = pl.program_id(2)
    t = q_ref.shape[0]
    n = v_ref.shape[0] // t
    r = tk // t
    bound = bound_ref[0]
    fixed_ref = bound_ref[1] > 0.5

    q = q_ref[...]
    lane = lax.broadcasted_iota(I32, q.shape, 1)
    zero = jnp.zeros_like(q)
    qs = (jnp.where(lane < DIFF_DK, q, zero), jnp.where(lane < DIFF_DK, zero, q))

    l_scr[...] = jnp.zeros(l_scr.shape, F32)
    acc_scr[...] = jnp.zeros(acc_scr.shape, F32)

    lo = jnp.maximum(qi - 1, 0)
    hi = jnp.minimum(qi + 2, n)
    lo_long = lo // r
    hi_long = (hi + r - 1) // r
    c_left = rb_ref[hd, N_BUCKETS // 2 - 1] * LOG2E
    c_right = rb_ref[hd, N_BUCKETS - 1] * LOG2E

    def sweep(update, after_left, after_near):
        def make_step(size, near):
            def step(j, carry):
                off = pl.multiple_of(j * size, size)
                k = k_ref[:, pl.ds(off, size)]
                v = v_ref[pl.ds(off, size), :]
                for mi in range(2):
                    s = jnp.dot(qs[mi], k, preferred_element_type=F32)
                    if near:
                        s = s + bias_ref[j - qi + 1]
                    update(mi, s, v)
                return carry
            return step

        lax.fori_loop(0, lo_long, make_step(tk, False), 0)
        lax.fori_loop(lo_long * r, lo, make_step(t, False), 0)
        after_left()
        lax.fori_loop(lo, hi, make_step(t, True), 0)
        after_near()
        lax.fori_loop(hi, hi_long * r, make_step(t, False), 0)
        lax.fori_loop(hi_long, n // r, make_step(tk, False), 0)

    @pl.when(fixed_ref)
    def _():
        def rescale(c):
            factor = jnp.exp2(jnp.full((1, LANES), c, F32))
            l_scr[...] = l_scr[...] * factor
            acc_scr[...] = acc_scr[...] * factor

        sweep(lambda mi, s, v: _fixed_ref_update(mi, s - bound, v, l_scr, acc_scr),
              lambda: rescale(c_left), lambda: rescale(-c_right))

    @pl.when(jnp.logical_not(fixed_ref))
    def _():
        m_scr[...] = jnp.full(m_scr.shape, NEG_BIG, F32)

        def shift(c):
            m_scr[...] = m_scr[...] + c

        sweep(lambda mi, s, v: _online_update(mi, s, v, m_scr, l_scr, acc_scr),
              lambda: shift(c_left), lambda: shift(-c_right))

    lam = (jnp.exp(jnp.sum(lq1_ref[...] * lk1_ref[...], axis=-1, keepdims=True))
           - jnp.exp(jnp.sum(lq2_ref[...] * lk2_ref[...], axis=-1, keepdims=True)) + lam_init)
    o = _normalised(0, l_scr, acc_scr) - lam * _normalised(1, l_scr, acc_scr)
    o_ref[...] = (_rms_rows(o, subln_ref[...]) * (1.0 - lam_init)).astype(BF16)


def _diff_attention(qd, kd, vd, bias_tiles, rel_bias_t, p, lam_init):
    b, s, _ = qd.shape
    t = ATT_T
    vec = lambda a: pl.BlockSpec(a.shape, lambda bi, h, i: (0, 0))
    return pl.pallas_call(
        functools.partial(_diff_attn_kernel, lam_init=lam_init, tk=DIFF_TK),
        grid=(b, DIFF_HEADS, s // t),
        in_specs=[pl.BlockSpec(memory_space=pltpu.SMEM), pl.BlockSpec(memory_space=pltpu.SMEM),
                  pl.BlockSpec((None, t, LANES), lambda bi, h, i: (bi, i, h)),
                  pl.BlockSpec((None, LANES, s), lambda bi, h, i: (bi, h, 0)),
                  pl.BlockSpec((None, s, LANES), lambda bi, h, i: (bi, 0, h)),
                  pl.BlockSpec((None, 3, t, t), lambda bi, h, i: (h, 0, 0, 0)),
                  vec(p["subln"]), vec(p["lq1"]), vec(p["lk1"]), vec(p["lq2"]), vec(p["lk2"])],
        out_specs=pl.BlockSpec((None, t, LANES), lambda bi, h, i: (bi, i, h)),
        out_shape=jax.ShapeDtypeStruct((b, s, DIFF_W), BF16),
        scratch_shapes=[pltpu.VMEM((2, t, LANES), F32)] * 3,
        compiler_params=_cparams(("parallel", "parallel", "arbitrary")),
        name="diff_attn",
    )(rel_bias_t, p["diff_bound"], qd, kd, vd, bias_tiles, p["subln"], p["lq1"], p["lk1"], p["lq2"], p["lk2"])


def _mla_attn_kernel(bound_ref, q_ref, k_ref, v_ref, o_ref, m_scr, l_scr, acc_scr, *, tk):
    t = q_ref.shape[0]
    n = v_ref.shape[0] // tk
    bound = bound_ref[0]
    fixed_ref = bound_ref[1] > 0.5

    l_scr[...] = jnp.zeros(l_scr.shape, F32)
    acc_scr[...] = jnp.zeros(acc_scr.shape, F32)

    def sweep(q, update):
        qs = (q[:, :LANES], q[:, LANES:])

        def step(j, carry):
            off = pl.multiple_of(j * tk, tk)
            v = v_ref[pl.ds(off, tk), :]
            for mi in range(2):
                k = k_ref[mi * LANES:(mi + 1) * LANES, pl.ds(off, tk)]
                update(mi, jnp.dot(qs[mi], k, preferred_element_type=F32), v)
            return carry

        lax.fori_loop(0, n, step, 0)

    @pl.when(fixed_ref)
    def _():
        q = q_ref[...]
        lane = lax.broadcasted_iota(I32, q.shape, 1)
        q = jnp.where(lane % LANES == MLA_QK, jnp.full(q.shape, -bound, F32).astype(BF16), q)
        sweep(q, lambda mi, s, v: _fixed_ref_update(mi, s, v, l_scr, acc_scr))

    @pl.when(jnp.logical_not(fixed_ref))
    def _():
        m_scr[...] = jnp.full(m_scr.shape, NEG_BIG, F32)
        sweep(q_ref[...], lambda mi, s, v: _online_update(mi, s, v, m_scr, l_scr, acc_scr))

    lane = lax.broadcasted_iota(I32, (t, LANES), 1)
    o = jnp.where(lane < MLA_V, _normalised(0, l_scr, acc_scr), _normalised(1, l_scr, acc_scr))
    o_ref[...] = o.astype(BF16)


def _mla_attention(qm, km, vm, bound):
    b, s, _ = qm.shape
    t = MLA_TQ
    return pl.pallas_call(
        functools.partial(_mla_attn_kernel, tk=MLA_TK),
        grid=(b, MLA_HEADS // 2, s // t),
        in_specs=[pl.BlockSpec(memory_space=pltpu.SMEM),
                  pl.BlockSpec((None, t, 2 * LANES), lambda bi, h, i: (bi, i, h)),
                  pl.BlockSpec((None, 2 * LANES, s), lambda bi, h, i: (bi, h, 0)),
                  pl.BlockSpec((None, s, LANES), lambda bi, h, i: (bi, 0, h))],
        out_specs=pl.BlockSpec((None, t, LANES), lambda bi, h, i: (bi, i, h)),
        out_shape=jax.ShapeDtypeStruct((b, s, MLA_HEADS * MLA_V), BF16),
        scratch_shapes=[pltpu.VMEM((2, t, LANES), F32)] * 3,
        compiler_params=_cparams(("parallel", "parallel", "arbitrary")),
        name="mla_attn",
    )(bound, qm, km, vm)


def _split_bf16(x):
    hi = x.astype(BF16)
    lo = (x - hi.astype(F32)).astype(BF16)
    return hi, lo


def _outproj_kernel(x_ref, od_ref, om_ref, wo1_ref, wo2_ref, fn_ref, wrh_ref, wrl_ref,
                    x1_ref, h2_ref, aff_ref):
    x1 = (x_ref[...]
          + jnp.dot(od_ref[...], wo1_ref[...], preferred_element_type=F32)
          + jnp.dot(om_ref[...], wo2_ref[...], preferred_element_type=F32))
    x1_ref[...] = x1
    h = _rms_rows(x1, fn_ref[...])
    hh, hl = _split_bf16(h)
    h2_ref[...] = hh
    logits = (jnp.dot(hh, wrh_ref[...], preferred_element_type=F32)
              + jnp.dot(hl, wrh_ref[...], preferred_element_type=F32)
              + jnp.dot(hh, wrl_ref[...], preferred_element_type=F32))
    lane = lax.broadcasted_iota(I32, logits.shape, 1)
    logits = jnp.where(lane < N_EXPERTS, logits, NEG_BIG)
    e = jnp.exp(logits - jnp.max(logits, axis=-1, keepdims=True))
    aff = e / jnp.sum(e, axis=-1, keepdims=True)
    aff_ref[...] = aff[:, :N_EXPERTS]


def _outproj(x2d, od, om, p):
    t = x2d.shape[0]
    tm = OUT_TM
    row = lambda i: (i, 0)
    wspec = lambda a: pl.BlockSpec(a.shape, lambda i: (0, 0))
    weights = (p["w_o1"], p["w_o2"], p["ffn_norm"], p["w_r_hi"], p["w_r_lo"])
    return pl.pallas_call(
        _outproj_kernel,
        grid=(t // tm,),
        in_specs=[pl.BlockSpec((tm, D_MODEL), row), pl.BlockSpec((tm, DIFF_W), row),
                  pl.BlockSpec((tm, MLA_HEADS * MLA_V), row)] + [wspec(w) for w in weights],
        out_specs=[pl.BlockSpec((tm, D_MODEL), row), pl.BlockSpec((tm, D_MODEL), row),
                   pl.BlockSpec((tm, N_EXPERTS), row)],
        out_shape=[jax.ShapeDtypeStruct((t, D_MODEL), F32), jax.ShapeDtypeStruct((t, D_MODEL), BF16),
                   jax.ShapeDtypeStruct((t, N_EXPERTS), F32)],
        compiler_params=_cparams(("parallel",)),
        name="outproj",
    )(x2d, od, om, *weights)


def _route_kernel(aff_ref, pos_ref, gate_ref, csx_ref, *, cap):
    rows = aff_ref.shape[0]
    groups = LANES // N_EXPERTS
    aff = aff_ref[...]
    bits = pltpu.bitcast(aff, I32)

    def expert_total(v):
        for sh in (N_EXPERTS, 2 * N_EXPERTS, 4 * N_EXPERTS):
            v = v + pltpu.roll(v, sh, 1)
        return v

    def count(mask):
        return expert_total(jnp.sum(mask.astype(I32), axis=0, keepdims=True))

    def search(i, thr):
        cand = thr | jnp.left_shift(jnp.int32(1), 30 - i)
        return jnp.where(count(bits >= cand) >= cap, cand, thr)

    thr = lax.fori_loop(0, 31, search, jnp.zeros((1, LANES), I32))

    a = lax.broadcasted_iota(I32, (LANES, LANES), 0)
    b = lax.broadcasted_iota(I32, (LANES, LANES), 1)
    same = (a % N_EXPERTS) == (b % N_EXPERTS)
    q_all = same.astype(BF16)
    q_before = (same & (a // N_EXPERTS < b // N_EXPERTS)).astype(BF16)
    rb = ROUTE_RB
    ra = lax.broadcasted_iota(I32, (rb, rb), 0)
    ca = lax.broadcasted_iota(I32, (rb, rb), 1)
    tri = (ca < ra).astype(BF16)

    def prefix(mask):
        mb = mask.astype(BF16)
        outs = []
        offset = jnp.zeros((1, LANES), F32)
        for r0 in range(0, rows, rb):
            blk = mb[r0:r0 + rb]
            tot = jnp.dot(blk, q_all, preferred_element_type=F32)
            within = jnp.dot(blk, q_before, preferred_element_type=F32)
            above = jnp.dot(tri, tot.astype(BF16), preferred_element_type=F32)
            outs.append(above + within + offset)
            offset = offset + jnp.sum(tot, axis=0, keepdims=True)
        return jnp.concatenate(outs, axis=0)

    gt = bits > thr
    eq = bits == thr
    need = (cap - count(gt)).astype(F32)
    sel = gt | (eq & (prefix(eq) < need))
    csx = prefix(sel)
    pos_ref[...] = jnp.where(sel, csx.astype(I32), -1)
    gate_ref[...] = jnp.where(sel, aff, 0.0)
    csx_ref[...] = csx.astype(I32)


def _route(aff_packed, cap):
    rows = aff_packed.shape[0]
    vm = pl.BlockSpec(memory_space=pltpu.VMEM)
    return pl.pallas_call(
        functools.partial(_route_kernel, cap=cap),
        in_specs=[vm],
        out_specs=[vm, vm, vm],
        out_shape=[jax.ShapeDtypeStruct((rows, LANES), I32), jax.ShapeDtypeStruct((rows, LANES), F32),
                   jax.ShapeDtypeStruct((rows, LANES), I32)],
        compiler_params=pltpu.CompilerParams(vmem_limit_bytes=VMEM_LIMIT),
        name="route",
    )(aff_packed)


def _window_start(first_slot, align, w, cap):
    return jnp.minimum((first_slot // align) * align, cap - w)


def _gather_kernel(p0_ref, pn_ref, pos_ref, h2_ref, xg_ref, *, cap, w):
    ne = xg_ref.shape[0]
    e0 = pl.program_id(0) * ne
    i = pl.program_id(1)
    tg = h2_ref.shape[0]

    @pl.when(i == 0)
    def _():
        xg_ref[...] = jnp.zeros(xg_ref.shape, BF16)

    for j in range(ne):
        w0 = _window_start(p0_ref[e0 + j, i], BF16_ROWS, w, cap)
        n_win = (pn_ref[e0 + j, i] - w0 + w - 1) // w
        prow = pos_ref[j, pl.ds(i, 1), :]

        def window(k, carry, j=j, w0=w0, prow=prow):
            nominal = w0 + k * w
            start = pl.multiple_of(jnp.minimum(nominal, cap - w), BF16_ROWS)
            slot = lax.broadcasted_iota(I32, (w, tg), 0) + start
            onehot = ((prow == slot) & (prow >= nominal)).astype(BF16)
            rows = jnp.dot(onehot, h2_ref[...], preferred_element_type=F32).astype(BF16)
            xg_ref[j, pl.ds(start, w), :] += rows
            return carry

        lax.fori_loop(0, n_win, window, 0)


def _gather(p0, pn, pos_t, h2, cap):
    t = h2.shape[0]
    tg, ne = GATHER_TG, GATHER_NE
    return pl.pallas_call(
        functools.partial(_gather_kernel, cap=cap, w=GATHER_W),
        grid_spec=pltpu.PrefetchScalarGridSpec(
            num_scalar_prefetch=2,
            grid=(N_EXPERTS // ne, t // tg),
            in_specs=[pl.BlockSpec((ne, t // tg, tg), lambda e, i, *_: (e, 0, 0)),
                      pl.BlockSpec((tg, D_MODEL), lambda e, i, *_: (i, 0))],
            out_specs=pl.BlockSpec((ne, cap, D_MODEL), lambda e, i, *_: (e, 0, 0))),
        out_shape=jax.ShapeDtypeStruct((N_EXPERTS, cap, D_MODEL), BF16),
        compiler_params=_cparams(("parallel", "arbitrary")),
        name="gather",
    )(p0, pn, pos_t, h2)


def _ffn_kernel(xg_ref, wg_ref, wu_ref, wd_ref, y_ref, acc_scr):
    f = pl.program_id(1)

    @pl.when((pl.program_id(0) == 0) & (f == 0))
    def _():
        acc_scr[...] = jnp.zeros(acc_scr.shape, F32)

    xg = xg_ref[...]
    a = jnp.dot(xg, wg_ref[...].astype(BF16), preferred_element_type=F32)
    b = jnp.dot(xg, wu_ref[...].astype(BF16), preferred_element_type=F32)
    act = (a * jax.nn.sigmoid(a) * b).astype(BF16)
    part = jnp.dot(act, wd_ref[...].astype(BF16), preferred_element_type=F32)
    acc_scr[...] = jnp.where(f == 0, part, acc_scr[...] + part)

    @pl.when(f == pl.num_programs(1) - 1)
    def _():
        y_ref[...] = acc_scr[...].astype(BF16)


def _ffn(xg, w_gate, w_up, w_down):
    _, cap, _ = xg.shape
    tf = FFN_TF
    return pl.pallas_call(
        _ffn_kernel,
        grid=(N_EXPERTS, D_FF // tf),
        in_specs=[pl.BlockSpec((None, cap, D_MODEL), lambda e, f: (e, 0, 0)),
                  pl.BlockSpec((None, D_MODEL, tf), lambda e, f: (e, 0, f)),
                  pl.BlockSpec((None, D_MODEL, tf), lambda e, f: (e, 0, f)),
                  pl.BlockSpec((None, tf, D_MODEL), lambda e, f: (e, f, 0))],
        out_specs=pl.BlockSpec((None, cap, D_MODEL), lambda e, f: (e, 0, 0)),
        out_shape=jax.ShapeDtypeStruct((N_EXPERTS, cap, D_MODEL), BF16),
        scratch_shapes=[pltpu.VMEM((cap, D_MODEL), F32)],
        compiler_params=_cparams(("arbitrary", "arbitrary")),
        name="ffn",
    )(xg, w_gate, w_up, w_down)


def _combine_kernel(p0_ref, pn_ref, x1_ref, pos_ref, gate_ref, y_hbm, o_ref, win, xwin, sem, xsem, *, cap):
    i = pl.program_id(0)
    tt = x1_ref.shape[0]
    w = xwin.shape[0]

    def first_window(e):
        return _window_start(p0_ref[i, e], BF16_ROWS, w, cap)

    def window_copy(e):
        return pltpu.make_async_copy(y_hbm.at[e, pl.ds(first_window(e), w), :],
                                     win.at[pl.ds(e * w, w), :], sem.at[e])

    for e in range(N_EXPERTS):
        window_copy(e).start()

    lane = lax.broadcasted_iota(I32, (tt, w), 1)

    def weights(e, first_slot, lo_slot):
        pcol = pos_ref[:, e:e + 1]
        hit = (pcol - first_slot == lane) & (pcol >= lo_slot)
        return jnp.where(hit, gate_ref[:, e:e + 1], 0.0).astype(BF16)

    c = jnp.concatenate([weights(e, first_window(e), 0) for e in range(N_EXPERTS)], axis=1)
    for e in range(N_EXPERTS):
        window_copy(e).wait()
    o_ref[...] = x1_ref[...] + jnp.dot(c, win[...], preferred_element_type=F32)

    for e in range(N_EXPERTS):
        w0 = first_window(e)
        n_win = (pn_ref[i, e] - w0 + w - 1) // w

        def extra(k, carry, e=e, w0=w0):
            nominal = w0 + k * w
            start = jnp.minimum(nominal, cap - w)
            cp = pltpu.make_async_copy(y_hbm.at[e, pl.ds(start, w), :], xwin, xsem)
            cp.start()
            cp.wait()
            o_ref[...] += jnp.dot(weights(e, start, nominal), xwin[...], preferred_element_type=F32)
            return carry

        lax.fori_loop(1, n_win, extra, 0)


def _combine(p0, pn, x1, pos_tok, gate_tok, y, cap):
    t = x1.shape[0]
    tt, w = COMB_TT, COMB_W
    row = lambda i, *_: (i, 0)
    return pl.pallas_call(
        functools.partial(_combine_kernel, cap=cap),
        grid_spec=pltpu.PrefetchScalarGridSpec(
            num_scalar_prefetch=2,
            grid=(t // tt,),
            in_specs=[pl.BlockSpec((tt, D_MODEL), row), pl.BlockSpec((tt, N_EXPERTS), row),
                      pl.BlockSpec((tt, N_EXPERTS), row), pl.BlockSpec(memory_space=pl.ANY)],
            out_specs=pl.BlockSpec((tt, D_MODEL), row),
            scratch_shapes=[pltpu.VMEM((N_EXPERTS * w, D_MODEL), BF16), pltpu.VMEM((w, D_MODEL), BF16),
                            pltpu.SemaphoreType.DMA((N_EXPERTS,)), pltpu.SemaphoreType.DMA(())]),
        out_shape=jax.ShapeDtypeStruct((t, D_MODEL), F32),
        compiler_params=_cparams(("arbitrary",)),
        name="combine",
    )(p0, pn, x1, pos_tok, gate_tok, y)


def _score_bound(width, q_gain, k_gain, max_bias):
    gmax = lambda g: jnp.max(jnp.abs(g.astype(F32)))
    bound = BOUND_SLACK * (width ** 0.5 * LOG2E * gmax(q_gain) * gmax(k_gain) + LOG2E * max_bias)
    return jnp.stack([bound, (3.0 * bound < F32_EXP_RANGE).astype(F32)]).astype(F32)


def _layer_params(l, rel_bias, attn_norm, w_in, diff_q_norm, diff_k_norm, lambda_q1, lambda_k1, lambda_q2,
                  lambda_k2, diff_subln, mla_q_latent_norm, mla_kv_latent_norm, w_uq, w_ukv, mla_q_norm,
                  mla_k_norm, w_out, ffn_norm, w_router):
    row = lambda v: v.reshape(1, -1).astype(F32)
    w = w_in[l]
    o = 3 * DIFF_W
    pad_rope = MLA_NOPE, LANES - MLA_QK
    w_kpe = jnp.pad(w[:, o + Q_LORA + KV_LORA:], ((0, 0), pad_rope))
    uq = jnp.pad(w_uq[l].reshape(Q_LORA, MLA_HEADS, MLA_QK), ((0, 0), (0, 0), (0, LANES - MLA_QK)))
    ukv = w_ukv[l].reshape(KV_LORA, MLA_HEADS, MLA_NOPE + MLA_V)
    uk = jnp.pad(ukv[:, :, :MLA_NOPE], ((0, 0), (0, 0), (0, LANES - MLA_NOPE)))
    head_gain = lambda g: jnp.pad(g.astype(F32), (0, LANES - MLA_QK)).reshape(1, LANES)
    half = MLA_ROPE // 2

    def swap_rope_halves(a):
        z = jnp.zeros_like(a)
        x1, x2 = a[..., MLA_NOPE:MLA_NOPE + half], a[..., MLA_NOPE + half:MLA_QK]
        return jnp.concatenate([z[..., :MLA_NOPE], x2, x1, z[..., MLA_QK:]], axis=-1)

    wr = jnp.pad(w_router[l].astype(F32), ((0, 0), (0, LANES - N_EXPERTS)))
    wr_hi = wr.astype(BF16)
    return dict(
        attn_norm=row(attn_norm[l]),
        w_qkv=w[:, :o].astype(BF16), w_cq=w[:, o:o + Q_LORA].astype(BF16),
        w_ckv=w[:, o + Q_LORA:o + Q_LORA + KV_LORA].astype(BF16), w_kpe=w_kpe.astype(BF16),
        w_uq=uq.reshape(Q_LORA, MLA_W).astype(BF16), w_uk=uk.reshape(KV_LORA, MLA_W).astype(BF16),
        w_uq_sw=swap_rope_halves(uq).reshape(Q_LORA, MLA_W).astype(BF16),
        mq_gain_sw=swap_rope_halves(head_gain(mla_q_norm[l])),
        w_uv=ukv[:, :, MLA_NOPE:].reshape(KV_LORA, MLA_HEADS * MLA_V).astype(BF16),
        dq_gain=row(jnp.tile(diff_q_norm[l], 2)), dk_gain=row(jnp.tile(diff_k_norm[l], 2)),
        cq_gain=row(mla_q_latent_norm[l]), ckv_gain=row(mla_kv_latent_norm[l]),
        mq_gain=head_gain(mla_q_norm[l]), mk_gain=head_gain(mla_k_norm[l]),
        subln=row(diff_subln[l]), lq1=row(lambda_q1[l]), lk1=row(lambda_k1[l]),
        lq2=row(lambda_q2[l]), lk2=row(lambda_k2[l]),
        rel_bias_t=rel_bias.astype(F32).T,
        diff_bound=_score_bound(DIFF_DK, diff_q_norm[l], diff_k_norm[l], jnp.max(jnp.abs(rel_bias))),
        mla_bound=_score_bound(MLA_QK, mla_q_norm[l], mla_k_norm[l], 0.0),
        w_o1=w_out[l][:DIFF_W].astype(BF16), w_o2=w_out[l][DIFF_W:].astype(BF16),
        ffn_norm=row(ffn_norm[l]), w_r_hi=wr_hi, w_r_lo=(wr - wr_hi.astype(F32)).astype(BF16),
    )


def _rope_tables(seq):
    half = MLA_ROPE // 2
    inv = 1.0 / (ROPE_BASE ** (jnp.arange(half, dtype=F32) / half))
    ang = jnp.arange(seq, dtype=jnp.int32).astype(F32)[:, None] * inv[None, :]
    cos, sin = jnp.cos(ang), jnp.sin(ang)
    ones = jnp.ones((seq, MLA_NOPE), F32)
    tail = LANES - MLA_QK
    cos_t = jnp.concatenate([ones, cos, cos, jnp.ones((seq, tail), F32)], axis=1)
    sin_t = jnp.concatenate([0 * ones, -sin, sin, jnp.zeros((seq, tail), F32)], axis=1)
    return cos_t, sin_t


def _layer(x, p, bias, w_gate, w_up, w_down, lam_init):
    b, s, d = x.shape
    t = b * s
    cap = CAPACITY_FACTOR * t // N_EXPERTS
    x2d = x.reshape(t, d)

    cos_t, sin_t = _rope_tables(s)
    qd, kd, vd, qm, km, vm = _prep(x2d, s, cos_t, sin_t, p)
    shp = lambda a: a.reshape(b, s, a.shape[-1])
    keys_t = lambda a: jnp.swapaxes(shp(a), 1, 2)
    od = _diff_attention(shp(qd), keys_t(kd), shp(vd), bias, p["rel_bias_t"], p, lam_init)
    om = _mla_attention(shp(qm), keys_t(km), shp(vm), p["mla_bound"])
    x1, h2, aff = _outproj(x2d, od.reshape(t, -1), om.reshape(t, -1), p)

    groups = LANES // N_EXPERTS
    pos_p, gate_p, csx_p = _route(aff.reshape(t // groups, LANES), cap)
    pos_tok = pos_p.reshape(t, N_EXPERTS)
    gate_tok = gate_p.reshape(t, N_EXPERTS)
    csx_tok = csx_p.reshape(t, N_EXPERTS)

    def tile_slots(tile):
        first = csx_tok[::tile]
        return first, jnp.concatenate([first[1:], jnp.full((1, N_EXPERTS), cap, I32)], axis=0)

    g0, gn = tile_slots(GATHER_TG)
    pos_t = pos_tok.T.reshape(N_EXPERTS, t // GATHER_TG, GATHER_TG)
    xg = _gather(g0.T, gn.T, pos_t, h2, cap)
    y = _ffn(xg, w_gate, w_up, w_down)
    c0, cn = tile_slots(COMB_TT)
    out = _combine(c0, cn, x1, pos_tok, gate_tok, y, cap)
    return out.reshape(b, s, d)


def kernel(x_prompt, x_sample, rel_bias, attn_norm, w_in, diff_q_norm, diff_k_norm, lambda_q1, lambda_k1, lambda_q2, lambda_k2, diff_subln, mla_q_latent_norm, mla_kv_latent_norm, w_uq, w_ukv, mla_q_norm, mla_k_norm, w_out, ffn_norm, w_router, w_gate, w_up, w_down):
    params = [_layer_params(l, rel_bias, attn_norm, w_in, diff_q_norm, diff_k_norm, lambda_q1, lambda_k1,
                            lambda_q2, lambda_k2, diff_subln, mla_q_latent_norm, mla_kv_latent_norm, w_uq,
                            w_ukv, mla_q_norm, mla_k_norm, w_out, ffn_norm, w_router) for l in range(DEPTH)]
    biases = [_bias_tiles(p["rel_bias_t"], ATT_T) for p in params]
    outs = []
    for x in (x_prompt, x_sample):
        for l in range(DEPTH):
            lam_init = 0.8 - 0.6 * math.exp(-0.3 * l)
            x = _layer(x, params[l], biases[l], w_gate[l], w_up[l], w_down[l], lam_init)
        outs.append(x)
    return tuple(outs)
```

```python
import functools
import math

import jax
import jax.numpy as jnp
from jax import lax
from jax.experimental import pallas as pl
from jax.experimental.pallas import tpu as pltpu

F32 = jnp.float32
BF16 = jnp.bfloat16
I32 = jnp.int32

D_MODEL = 1024
DEPTH = 1
DIFF_HEADS = 4
DIFF_DK = 64
DIFF_DV = 128
MLA_HEADS = 8
MLA_NOPE = 64
MLA_ROPE = 32
MLA_QK = MLA_NOPE + MLA_ROPE
MLA_V = 64
Q_LORA = 256
KV_LORA = 128
ROPE_BASE = 10000.0
N_BUCKETS = 32
MAX_DISTANCE = 128
N_EXPERTS = 16
CAPACITY_FACTOR = 2
D_FF = 2816
EPS = 1e-6

LANES = 128
SUBLANES = 8
BF16_ROWS = 16
DIFF_W = DIFF_HEADS * DIFF_DV
MLA_W = MLA_HEADS * LANES
NEG_BIG = -1e30
LOG2E = math.log2(math.e)
VMEM_LIMIT = 56 * 1024 * 1024
F32_EXP_RANGE = 120.0
BOUND_SLACK = 1.02

PREP_TM = 512
ATT_T = 512
DIFF_TK = 2048
BIAS_TILES = 5
MLA_TQ = 512
MLA_TK = 2048
OUT_TM = 512
ROUTE_RB = 256
GATHER_TG = 1024
GATHER_W = 192
GATHER_NE = 4
FFN_TF = 256
COMB_TT = 512
COMB_W = 128


def _cparams(sem, vmem=VMEM_LIMIT):
    return pltpu.CompilerParams(dimension_semantics=sem, vmem_limit_bytes=vmem)


def _rms_rows(x, gain):
    ms = jnp.mean(x * x, axis=-1, keepdims=True)
    return x * lax.rsqrt(ms + EPS) * gain


def _prep_kernel(x_ref, cos_ref, sin_ref, an_ref, wqkv_ref, wcq_ref, wckv_ref, wkpe_ref,
                 wuq_ref, wuqs_ref, wuk_ref, wuv_ref, dqg_ref, dkg_ref, cqg_ref, ckvg_ref, mqg_ref, mqgs_ref,
                 mkg_ref, qd_ref, kd_ref, vd_ref, qm_ref, km_ref, vm_ref):
    tm = x_ref.shape[0]
    hb = _rms_rows(x_ref[...], an_ref[...]).astype(BF16)

    lane = lax.broadcasted_iota(I32, (tm, LANES), 1)
    lo = lane < DIFF_DK

    qkv = jnp.dot(hb, wqkv_ref[...], preferred_element_type=F32)
    diff_scale = DIFF_DK ** -0.5 * LOG2E
    for hd in range(DIFF_HEADS):
        for off, g_ref, o_ref, scale in ((0, dqg_ref, qd_ref, diff_scale), (DIFF_W, dkg_ref, kd_ref, None)):
            blk = qkv[:, off + hd * LANES: off + (hd + 1) * LANES]
            sq = blk * blk
            s_lo = jnp.sum(jnp.where(lo, sq, 0.0), axis=-1, keepdims=True)
            s_hi = jnp.sum(jnp.where(lo, 0.0, sq), axis=-1, keepdims=True)
            r = jnp.where(lo, lax.rsqrt(s_lo / DIFF_DK + EPS), lax.rsqrt(s_hi / DIFF_DK + EPS))
            y = blk * r * g_ref[...]
            if scale is not None:
                y = y * scale
            o_ref[:, hd * LANES:(hd + 1) * LANES] = y.astype(BF16)
    vd_ref[...] = qkv[:, 2 * DIFF_W:3 * DIFF_W].astype(BF16)

    cq = _rms_rows(jnp.dot(hb, wcq_ref[...], preferred_element_type=F32), cqg_ref[...]).astype(BF16)
    ckv = _rms_rows(jnp.dot(hb, wckv_ref[...], preferred_element_type=F32), ckvg_ref[...]).astype(BF16)
    kpe = jnp.dot(hb, wkpe_ref[...], preferred_element_type=F32)
    q = jnp.dot(cq, wuq_ref[...], preferred_element_type=F32)
    q_sw = jnp.dot(cq, wuqs_ref[...], preferred_element_type=F32)
    k = jnp.dot(ckv, wuk_ref[...], preferred_element_type=F32)
    vm_ref[...] = jnp.dot(ckv, wuv_ref[...], preferred_element_type=F32).astype(BF16)

    cosb = cos_ref[...]
    sinb = sin_ref[...]
    mla_scale = MLA_QK ** -0.5 * LOG2E
    first_half = (lane >= MLA_NOPE) & (lane < MLA_NOPE + MLA_ROPE // 2)
    kpe_g = kpe * mkg_ref[...]
    kpe_rot = kpe_g * cosb + sinb * jnp.where(first_half,
                                              pltpu.roll(kpe_g, LANES - MLA_ROPE // 2, 1),
                                              pltpu.roll(kpe_g, MLA_ROPE // 2, 1))
    kpe_ss = jnp.sum(kpe * kpe, axis=-1, keepdims=True)

    for hd in range(MLA_HEADS):
        sl = slice(hd * LANES, (hd + 1) * LANES)
        qb = q[:, sl]
        rq = lax.rsqrt(jnp.sum(qb * qb, axis=-1, keepdims=True) / MLA_QK + EPS) * mla_scale
        qm_ref[:, sl] = (rq * (qb * mqg_ref[...] * cosb + q_sw[:, sl] * mqgs_ref[...] * sinb)).astype(BF16)
        kb = k[:, sl]
        rk = lax.rsqrt((jnp.sum(kb * kb, axis=-1, keepdims=True) + kpe_ss) / MLA_QK + EPS)
        kn = rk * (kb * mkg_ref[...] + kpe_rot)
        km_ref[:, sl] = jnp.where(lane == MLA_QK, 1.0, kn).astype(BF16)


def _prep(x2d, seq, cos_t, sin_t, p):
    t = x2d.shape[0]
    tm = PREP_TM
    nseq = seq // tm
    row = lambda i: (i, 0)
    full = lambda i: (0, 0)
    wspec = lambda a: pl.BlockSpec(a.shape, full)
    weights = (p["attn_norm"], p["w_qkv"], p["w_cq"], p["w_ckv"], p["w_kpe"], p["w_uq"], p["w_uq_sw"], p["w_uk"],
               p["w_uv"], p["dq_gain"], p["dk_gain"], p["cq_gain"], p["ckv_gain"], p["mq_gain"], p["mq_gain_sw"],
               p["mk_gain"])
    out_w = (DIFF_W, DIFF_W, DIFF_W, MLA_W, MLA_W, MLA_HEADS * MLA_V)
    return pl.pallas_call(
        _prep_kernel,
        grid=(t // tm,),
        in_specs=[pl.BlockSpec((tm, D_MODEL), row),
                  pl.BlockSpec((tm, LANES), lambda i: (i % nseq, 0)),
                  pl.BlockSpec((tm, LANES), lambda i: (i % nseq, 0))] + [wspec(w) for w in weights],
        out_specs=[pl.BlockSpec((tm, w), row) for w in out_w],
        out_shape=[jax.ShapeDtypeStruct((t, w), BF16) for w in out_w],
        compiler_params=_cparams(("parallel",)),
        name="prep",
    )(x2d, cos_t, sin_t, *weights)


def _bias_kernel(rb_ref, o_ref):
    hd = pl.program_id(0)
    d = pl.program_id(1)
    tq, tk = o_ref.shape
    row = lax.broadcasted_iota(I32, (tq, tk), 0)
    col = lax.broadcasted_iota(I32, (tq, tk), 1)
    rel = col - row + (d - BIAS_TILES // 2) * tk
    half = N_BUCKETS // 2
    max_exact = half // 2
    n = jnp.abs(rel)
    nf = jnp.maximum(n, 1).astype(F32)
    large = max_exact + (jnp.log(nf / max_exact) / math.log(MAX_DISTANCE / max_exact)
                         * (half - max_exact)).astype(I32)
    large = jnp.minimum(large, half - 1)
    bucket = jnp.where(rel > 0, half, 0) + jnp.where(n < max_exact, n, large)
    acc = jnp.zeros((tq, tk), F32)
    for b in range(N_BUCKETS):
        acc = jnp.where(bucket == b, rb_ref[hd, b], acc)
    o_ref[...] = acc * LOG2E


def _bias_tiles(rel_bias_t, t):
    return pl.pallas_call(
        _bias_kernel,
        grid=(DIFF_HEADS, BIAS_TILES),
        in_specs=[pl.BlockSpec(memory_space=pltpu.SMEM)],
        out_specs=pl.BlockSpec((None, None, t, t), lambda h, d: (h, d, 0, 0)),
        out_shape=jax.ShapeDtypeStruct((DIFF_HEADS, BIAS_TILES, t, t), F32),
        compiler_params=_cparams(("parallel", "parallel")),
        name="bias_tiles",
    )(rel_bias_t)


def _online_update(mi, s, v, m_scr, l_scr, acc_scr):
    blocks = [s[:, i:i + LANES] for i in range(0, s.shape[1], LANES)]
    m_old = m_scr[mi]
    row_max = jnp.max(functools.reduce(jnp.maximum, blocks), axis=-1, keepdims=True)
    m_new = jnp.maximum(m_old, row_max)
    alpha = jnp.exp2(m_old - m_new)
    ps = [jnp.exp2(blk - m_new) for blk in blocks]
    l_scr[mi] = alpha * l_scr[mi] + functools.reduce(jnp.add, ps)
    p = jnp.concatenate([x.astype(BF16) for x in ps], axis=1)
    acc_scr[mi] = alpha * acc_scr[mi] + jnp.dot(p, v, preferred_element_type=F32)
    m_scr[mi] = m_new


def _fixed_ref_update(mi, s, v, l_scr, acc_scr):
    ps = [jnp.exp2(s[:, i:i + LANES]) for i in range(0, s.shape[1], LANES)]
    l_scr[mi] += functools.reduce(jnp.add, ps)
    p = jnp.concatenate([x.astype(BF16) for x in ps], axis=1)
    acc_scr[mi] += jnp.dot(p, v, preferred_element_type=F32)


def _normalised(mi, l_scr, acc_scr):
    return acc_scr[mi] / jnp.sum(l_scr[mi], axis=-1, keepdims=True)


def _diff_attn_kernel(rb_ref, bound_ref, q_ref, k_ref, v_ref, bias_ref, subln_ref, lq1_ref, lk1_ref, lq2_ref,
                      lk2_ref, o_ref, m_scr, l_scr, acc_scr, *, lam_init, tk):
    hd = pl.program_id(1)
    qi = pl.program_id(2)
    t = q_ref.shape[0]
    n = v_ref.shape[0] // t
    r = tk // t
    bound = bound_ref[0]
    fixed_ref = bound_ref[1] > 0.5

    q = q_ref[...]
    lane = lax.broadcasted_iota(I32, q.shape, 1)
    zero = jnp.zeros_like(q)
    qs = (jnp.where(lane < DIFF_DK, q, zero), jnp.where(lane < DIFF_DK, zero, q))

    l_scr[...] = jnp.zeros(l_scr.shape, F32)
    acc_scr[...] = jnp.zeros(acc_scr.shape, F32)

    lo = jnp.maximum(qi - 1, 0) // r
    hi = (jnp.minimum(qi + 2, n) + r - 1) // r
    c_left = rb_ref[hd, N_BUCKETS // 2 - 1] * LOG2E
    c_right = rb_ref[hd, N_BUCKETS - 1] * LOG2E
    n_side = bias_ref.shape[0] // 2

    def sweep(update, after_left, after_near):
        def make_step(near):
            def step(j, carry):
                off = pl.multiple_of(j * tk, tk)
                k = k_ref[:, pl.ds(off, tk)]
                v = v_ref[pl.ds(off, tk), :]
                for mi in range(2):
                    s = jnp.dot(qs[mi], k, preferred_element_type=F32)
                    if near:
                        pieces = []
                        for c in range(r):
                            d = jnp.clip(j * r + c - qi, -n_side, n_side)
                            pieces.append(s[:, c * t:(c + 1) * t] + bias_ref[d + n_side])
                        s = jnp.concatenate(pieces, axis=1)
                    update(mi, s, v)
                return carry
            return step

        lax.fori_loop(0, lo, make_step(False), 0)
        after_left()
        lax.fori_loop(lo, hi, make_step(True), 0)
        after_near()
        lax.fori_loop(hi, n // r, make_step(False), 0)

    @pl.when(fixed_ref)
    def _():
        def rescale(c):
            factor = jnp.exp2(jnp.full((1, LANES), c, F32))
            l_scr[...] = l_scr[...] * factor
            acc_scr[...] = acc_scr[...] * factor

        sweep(lambda mi, s, v: _fixed_ref_update(mi, s - bound, v, l_scr, acc_scr),
              lambda: rescale(c_left), lambda: rescale(-c_right))

    @pl.when(jnp.logical_not(fixed_ref))
    def _():
        m_scr[...] = jnp.full(m_scr.shape, NEG_BIG, F32)

        def shift(c):
            m_scr[...] = m_scr[...] + c

        sweep(lambda mi, s, v: _online_update(mi, s, v, m_scr, l_scr, acc_scr),
              lambda: shift(c_left), lambda: shift(-c_right))

    lam = (jnp.exp(jnp.sum(lq1_ref[...] * lk1_ref[...], axis=-1, keepdims=True))
           - jnp.exp(jnp.sum(lq2_ref[...] * lk2_ref[...], axis=-1, keepdims=True)) + lam_init)
    o = _normalised(0, l_scr, acc_scr) - lam * _normalised(1, l_scr, acc_scr)
    o_ref[...] = (_rms_rows(o, subln_ref[...]) * (1.0 - lam_init)).astype(BF16)


def _diff_attention(qd, kd, vd, bias_tiles, rel_bias_t, p, lam_init):
    b, s, _ = qd.shape
    t = ATT_T
    vec = lambda a: pl.BlockSpec(a.shape, lambda bi, h, i: (0, 0))
    return pl.pallas_call(
        functools.partial(_diff_attn_kernel, lam_init=lam_init, tk=DIFF_TK),
        grid=(b, DIFF_HEADS, s // t),
        in_specs=[pl.BlockSpec(memory_space=pltpu.SMEM), pl.BlockSpec(memory_space=pltpu.SMEM),
                  pl.BlockSpec((None, t, LANES), lambda bi, h, i: (bi, i, h)),
                  pl.BlockSpec((None, LANES, s), lambda bi, h, i: (bi, h, 0)),
                  pl.BlockSpec((None, s, LANES), lambda bi, h, i: (bi, 0, h)),
                  pl.BlockSpec((None, BIAS_TILES, t, t), lambda bi, h, i: (h, 0, 0, 0)),
                  vec(p["subln"]), vec(p["lq1"]), vec(p["lk1"]), vec(p["lq2"]), vec(p["lk2"])],
        out_specs=pl.BlockSpec((None, t, LANES), lambda bi, h, i: (bi, i, h)),
        out_shape=jax.ShapeDtypeStruct((b, s, DIFF_W), BF16),
        scratch_shapes=[pltpu.VMEM((2, t, LANES), F32)] * 3,
        compiler_params=_cparams(("parallel", "parallel", "arbitrary")),
        name="diff_attn",
    )(rel_bias_t, p["diff_bound"], qd, kd, vd, bias_tiles, p["subln"], p["lq1"], p["lk1"], p["lq2"], p["lk2"])


def _mla_attn_kernel(bound_ref, q_ref, k_ref, v_ref, o_ref, m_scr, l_scr, acc_scr, *, tk):
    t = q_ref.shape[0]
    n = v_ref.shape[0] // tk
    bound = bound_ref[0]
    fixed_ref = bound_ref[1] > 0.5

    l_scr[...] = jnp.zeros(l_scr.shape, F32)
    acc_scr[...] = jnp.zeros(acc_scr.shape, F32)

    def sweep(q, update):
        qs = (q[:, :LANES], q[:, LANES:])

        def step(j, carry):
            off = pl.multiple_of(j * tk, tk)
            v = v_ref[pl.ds(off, tk), :]
            for mi in range(2):
                k = k_ref[mi * LANES:(mi + 1) * LANES, pl.ds(off, tk)]
                update(mi, jnp.dot(qs[mi], k, preferred_element_type=F32), v)
            return carry

        lax.fori_loop(0, n, step, 0)

    @pl.when(fixed_ref)
    def _():
        q = q_ref[...]
        lane = lax.broadcasted_iota(I32, q.shape, 1)
        q = jnp.where(lane % LANES == MLA_QK, jnp.full(q.shape, -bound, F32).astype(BF16), q)
        sweep(q, lambda mi, s, v: _fixed_ref_update(mi, s, v, l_scr, acc_scr))

    @pl.when(jnp.logical_not(fixed_ref))
    def _():
        m_scr[...] = jnp.full(m_scr.shape, NEG_BIG, F32)
        sweep(q_ref[...], lambda mi, s, v: _online_update(mi, s, v, m_scr, l_scr, acc_scr))

    lane = lax.broadcasted_iota(I32, (t, LANES), 1)
    o = jnp.where(lane < MLA_V, _normalised(0, l_scr, acc_scr), _normalised(1, l_scr, acc_scr))
    o_ref[...] = o.astype(BF16)


def _mla_attention(qm, km, vm, bound):
    b, s, _ = qm.shape
    t = MLA_TQ
    return pl.pallas_call(
        functools.partial(_mla_attn_kernel, tk=MLA_TK),
        grid=(b, MLA_HEADS // 2, s // t),
        in_specs=[pl.BlockSpec(memory_space=pltpu.SMEM),
                  pl.BlockSpec((None, t, 2 * LANES), lambda bi, h, i: (bi, i, h)),
                  pl.BlockSpec((None, 2 * LANES, s), lambda bi, h, i: (bi, h, 0)),
                  pl.BlockSpec((None, s, LANES), lambda bi, h, i: (bi, 0, h))],
        out_specs=pl.BlockSpec((None, t, LANES), lambda bi, h, i: (bi, i, h)),
        out_shape=jax.ShapeDtypeStruct((b, s, MLA_HEADS * MLA_V), BF16),
        scratch_shapes=[pltpu.VMEM((2, t, LANES), F32)] * 3,
        compiler_params=_cparams(("parallel", "parallel", "arbitrary")),
        name="mla_attn",
    )(bound, qm, km, vm)


def _split_bf16(x):
    hi = x.astype(BF16)
    lo = (x - hi.astype(F32)).astype(BF16)
    return hi, lo


def _outproj_kernel(x_ref, od_ref, om_ref, wo1_ref, wo2_ref, fn_ref, wrh_ref, wrl_ref,
                    x1_ref, h2_ref, aff_ref):
    x1 = (x_ref[...]
          + jnp.dot(od_ref[...], wo1_ref[...], preferred_element_type=F32)
          + jnp.dot(om_ref[...], wo2_ref[...], preferred_element_type=F32))
    x1_ref[...] = x1
    h = _rms_rows(x1, fn_ref[...])
    hh, hl = _split_bf16(h)
    h2_ref[...] = hh
    logits = (jnp.dot(hh, wrh_ref[...], preferred_element_type=F32)
              + jnp.dot(hl, wrh_ref[...], preferred_element_type=F32)
              + jnp.dot(hh, wrl_ref[...], preferred_element_type=F32))
    lane = lax.broadcasted_iota(I32, logits.shape, 1)
    logits = jnp.where(lane < N_EXPERTS, logits, NEG_BIG)
    e = jnp.exp(logits - jnp.max(logits, axis=-1, keepdims=True))
    aff = e / jnp.sum(e, axis=-1, keepdims=True)
    aff_ref[...] = aff[:, :N_EXPERTS]


def _outproj(x2d, od, om, p):
    t = x2d.shape[0]
    tm = OUT_TM
    row = lambda i: (i, 0)
    wspec = lambda a: pl.BlockSpec(a.shape, lambda i: (0, 0))
    weights = (p["w_o1"], p["w_o2"], p["ffn_norm"], p["w_r_hi"], p["w_r_lo"])
    return pl.pallas_call(
        _outproj_kernel,
        grid=(t // tm,),
        in_specs=[pl.BlockSpec((tm, D_MODEL), row), pl.BlockSpec((tm, DIFF_W), row),
                  pl.BlockSpec((tm, MLA_HEADS * MLA_V), row)] + [wspec(w) for w in weights],
        out_specs=[pl.BlockSpec((tm, D_MODEL), row), pl.BlockSpec((tm, D_MODEL), row),
                   pl.BlockSpec((tm, N_EXPERTS), row)],
        out_shape=[jax.ShapeDtypeStruct((t, D_MODEL), F32), jax.ShapeDtypeStruct((t, D_MODEL), BF16),
                   jax.ShapeDtypeStruct((t, N_EXPERTS), F32)],
        compiler_params=_cparams(("parallel",)),
        name="outproj",
    )(x2d, od, om, *weights)


def _route_kernel(aff_ref, pos_ref, gate_ref, csx_ref, *, cap):
    rows = aff_ref.shape[0]
    groups = LANES // N_EXPERTS
    aff = aff_ref[...]
    bits = pltpu.bitcast(aff, I32)

    def expert_total(v):
        for sh in (N_EXPERTS, 2 * N_EXPERTS, 4 * N_EXPERTS):
            v = v + pltpu.roll(v, sh, 1)
        return v

    def count(mask):
        return expert_total(jnp.sum(mask.astype(I32), axis=0, keepdims=True))

    def search(i, thr):
        cand = thr | jnp.left_shift(jnp.int32(1), 30 - i)
        return jnp.where(count(bits >= cand) >= cap, cand, thr)

    thr = lax.fori_loop(0, 31, search, jnp.zeros((1, LANES), I32))

    a = lax.broadcasted_iota(I32, (LANES, LANES), 0)
    b = lax.broadcasted_iota(I32, (LANES, LANES), 1)
    same = (a % N_EXPERTS) == (b % N_EXPERTS)
    q_all = same.astype(BF16)
    q_before = (same & (a // N_EXPERTS < b // N_EXPERTS)).astype(BF16)
    rb = ROUTE_RB
    ra = lax.broadcasted_iota(I32, (rb, rb), 0)
    ca = lax.broadcasted_iota(I32, (rb, rb), 1)
    tri = (ca < ra).astype(BF16)

    def prefix(mask):
        mb = mask.astype(BF16)
        outs = []
        offset = jnp.zeros((1, LANES), F32)
        for r0 in range(0, rows, rb):
            blk = mb[r0:r0 + rb]
            tot = jnp.dot(blk, q_all, preferred_element_type=F32)
            within = jnp.dot(blk, q_before, preferred_element_type=F32)
            above = jnp.dot(tri, tot.astype(BF16), preferred_element_type=F32)
            outs.append(above + within + offset)
            offset = offset + jnp.sum(tot, axis=0, keepdims=True)
        return jnp.concatenate(outs, axis=0)

    gt = bits > thr
    eq = bits == thr
    need = (cap - count(gt)).astype(F32)
    sel = gt | (eq & (prefix(eq) < need))
    csx = prefix(sel)
    pos_ref[...] = jnp.where(sel, csx.astype(I32), -1)
    gate_ref[...] = jnp.where(sel, aff, 0.0)
    csx_ref[...] = csx.astype(I32)


def _route(aff_packed, cap):
    rows = aff_packed.shape[0]
    vm = pl.BlockSpec(memory_space=pltpu.VMEM)
    return pl.pallas_call(
        functools.partial(_route_kernel, cap=cap),
        in_specs=[vm],
        out_specs=[vm, vm, vm],
        out_shape=[jax.ShapeDtypeStruct((rows, LANES), I32), jax.ShapeDtypeStruct((rows, LANES), F32),
                   jax.ShapeDtypeStruct((rows, LANES), I32)],
        compiler_params=pltpu.CompilerParams(vmem_limit_bytes=VMEM_LIMIT),
        name="route",
    )(aff_packed)


def _window_start(first_slot, align, w, cap):
    return jnp.minimum((first_slot // align) * align, cap - w)


def _gather_kernel(p0_ref, pn_ref, pos_ref, h2_ref, xg_ref, *, cap, w):
    ne = xg_ref.shape[0]
    e0 = pl.program_id(0) * ne
    i = pl.program_id(1)
    tg = h2_ref.shape[0]

    @pl.when(i == 0)
    def _():
        xg_ref[...] = jnp.zeros(xg_ref.shape, BF16)

    row_id = lax.broadcasted_iota(I32, (w, tg), 0)
    first = [pl.multiple_of(_window_start(p0_ref[e0 + j, i], BF16_ROWS, w, cap), BF16_ROWS) for j in range(ne)]
    prows = [pos_ref[j, pl.ds(i, 1), :] for j in range(ne)]

    onehot = jnp.concatenate([(prows[j] == row_id + first[j]).astype(BF16) for j in range(ne)], axis=0)
    rows = jnp.dot(onehot, h2_ref[...], preferred_element_type=F32).astype(BF16)
    for j in range(ne):
        xg_ref[j, pl.ds(first[j], w), :] += rows[j * w:(j + 1) * w]

    for j in range(ne):
        n_win = (pn_ref[e0 + j, i] - first[j] + w - 1) // w

        def window(k, carry, j=j):
            nominal = first[j] + k * w
            start = pl.multiple_of(jnp.minimum(nominal, cap - w), BF16_ROWS)
            hit = (prows[j] == row_id + start) & (prows[j] >= nominal)
            more = jnp.dot(hit.astype(BF16), h2_ref[...], preferred_element_type=F32).astype(BF16)
            xg_ref[j, pl.ds(start, w), :] += more
            return carry

        lax.fori_loop(1, n_win, window, 0)


def _gather(p0, pn, pos_t, h2, cap):
    t = h2.shape[0]
    tg, ne = GATHER_TG, GATHER_NE
    return pl.pallas_call(
        functools.partial(_gather_kernel, cap=cap, w=GATHER_W),
        grid_spec=pltpu.PrefetchScalarGridSpec(
            num_scalar_prefetch=2,
            grid=(N_EXPERTS // ne, t // tg),
            in_specs=[pl.BlockSpec((ne, t // tg, tg), lambda e, i, *_: (e, 0, 0)),
                      pl.BlockSpec((tg, D_MODEL), lambda e, i, *_: (i, 0))],
            out_specs=pl.BlockSpec((ne, cap, D_MODEL), lambda e, i, *_: (e, 0, 0))),
        out_shape=jax.ShapeDtypeStruct((N_EXPERTS, cap, D_MODEL), BF16),
        compiler_params=_cparams(("parallel", "arbitrary")),
        name="gather",
    )(p0, pn, pos_t, h2)


def _ffn_kernel(xg_ref, wg_ref, wu_ref, wd_ref, y_ref, acc_scr):
    f = pl.program_id(1)

    @pl.when((pl.program_id(0) == 0) & (f == 0))
    def _():
        acc_scr[...] = jnp.zeros(acc_scr.shape, F32)

    xg = xg_ref[...]
    a = jnp.dot(xg, wg_ref[...].astype(BF16), preferred_element_type=F32)
    b = jnp.dot(xg, wu_ref[...].astype(BF16), preferred_element_type=F32)
    act = (a * jax.nn.sigmoid(a) * b).astype(BF16)
    part = jnp.dot(act, wd_ref[...].astype(BF16), preferred_element_type=F32)
    acc_scr[...] = jnp.where(f == 0, part, acc_scr[...] + part)

    @pl.when(f == pl.num_programs(1) - 1)
    def _():
        y_ref[...] = acc_scr[...].astype(BF16)


def _ffn(xg, w_gate, w_up, w_down):
    _, cap, _ = xg.shape
    tf = FFN_TF
    return pl.pallas_call(
        _ffn_kernel,
        grid=(N_EXPERTS, D_FF // tf),
        in_specs=[pl.BlockSpec((None, cap, D_MODEL), lambda e, f: (e, 0, 0)),
                  pl.BlockSpec((None, D_MODEL, tf), lambda e, f: (e, 0, f)),
                  pl.BlockSpec((None, D_MODEL, tf), lambda e, f: (e, 0, f)),
                  pl.BlockSpec((None, tf, D_MODEL), lambda e, f: (e, f, 0))],
        out_specs=pl.BlockSpec((None, cap, D_MODEL), lambda e, f: (e, 0, 0)),
        out_shape=jax.ShapeDtypeStruct((N_EXPERTS, cap, D_MODEL), BF16),
        scratch_shapes=[pltpu.VMEM((cap, D_MODEL), F32)],
        compiler_params=_cparams(("arbitrary", "arbitrary")),
        name="ffn",
    )(xg, w_gate, w_up, w_down)


def _combine_kernel(p0_ref, pn_ref, x1_ref, pos_ref, gate_ref, y_hbm, o_ref, win, xwin, sem, xsem, *, cap):
    i = pl.program_id(0)
    tt = x1_ref.shape[0]
    w = xwin.shape[0]

    def first_window(e):
        return _window_start(p0_ref[i, e], BF16_ROWS, w, cap)

    def window_copy(e):
        return pltpu.make_async_copy(y_hbm.at[e, pl.ds(first_window(e), w), :],
                                     win.at[pl.ds(e * w, w), :], sem.at[e])

    for e in range(N_EXPERTS):
        window_copy(e).start()

    lane = lax.broadcasted_iota(I32, (tt, w), 1)

    def weights(e, first_slot, lo_slot):
        pcol = pos_ref[:, e:e + 1]
        hit = (pcol - first_slot == lane) & (pcol >= lo_slot)
        return jnp.where(hit, gate_ref[:, e:e + 1], 0.0).astype(BF16)

    c = jnp.concatenate([weights(e, first_window(e), 0) for e in range(N_EXPERTS)], axis=1)
    for e in range(N_EXPERTS):
        window_copy(e).wait()
    o_ref[...] = x1_ref[...] + jnp.dot(c, win[...], preferred_element_type=F32)

    for e in range(N_EXPERTS):
        w0 = first_window(e)
        n_win = (pn_ref[i, e] - w0 + w - 1) // w

        def extra(k, carry, e=e, w0=w0):
            nominal = w0 + k * w
            start = jnp.minimum(nominal, cap - w)
            cp = pltpu.make_async_copy(y_hbm.at[e, pl.ds(start, w), :], xwin, xsem)
            cp.start()
            cp.wait()
            o_ref[...] += jnp.dot(weights(e, start, nominal), xwin[...], preferred_element_type=F32)
            return carry

        lax.fori_loop(1, n_win, extra, 0)


def _combine(p0, pn, x1, pos_tok, gate_tok, y, cap):
    t = x1.shape[0]
    tt, w = COMB_TT, COMB_W
    row = lambda i, *_: (i, 0)
    return pl.pallas_call(
        functools.partial(_combine_kernel, cap=cap),
        grid_spec=pltpu.PrefetchScalarGridSpec(
            num_scalar_prefetch=2,
            grid=(t // tt,),
            in_specs=[pl.BlockSpec((tt, D_MODEL), row), pl.BlockSpec((tt, N_EXPERTS), row),
                      pl.BlockSpec((tt, N_EXPERTS), row), pl.BlockSpec(memory_space=pl.ANY)],
            out_specs=pl.BlockSpec((tt, D_MODEL), row),
            scratch_shapes=[pltpu.VMEM((N_EXPERTS * w, D_MODEL), BF16), pltpu.VMEM((w, D_MODEL), BF16),
                            pltpu.SemaphoreType.DMA((N_EXPERTS,)), pltpu.SemaphoreType.DMA(())]),
        out_shape=jax.ShapeDtypeStruct((t, D_MODEL), F32),
        compiler_params=_cparams(("arbitrary",)),
        name="combine",
    )(p0, pn, x1, pos_tok, gate_tok, y)


def _score_bound(width, q_gain, k_gain, max_bias):
    gmax = lambda g: jnp.max(jnp.abs(g.astype(F32)))
    bound = BOUND_SLACK * (width ** 0.5 * LOG2E * gmax(q_gain) * gmax(k_gain) + LOG2E * max_bias)
    return jnp.stack([bound, (3.0 * bound < F32_EXP_RANGE).astype(F32)]).astype(F32)


def _layer_params(l, rel_bias, attn_norm, w_in, diff_q_norm, diff_k_norm, lambda_q1, lambda_k1, lambda_q2,
                  lambda_k2, diff_subln, mla_q_latent_norm, mla_kv_latent_norm, w_uq, w_ukv, mla_q_norm,
                  mla_k_norm, w_out, ffn_norm, w_router):
    row = lambda v: v.reshape(1, -1).astype(F32)
    w = w_in[l]
    o = 3 * DIFF_W
    pad_rope = MLA_NOPE, LANES - MLA_QK
    w_kpe = jnp.pad(w[:, o + Q_LORA + KV_LORA:], ((0, 0), pad_rope))
    uq = jnp.pad(w_uq[l].reshape(Q_LORA, MLA_HEADS, MLA_QK), ((0, 0), (0, 0), (0, LANES - MLA_QK)))
    ukv = w_ukv[l].reshape(KV_LORA, MLA_HEADS, MLA_NOPE + MLA_V)
    uk = jnp.pad(ukv[:, :, :MLA_NOPE], ((0, 0), (0, 0), (0, LANES - MLA_NOPE)))
    head_gain = lambda g: jnp.pad(g.astype(F32), (0, LANES - MLA_QK)).reshape(1, LANES)
    half = MLA_ROPE // 2

    def swap_rope_halves(a):
        z = jnp.zeros_like(a)
        x1, x2 = a[..., MLA_NOPE:MLA_NOPE + half], a[..., MLA_NOPE + half:MLA_QK]
        return jnp.concatenate([z[..., :MLA_NOPE], x2, x1, z[..., MLA_QK:]], axis=-1)

    wr = jnp.pad(w_router[l].astype(F32), ((0, 0), (0, LANES - N_EXPERTS)))
    wr_hi = wr.astype(BF16)
    return dict(
        attn_norm=row(attn_norm[l]),
        w_qkv=w[:, :o].astype(BF16), w_cq=w[:, o:o + Q_LORA].astype(BF16),
        w_ckv=w[:, o + Q_LORA:o + Q_LORA + KV_LORA].astype(BF16), w_kpe=w_kpe.astype(BF16),
        w_uq=uq.reshape(Q_LORA, MLA_W).astype(BF16), w_uk=uk.reshape(KV_LORA, MLA_W).astype(BF16),
        w_uq_sw=swap_rope_halves(uq).reshape(Q_LORA, MLA_W).astype(BF16),
        mq_gain_sw=swap_rope_halves(head_gain(mla_q_norm[l])),
        w_uv=ukv[:, :, MLA_NOPE:].reshape(KV_LORA, MLA_HEADS * MLA_V).astype(BF16),
        dq_gain=row(jnp.tile(diff_q_norm[l], 2)), dk_gain=row(jnp.tile(diff_k_norm[l], 2)),
        cq_gain=row(mla_q_latent_norm[l]), ckv_gain=row(mla_kv_latent_norm[l]),
        mq_gain=head_gain(mla_q_norm[l]), mk_gain=head_gain(mla_k_norm[l]),
        subln=row(diff_subln[l]), lq1=row(lambda_q1[l]), lk1=row(lambda_k1[l]),
        lq2=row(lambda_q2[l]), lk2=row(lambda_k2[l]),
        rel_bias_t=rel_bias.astype(F32).T,
        diff_bound=_score_bound(DIFF_DK, diff_q_norm[l], diff_k_norm[l], jnp.max(jnp.abs(rel_bias))),
        mla_bound=_score_bound(MLA_QK, mla_q_norm[l], mla_k_norm[l], 0.0),
        w_o1=w_out[l][:DIFF_W].astype(BF16), w_o2=w_out[l][DIFF_W:].astype(BF16),
        ffn_norm=row(ffn_norm[l]), w_r_hi=wr_hi, w_r_lo=(wr - wr_hi.astype(F32)).astype(BF16),
    )


def _rope_tables(seq):
    half = MLA_ROPE // 2
    inv = 1.0 / (ROPE_BASE ** (jnp.arange(half, dtype=F32) / half))
    ang = jnp.arange(seq, dtype=jnp.int32).astype(F32)[:, None] * inv[None, :]
    cos, sin = jnp.cos(ang), jnp.sin(ang)
    ones = jnp.ones((seq, MLA_NOPE), F32)
    tail = LANES - MLA_QK
    cos_t = jnp.concatenate([ones, cos, cos, jnp.ones((seq, tail), F32)], axis=1)
    sin_t = jnp.concatenate([0 * ones, -sin, sin, jnp.zeros((seq, tail), F32)], axis=1)
    return cos_t, sin_t


def _layer(x, p, bias, w_gate, w_up, w_down, lam_init):
    b, s, d = x.shape
    t = b * s
    cap = CAPACITY_FACTOR * t // N_EXPERTS
    x2d = x.reshape(t, d)

    cos_t, sin_t = _rope_tables(s)
    qd, kd, vd, qm, km, vm = _prep(x2d, s, cos_t, sin_t, p)
    shp = lambda a: a.reshape(b, s, a.shape[-1])
    keys_t = lambda a: jnp.swapaxes(shp(a), 1, 2)
    od = _diff_attention(shp(qd), keys_t(kd), shp(vd), bias, p["rel_bias_t"], p, lam_init)
    om = _mla_attention(shp(qm), keys_t(km), shp(vm), p["mla_bound"])
    x1, h2, aff = _outproj(x2d, od.reshape(t, -1), om.reshape(t, -1), p)

    groups = LANES // N_EXPERTS
    pos_p, gate_p, csx_p = _route(aff.reshape(t // groups, LANES), cap)
    pos_tok = pos_p.reshape(t, N_EXPERTS)
    gate_tok = gate_p.reshape(t, N_EXPERTS)
    csx_tok = csx_p.reshape(t, N_EXPERTS)

    def tile_slots(tile):
        first = csx_tok[::tile]
        return first, jnp.concatenate([first[1:], jnp.full((1, N_EXPERTS), cap, I32)], axis=0)

    g0, gn = tile_slots(GATHER_TG)
    pos_t = pos_tok.T.reshape(N_EXPERTS, t // GATHER_TG, GATHER_TG)
    xg = _gather(g0.T, gn.T, pos_t, h2, cap)
    y = _ffn(xg, w_gate, w_up, w_down)
    c0, cn = tile_slots(COMB_TT)
    out = _combine(c0, cn, x1, pos_tok, gate_tok, y, cap)
    return out.reshape(b, s, d)


def kernel(x_prompt, x_sample, rel_bias, attn_norm, w_in, diff_q_norm, diff_k_norm, lambda_q1, lambda_k1, lambda_q2, lambda_k2, diff_subln, mla_q_latent_norm, mla_kv_latent_norm, w_uq, w_ukv, mla_q_norm, mla_k_norm, w_out, ffn_norm, w_router, w_gate, w_up, w_down):
    params = [_layer_params(l, rel_bias, attn_norm, w_in, diff_q_norm, diff_k_norm, lambda_q1, lambda_k1,
                            lambda_q2, lambda_k2, diff_subln, mla_q_latent_norm, mla_kv_latent_norm, w_uq,
                            w_ukv, mla_q_norm, mla_k_norm, w_out, ffn_norm, w_router) for l in range(DEPTH)]
    biases = [_bias_tiles(p["rel_bias_t"], ATT_T) for p in params]
    outs = []
    for x in (x_prompt, x_sample):
        for l in range(DEPTH):
            lam_init = 0.8 - 0.6 * math.exp(-0.3 * l)
            x = _layer(x, params[l], biases[l], w_gate[l], w_up[l], w_down[l], lam_init)
        outs.append(x)
    return tuple(outs)
```

```python
import functools
import math

import jax
import jax.numpy as jnp
from jax import lax
from jax.experimental import pallas as pl
from jax.experimental.pallas import tpu as pltpu

F32 = jnp.float32
BF16 = jnp.bfloat16
I32 = jnp.int32

D_MODEL = 1024
DEPTH = 1
DIFF_HEADS = 4
DIFF_DK = 64
DIFF_DV = 128
MLA_HEADS = 8
MLA_NOPE = 64
MLA_ROPE = 32
MLA_QK = MLA_NOPE + MLA_ROPE
MLA_V = 64
Q_LORA = 256
KV_LORA = 128
ROPE_BASE = 10000.0
N_BUCKETS = 32
MAX_DISTANCE = 128
N_EXPERTS = 16
CAPACITY_FACTOR = 2
D_FF = 2816
EPS = 1e-6

LANES = 128
SUBLANES = 8
BF16_ROWS = 16
DIFF_W = DIFF_HEADS * DIFF_DV
MLA_W = MLA_HEADS * LANES
NEG_BIG = -1e30
LOG2E = math.log2(math.e)
VMEM_LIMIT = 56 * 1024 * 1024
F32_EXP_RANGE = 120.0
BOUND_SLACK = 1.02

PREP_TM = 512
ATT_T = 512
DIFF_TK = 2048
BIAS_TILES = 5
MLA_TQ = 512
MLA_TK = 4096
OUT_TM = 512
ROUTE_RB = 256
GATHER_TG = 1024
GATHER_W = 192
GATHER_NE = 4
FFN_TF = 256
COMB_TT = 512
COMB_W = 128
COMB_GROUP = 4


def _cparams(sem, vmem=VMEM_LIMIT):
    return pltpu.CompilerParams(dimension_semantics=sem, vmem_limit_bytes=vmem)


def _rms_rows(x, gain):
    ms = jnp.mean(x * x, axis=-1, keepdims=True)
    return x * lax.rsqrt(ms + EPS) * gain


def _prep_kernel(x_ref, cos_ref, sin_ref, an_ref, wqkv_ref, wcq_ref, wckv_ref, wkpe_ref,
                 wuq_ref, wuqs_ref, wuk_ref, wuv_ref, dqg_ref, dkg_ref, cqg_ref, ckvg_ref, mqg_ref, mqgs_ref,
                 mkg_ref, qd_ref, kd_ref, vd_ref, qm_ref, km_ref, vm_ref):
    tm = x_ref.shape[0]
    hb = _rms_rows(x_ref[...], an_ref[...]).astype(BF16)

    lane = lax.broadcasted_iota(I32, (tm, LANES), 1)
    lo = lane < DIFF_DK

    qkv = jnp.dot(hb, wqkv_ref[...], preferred_element_type=F32)
    diff_scale = DIFF_DK ** -0.5 * LOG2E
    for hd in range(DIFF_HEADS):
        for off, g_ref, o_ref, scale in ((0, dqg_ref, qd_ref, diff_scale), (DIFF_W, dkg_ref, kd_ref, None)):
            blk = qkv[:, off + hd * LANES: off + (hd + 1) * LANES]
            sq = blk * blk
            s_lo = jnp.sum(jnp.where(lo, sq, 0.0), axis=-1, keepdims=True)
            s_hi = jnp.sum(jnp.where(lo, 0.0, sq), axis=-1, keepdims=True)
            r = jnp.where(lo, lax.rsqrt(s_lo / DIFF_DK + EPS), lax.rsqrt(s_hi / DIFF_DK + EPS))
            y = blk * r * g_ref[...]
            if scale is not None:
                y = y * scale
            o_ref[:, hd * LANES:(hd + 1) * LANES] = y.astype(BF16)
    vd_ref[...] = qkv[:, 2 * DIFF_W:3 * DIFF_W].astype(BF16)

    cq = _rms_rows(jnp.dot(hb, wcq_ref[...], preferred_element_type=F32), cqg_ref[...]).astype(BF16)
    ckv = _rms_rows(jnp.dot(hb, wckv_ref[...], preferred_element_type=F32), ckvg_ref[...]).astype(BF16)
    kpe = jnp.dot(hb, wkpe_ref[...], preferred_element_type=F32)
    q = jnp.dot(cq, wuq_ref[...], preferred_element_type=F32)
    q_sw = jnp.dot(cq, wuqs_ref[...], preferred_element_type=F32)
    k = jnp.dot(ckv, wuk_ref[...], preferred_element_type=F32)
    vm_ref[...] = jnp.dot(ckv, wuv_ref[...], preferred_element_type=F32).astype(BF16)

    cosb = cos_ref[...]
    sinb = sin_ref[...]
    mla_scale = MLA_QK ** -0.5 * LOG2E
    first_half = (lane >= MLA_NOPE) & (lane < MLA_NOPE + MLA_ROPE // 2)
    kpe_g = kpe * mkg_ref[...]
    kpe_rot = kpe_g * cosb + sinb * jnp.where(first_half,
                                              pltpu.roll(kpe_g, LANES - MLA_ROPE // 2, 1),
                                              pltpu.roll(kpe_g, MLA_ROPE // 2, 1))
    kpe_ss = jnp.sum(kpe * kpe, axis=-1, keepdims=True)

    for hd in range(MLA_HEADS):
        sl = slice(hd * LANES, (hd + 1) * LANES)
        qb = q[:, sl]
        rq = lax.rsqrt(jnp.sum(qb * qb, axis=-1, keepdims=True) / MLA_QK + EPS) * mla_scale
        qm_ref[:, sl] = (rq * (qb * mqg_ref[...] * cosb + q_sw[:, sl] * mqgs_ref[...] * sinb)).astype(BF16)
        kb = k[:, sl]
        rk = lax.rsqrt((jnp.sum(kb * kb, axis=-1, keepdims=True) + kpe_ss) / MLA_QK + EPS)
        kn = rk * (kb * mkg_ref[...] + kpe_rot)
        km_ref[:, sl] = jnp.where(lane == MLA_QK, 1.0, kn).astype(BF16)


def _prep(x2d, seq, cos_t, sin_t, p):
    t = x2d.shape[0]
    tm = PREP_TM
    nseq = seq // tm
    row = lambda i: (i, 0)
    full = lambda i: (0, 0)
    wspec = lambda a: pl.BlockSpec(a.shape, full)
    weights = (p["attn_norm"], p["w_qkv"], p["w_cq"], p["w_ckv"], p["w_kpe"], p["w_uq"], p["w_uq_sw"], p["w_uk"],
               p["w_uv"], p["dq_gain"], p["dk_gain"], p["cq_gain"], p["ckv_gain"], p["mq_gain"], p["mq_gain_sw"],
               p["mk_gain"])
    out_w = (DIFF_W, DIFF_W, DIFF_W, MLA_W, MLA_W, MLA_HEADS * MLA_V)
    return pl.pallas_call(
        _prep_kernel,
        grid=(t // tm,),
        in_specs=[pl.BlockSpec((tm, D_MODEL), row),
                  pl.BlockSpec((tm, LANES), lambda i: (i % nseq, 0)),
                  pl.BlockSpec((tm, LANES), lambda i: (i % nseq, 0))] + [wspec(w) for w in weights],
        out_specs=[pl.BlockSpec((tm, w), row) for w in out_w],
        out_shape=[jax.ShapeDtypeStruct((t, w), BF16) for w in out_w],
        compiler_params=_cparams(("parallel",)),
        name="prep",
    )(x2d, cos_t, sin_t, *weights)


def _bias_kernel(rb_ref, o_ref):
    hd = pl.program_id(0)
    d = pl.program_id(1)
    tq, tk = o_ref.shape
    row = lax.broadcasted_iota(I32, (tq, tk), 0)
    col = lax.broadcasted_iota(I32, (tq, tk), 1)
    rel = col - row + (d - BIAS_TILES // 2) * tk
    half = N_BUCKETS // 2
    max_exact = half // 2
    n = jnp.abs(rel)
    nf = jnp.maximum(n, 1).astype(F32)
    large = max_exact + (jnp.log(nf / max_exact) / math.log(MAX_DISTANCE / max_exact)
                         * (half - max_exact)).astype(I32)
    large = jnp.minimum(large, half - 1)
    bucket = jnp.where(rel > 0, half, 0) + jnp.where(n < max_exact, n, large)
    acc = jnp.zeros((tq, tk), F32)
    for b in range(N_BUCKETS):
        acc = jnp.where(bucket == b, rb_ref[hd, b], acc)
    o_ref[...] = acc * LOG2E


def _bias_tiles(rel_bias_t, t):
    return pl.pallas_call(
        _bias_kernel,
        grid=(DIFF_HEADS, BIAS_TILES),
        in_specs=[pl.BlockSpec(memory_space=pltpu.SMEM)],
        out_specs=pl.BlockSpec((None, None, t, t), lambda h, d: (h, d, 0, 0)),
        out_shape=jax.ShapeDtypeStruct((DIFF_HEADS, BIAS_TILES, t, t), F32),
        compiler_params=_cparams(("parallel", "parallel")),
        name="bias_tiles",
    )(rel_bias_t)


def _online_update(mi, s, v, m_scr, l_scr, acc_scr):
    blocks = [s[:, i:i + LANES] for i in range(0, s.shape[1], LANES)]
    m_old = m_scr[mi]
    row_max = jnp.max(functools.reduce(jnp.maximum, blocks), axis=-1, keepdims=True)
    m_new = jnp.maximum(m_old, row_max)
    alpha = jnp.exp2(m_old - m_new)
    ps = [jnp.exp2(blk - m_new) for blk in blocks]
    l_scr[mi] = alpha * l_scr[mi] + functools.reduce(jnp.add, ps)
    p = jnp.concatenate([x.astype(BF16) for x in ps], axis=1)
    acc_scr[mi] = alpha * acc_scr[mi] + jnp.dot(p, v, preferred_element_type=F32)
    m_scr[mi] = m_new


def _fixed_ref_update(mi, s, v, l_scr, acc_scr):
    ps = [jnp.exp2(s[:, i:i + LANES]) for i in range(0, s.shape[1], LANES)]
    l_scr[mi] += functools.reduce(jnp.add, ps)
    p = jnp.concatenate([x.astype(BF16) for x in ps], axis=1)
    acc_scr[mi] += jnp.dot(p, v, preferred_element_type=F32)


def _normalised(mi, l_scr, acc_scr):
    return acc_scr[mi] / jnp.sum(l_scr[mi], axis=-1, keepdims=True)


def _diff_attn_kernel(rb_ref, bound_ref, q_ref, k_ref, v_ref, bias_ref, subln_ref, lq1_ref, lk1_ref, lq2_ref,
                      lk2_ref, o_ref, m_scr, l_scr, acc_scr, *, lam_init, tk):
    hd = pl.program_id(1)
    qi = pl.program_id(2)
    t = q_ref.shape[0]
    n = v_ref.shape[0] // t
    r = tk // t
    bound = bound_ref[0]
    fixed_ref = bound_ref[1] > 0.5

    q = q_ref[...]
    lane = lax.broadcasted_iota(I32, q.shape, 1)
    zero = jnp.zeros_like(q)
    qs = (jnp.where(lane < DIFF_DK, q, zero), jnp.where(lane < DIFF_DK, zero, q))

    l_scr[...] = jnp.zeros(l_scr.shape, F32)
    acc_scr[...] = jnp.zeros(acc_scr.shape, F32)

    lo = jnp.maximum(qi - 1, 0) // r
    hi = (jnp.minimum(qi + 2, n) + r - 1) // r
    c_left = rb_ref[hd, N_BUCKETS // 2 - 1] * LOG2E
    c_right = rb_ref[hd, N_BUCKETS - 1] * LOG2E
    n_side = bias_ref.shape[0] // 2

    def sweep(update, after_left, after_near):
        def make_step(near):
            def step(j, carry):
                off = pl.multiple_of(j * tk, tk)
                k = k_ref[:, pl.ds(off, tk)]
                v = v_ref[pl.ds(off, tk), :]
                for mi in range(2):
                    s = jnp.dot(qs[mi], k, preferred_element_type=F32)
                    if near:
                        pieces = []
                        for c in range(r):
                            d = jnp.clip(j * r + c - qi, -n_side, n_side)
                            pieces.append(s[:, c * t:(c + 1) * t] + bias_ref[d + n_side])
                        s = jnp.concatenate(pieces, axis=1)
                    update(mi, s, v)
                return carry
            return step

        lax.fori_loop(0, lo, make_step(False), 0)
        after_left()
        lax.fori_loop(lo, hi, make_step(True), 0)
        after_near()
        lax.fori_loop(hi, n // r, make_step(False), 0)

    @pl.when(fixed_ref)
    def _():
        def rescale(c):
            factor = jnp.exp2(jnp.full((1, LANES), c, F32))
            l_scr[...] = l_scr[...] * factor
            acc_scr[...] = acc_scr[...] * factor

        sweep(lambda mi, s, v: _fixed_ref_update(mi, s - bound, v, l_scr, acc_scr),
              lambda: rescale(c_left), lambda: rescale(-c_right))

    @pl.when(jnp.logical_not(fixed_ref))
    def _():
        m_scr[...] = jnp.full(m_scr.shape, NEG_BIG, F32)

        def shift(c):
            m_scr[...] = m_scr[...] + c

        sweep(lambda mi, s, v: _online_update(mi, s, v, m_scr, l_scr, acc_scr),
              lambda: shift(c_left), lambda: shift(-c_right))

    lam = (jnp.exp(jnp.sum(lq1_ref[...] * lk1_ref[...], axis=-1, keepdims=True))
           - jnp.exp(jnp.sum(lq2_ref[...] * lk2_ref[...], axis=-1, keepdims=True)) + lam_init)
    o = _normalised(0, l_scr, acc_scr) - lam * _normalised(1, l_scr, acc_scr)
    o_ref[...] = (_rms_rows(o, subln_ref[...]) * (1.0 - lam_init)).astype(BF16)


def _diff_attention(qd, kd, vd, bias_tiles, rel_bias_t, p, lam_init):
    b, s, _ = qd.shape
    t = ATT_T
    vec = lambda a: pl.BlockSpec(a.shape, lambda bi, h, i: (0, 0))
    return pl.pallas_call(
        functools.partial(_diff_attn_kernel, lam_init=lam_init, tk=DIFF_TK),
        grid=(b, DIFF_HEADS, s // t),
        in_specs=[pl.BlockSpec(memory_space=pltpu.SMEM), pl.BlockSpec(memory_space=pltpu.SMEM),
                  pl.BlockSpec((None, t, LANES), lambda bi, h, i: (bi, i, h)),
                  pl.BlockSpec((None, LANES, s), lambda bi, h, i: (bi, h, 0)),
                  pl.BlockSpec((None, s, LANES), lambda bi, h, i: (bi, 0, h)),
                  pl.BlockSpec((None, BIAS_TILES, t, t), lambda bi, h, i: (h, 0, 0, 0)),
                  vec(p["subln"]), vec(p["lq1"]), vec(p["lk1"]), vec(p["lq2"]), vec(p["lk2"])],
        out_specs=pl.BlockSpec((None, t, LANES), lambda bi, h, i: (bi, i, h)),
        out_shape=jax.ShapeDtypeStruct((b, s, DIFF_W), BF16),
        scratch_shapes=[pltpu.VMEM((2, t, LANES), F32)] * 3,
        compiler_params=_cparams(("parallel", "parallel", "arbitrary")),
        name="diff_attn",
    )(rel_bias_t, p["diff_bound"], qd, kd, vd, bias_tiles, p["subln"], p["lq1"], p["lk1"], p["lq2"], p["lk2"])


def _mla_attn_kernel(bound_ref, q_ref, k_ref, v_ref, o_ref, m_scr, l_scr, acc_scr, *, tk):
    t = q_ref.shape[0]
    n = v_ref.shape[0] // tk
    bound = bound_ref[0]
    fixed_ref = bound_ref[1] > 0.5

    l_scr[...] = jnp.zeros(l_scr.shape, F32)
    acc_scr[...] = jnp.zeros(acc_scr.shape, F32)

    def sweep(q, update):
        qs = (q[:, :LANES], q[:, LANES:])

        def step(j, carry):
            off = pl.multiple_of(j * tk, tk)
            v = v_ref[pl.ds(off, tk), :]
            for mi in range(2):
                k = k_ref[mi * LANES:(mi + 1) * LANES, pl.ds(off, tk)]
                update(mi, jnp.dot(qs[mi], k, preferred_element_type=F32), v)
            return carry

        lax.fori_loop(0, n, step, 0)

    @pl.when(fixed_ref)
    def _():
        q = q_ref[...]
        lane = lax.broadcasted_iota(I32, q.shape, 1)
        q = jnp.where(lane % LANES == MLA_QK, jnp.full(q.shape, -bound, F32).astype(BF16), q)
        sweep(q, lambda mi, s, v: _fixed_ref_update(mi, s, v, l_scr, acc_scr))

    @pl.when(jnp.logical_not(fixed_ref))
    def _():
        m_scr[...] = jnp.full(m_scr.shape, NEG_BIG, F32)
        sweep(q_ref[...], lambda mi, s, v: _online_update(mi, s, v, m_scr, l_scr, acc_scr))

    lane = lax.broadcasted_iota(I32, (t, LANES), 1)
    o = jnp.where(lane < MLA_V, _normalised(0, l_scr, acc_scr), _normalised(1, l_scr, acc_scr))
    o_ref[...] = o.astype(BF16)


def _mla_attention(qm, km, vm, bound):
    b, s, _ = qm.shape
    t = MLA_TQ
    return pl.pallas_call(
        functools.partial(_mla_attn_kernel, tk=MLA_TK),
        grid=(b, MLA_HEADS // 2, s // t),
        in_specs=[pl.BlockSpec(memory_space=pltpu.SMEM),
                  pl.BlockSpec((None, t, 2 * LANES), lambda bi, h, i: (bi, i, h)),
                  pl.BlockSpec((None, 2 * LANES, s), lambda bi, h, i: (bi, h, 0)),
                  pl.BlockSpec((None, s, LANES), lambda bi, h, i: (bi, 0, h))],
        out_specs=pl.BlockSpec((None, t, LANES), lambda bi, h, i: (bi, i, h)),
        out_shape=jax.ShapeDtypeStruct((b, s, MLA_HEADS * MLA_V), BF16),
        scratch_shapes=[pltpu.VMEM((2, t, LANES), F32)] * 3,
        compiler_params=_cparams(("parallel", "parallel", "arbitrary")),
        name="mla_attn",
    )(bound, qm, km, vm)


def _split_bf16(x):
    hi = x.astype(BF16)
    lo = (x - hi.astype(F32)).astype(BF16)
    return hi, lo


def _outproj_kernel(x_ref, od_ref, om_ref, wo1_ref, wo2_ref, fn_ref, wrh_ref, wrl_ref,
                    x1_ref, h2_ref, aff_ref):
    x1 = (x_ref[...]
          + jnp.dot(od_ref[...], wo1_ref[...], preferred_element_type=F32)
          + jnp.dot(om_ref[...], wo2_ref[...], preferred_element_type=F32))
    x1_ref[...] = x1
    h = _rms_rows(x1, fn_ref[...])
    hh, hl = _split_bf16(h)
    h2_ref[...] = hh
    logits = (jnp.dot(hh, wrh_ref[...], preferred_element_type=F32)
              + jnp.dot(hl, wrh_ref[...], preferred_element_type=F32)
              + jnp.dot(hh, wrl_ref[...], preferred_element_type=F32))
    lane = lax.broadcasted_iota(I32, logits.shape, 1)
    logits = jnp.where(lane < N_EXPERTS, logits, NEG_BIG)
    e = jnp.exp(logits - jnp.max(logits, axis=-1, keepdims=True))
    aff = e / jnp.sum(e, axis=-1, keepdims=True)
    aff_ref[...] = aff[:, :N_EXPERTS]


def _outproj(x2d, od, om, p):
    t = x2d.shape[0]
    tm = OUT_TM
    row = lambda i: (i, 0)
    wspec = lambda a: pl.BlockSpec(a.shape, lambda i: (0, 0))
    weights = (p["w_o1"], p["w_o2"], p["ffn_norm"], p["w_r_hi"], p["w_r_lo"])
    return pl.pallas_call(
        _outproj_kernel,
        grid=(t // tm,),
        in_specs=[pl.BlockSpec((tm, D_MODEL), row), pl.BlockSpec((tm, DIFF_W), row),
                  pl.BlockSpec((tm, MLA_HEADS * MLA_V), row)] + [wspec(w) for w in weights],
        out_specs=[pl.BlockSpec((tm, D_MODEL), row), pl.BlockSpec((tm, D_MODEL), row),
                   pl.BlockSpec((tm, N_EXPERTS), row)],
        out_shape=[jax.ShapeDtypeStruct((t, D_MODEL), F32), jax.ShapeDtypeStruct((t, D_MODEL), BF16),
                   jax.ShapeDtypeStruct((t, N_EXPERTS), F32)],
        compiler_params=_cparams(("parallel",)),
        name="outproj",
    )(x2d, od, om, *weights)


def _route_kernel(aff_ref, pos_ref, gate_ref, csx_ref, *, cap):
    rows = aff_ref.shape[0]
    groups = LANES // N_EXPERTS
    aff = aff_ref[...]
    bits = pltpu.bitcast(aff, I32)

    def expert_total(v):
        for sh in (N_EXPERTS, 2 * N_EXPERTS, 4 * N_EXPERTS):
            v = v + pltpu.roll(v, sh, 1)
        return v

    def count(mask):
        return expert_total(jnp.sum(mask.astype(I32), axis=0, keepdims=True))

    def search(i, thr):
        cand = thr | jnp.left_shift(jnp.int32(1), 30 - i)
        return jnp.where(count(bits >= cand) >= cap, cand, thr)

    thr = lax.fori_loop(0, 31, search, jnp.zeros((1, LANES), I32))

    a = lax.broadcasted_iota(I32, (LANES, LANES), 0)
    b = lax.broadcasted_iota(I32, (LANES, LANES), 1)
    same = (a % N_EXPERTS) == (b % N_EXPERTS)
    q_all = same.astype(BF16)
    q_before = (same & (a // N_EXPERTS < b // N_EXPERTS)).astype(BF16)
    rb = ROUTE_RB
    ra = lax.broadcasted_iota(I32, (rb, rb), 0)
    ca = lax.broadcasted_iota(I32, (rb, rb), 1)
    tri = (ca < ra).astype(BF16)

    def prefix(mask):
        mb = mask.astype(BF16)
        outs = []
        offset = jnp.zeros((1, LANES), F32)
        for r0 in range(0, rows, rb):
            blk = mb[r0:r0 + rb]
            tot = jnp.dot(blk, q_all, preferred_element_type=F32)
            within = jnp.dot(blk, q_before, preferred_element_type=F32)
            above = jnp.dot(tri, tot.astype(BF16), preferred_element_type=F32)
            outs.append(above + within + offset)
            offset = offset + jnp.sum(tot, axis=0, keepdims=True)
        return jnp.concatenate(outs, axis=0)

    gt = bits > thr
    eq = bits == thr
    need = (cap - count(gt)).astype(F32)
    sel = gt | (eq & (prefix(eq) < need))
    csx = prefix(sel)
    pos_ref[...] = jnp.where(sel, csx.astype(I32), -1)
    gate_ref[...] = jnp.where(sel, aff, 0.0)
    csx_ref[...] = csx.astype(I32)


def _route(aff_packed, cap):
    rows = aff_packed.shape[0]
    vm = pl.BlockSpec(memory_space=pltpu.VMEM)
    return pl.pallas_call(
        functools.partial(_route_kernel, cap=cap),
        in_specs=[vm],
        out_specs=[vm, vm, vm],
        out_shape=[jax.ShapeDtypeStruct((rows, LANES), I32), jax.ShapeDtypeStruct((rows, LANES), F32),
                   jax.ShapeDtypeStruct((rows, LANES), I32)],
        compiler_params=pltpu.CompilerParams(vmem_limit_bytes=VMEM_LIMIT),
        name="route",
    )(aff_packed)


def _window_start(first_slot, align, w, cap):
    return jnp.minimum((first_slot // align) * align, cap - w)


def _gather_kernel(p0_ref, pn_ref, pos_ref, h2_ref, xg_ref, *, cap, w):
    ne = xg_ref.shape[0]
    e0 = pl.program_id(0) * ne
    i = pl.program_id(1)
    tg = h2_ref.shape[0]

    @pl.when(i == 0)
    def _():
        xg_ref[...] = jnp.zeros(xg_ref.shape, BF16)

    row_id = lax.broadcasted_iota(I32, (w, tg), 0)
    first = [pl.multiple_of(_window_start(p0_ref[e0 + j, i], BF16_ROWS, w, cap), BF16_ROWS) for j in range(ne)]
    prows = [pos_ref[j, pl.ds(i, 1), :] for j in range(ne)]

    onehot = jnp.concatenate([(prows[j] == row_id + first[j]).astype(BF16) for j in range(ne)], axis=0)
    rows = jnp.dot(onehot, h2_ref[...], preferred_element_type=F32).astype(BF16)
    for j in range(ne):
        xg_ref[j, pl.ds(first[j], w), :] += rows[j * w:(j + 1) * w]

    for j in range(ne):
        n_win = (pn_ref[e0 + j, i] - first[j] + w - 1) // w

        def window(k, carry, j=j):
            nominal = first[j] + k * w
            start = pl.multiple_of(jnp.minimum(nominal, cap - w), BF16_ROWS)
            hit = (prows[j] == row_id + start) & (prows[j] >= nominal)
            more = jnp.dot(hit.astype(BF16), h2_ref[...], preferred_element_type=F32).astype(BF16)
            xg_ref[j, pl.ds(start, w), :] += more
            return carry

        lax.fori_loop(1, n_win, window, 0)


def _gather(p0, pn, pos_t, h2, cap):
    t = h2.shape[0]
    tg, ne = GATHER_TG, GATHER_NE
    return pl.pallas_call(
        functools.partial(_gather_kernel, cap=cap, w=GATHER_W),
        grid_spec=pltpu.PrefetchScalarGridSpec(
            num_scalar_prefetch=2,
            grid=(N_EXPERTS // ne, t // tg),
            in_specs=[pl.BlockSpec((ne, t // tg, tg), lambda e, i, *_: (e, 0, 0)),
                      pl.BlockSpec((tg, D_MODEL), lambda e, i, *_: (i, 0))],
            out_specs=pl.BlockSpec((ne, cap, D_MODEL), lambda e, i, *_: (e, 0, 0))),
        out_shape=jax.ShapeDtypeStruct((N_EXPERTS, cap, D_MODEL), BF16),
        compiler_params=_cparams(("parallel", "arbitrary")),
        name="gather",
    )(p0, pn, pos_t, h2)


def _ffn_kernel(xg_ref, wg_ref, wu_ref, wd_ref, y_ref, acc_scr):
    f = pl.program_id(1)

    @pl.when((pl.program_id(0) == 0) & (f == 0))
    def _():
        acc_scr[...] = jnp.zeros(acc_scr.shape, F32)

    xg = xg_ref[...]
    a = jnp.dot(xg, wg_ref[...].astype(BF16), preferred_element_type=F32)
    b = jnp.dot(xg, wu_ref[...].astype(BF16), preferred_element_type=F32)
    act = (a * jax.nn.sigmoid(a) * b).astype(BF16)
    part = jnp.dot(act, wd_ref[...].astype(BF16), preferred_element_type=F32)
    acc_scr[...] = jnp.where(f == 0, part, acc_scr[...] + part)

    @pl.when(f == pl.num_programs(1) - 1)
    def _():
        y_ref[...] = acc_scr[...].astype(BF16)


def _ffn(xg, w_gate, w_up, w_down):
    _, cap, _ = xg.shape
    tf = FFN_TF
    return pl.pallas_call(
        _ffn_kernel,
        grid=(N_EXPERTS, D_FF // tf),
        in_specs=[pl.BlockSpec((None, cap, D_MODEL), lambda e, f: (e, 0, 0)),
                  pl.BlockSpec((None, D_MODEL, tf), lambda e, f: (e, 0, f)),
                  pl.BlockSpec((None, D_MODEL, tf), lambda e, f: (e, 0, f)),
                  pl.BlockSpec((None, tf, D_MODEL), lambda e, f: (e, f, 0))],
        out_specs=pl.BlockSpec((None, cap, D_MODEL), lambda e, f: (e, 0, 0)),
        out_shape=jax.ShapeDtypeStruct((N_EXPERTS, cap, D_MODEL), BF16),
        scratch_shapes=[pltpu.VMEM((cap, D_MODEL), F32)],
        compiler_params=_cparams(("arbitrary", "arbitrary")),
        name="ffn",
    )(xg, w_gate, w_up, w_down)


def _combine_kernel(p0_ref, pn_ref, x1_ref, pos_ref, gate_ref, y_hbm, o_ref, win, xwin, sem, xsem, *, cap):
    i = pl.program_id(0)
    tt = x1_ref.shape[0]
    w = xwin.shape[0]

    def first_window(e):
        return _window_start(p0_ref[i, e], BF16_ROWS, w, cap)

    def window_copy(e):
        return pltpu.make_async_copy(y_hbm.at[e, pl.ds(first_window(e), w), :],
                                     win.at[pl.ds(e * w, w), :], sem.at[e])

    for e in range(N_EXPERTS):
        window_copy(e).start()

    lane = lax.broadcasted_iota(I32, (tt, w), 1)

    def weights(e, first_slot, lo_slot):
        pcol = pos_ref[:, e:e + 1]
        hit = (pcol - first_slot == lane) & (pcol >= lo_slot)
        return jnp.where(hit, gate_ref[:, e:e + 1], 0.0).astype(BF16)

    acc = x1_ref[...]
    for g0 in range(0, N_EXPERTS, COMB_GROUP):
        group = range(g0, g0 + COMB_GROUP)
        c = jnp.concatenate([weights(e, first_window(e), 0) for e in group], axis=1)
        for e in group:
            window_copy(e).wait()
        acc = acc + jnp.dot(c, win[g0 * w:(g0 + COMB_GROUP) * w, :], preferred_element_type=F32)
    o_ref[...] = acc

    for e in range(N_EXPERTS):
        w0 = first_window(e)
        n_win = (pn_ref[i, e] - w0 + w - 1) // w

        def extra(k, carry, e=e, w0=w0):
            nominal = w0 + k * w
            start = jnp.minimum(nominal, cap - w)
            cp = pltpu.make_async_copy(y_hbm.at[e, pl.ds(start, w), :], xwin, xsem)
            cp.start()
            cp.wait()
            o_ref[...] += jnp.dot(weights(e, start, nominal), xwin[...], preferred_element_type=F32)
            return carry

        lax.fori_loop(1, n_win, extra, 0)


def _combine(p0, pn, x1, pos_tok, gate_tok, y, cap):
    t = x1.shape[0]
    tt, w = COMB_TT, COMB_W
    row = lambda i, *_: (i, 0)
    return pl.pallas_call(
        functools.partial(_combine_kernel, cap=cap),
        grid_spec=pltpu.PrefetchScalarGridSpec(
            num_scalar_prefetch=2,
            grid=(t // tt,),
            in_specs=[pl.BlockSpec((tt, D_MODEL), row), pl.BlockSpec((tt, N_EXPERTS), row),
                      pl.BlockSpec((tt, N_EXPERTS), row), pl.BlockSpec(memory_space=pl.ANY)],
            out_specs=pl.BlockSpec((tt, D_MODEL), row),
            scratch_shapes=[pltpu.VMEM((N_EXPERTS * w, D_MODEL), BF16), pltpu.VMEM((w, D_MODEL), BF16),
                            pltpu.SemaphoreType.DMA((N_EXPERTS,)), pltpu.SemaphoreType.DMA(())]),
        out_shape=jax.ShapeDtypeStruct((t, D_MODEL), F32),
        compiler_params=_cparams(("arbitrary",)),
        name="combine",
    )(p0, pn, x1, pos_tok, gate_tok, y)


def _score_bound(width, q_gain, k_gain, max_bias):
    gmax = lambda g: jnp.max(jnp.abs(g.astype(F32)))
    bound = BOUND_SLACK * (width ** 0.5 * LOG2E * gmax(q_gain) * gmax(k_gain) + LOG2E * max_bias)
    return jnp.stack([bound, (3.0 * bound < F32_EXP_RANGE).astype(F32)]).astype(F32)


def _layer_params(l, rel_bias, attn_norm, w_in, diff_q_norm, diff_k_norm, lambda_q1, lambda_k1, lambda_q2,
                  lambda_k2, diff_subln, mla_q_latent_norm, mla_kv_latent_norm, w_uq, w_ukv, mla_q_norm,
                  mla_k_norm, w_out, ffn_norm, w_router):
    row = lambda v: v.reshape(1, -1).astype(F32)
    w = w_in[l]
    o = 3 * DIFF_W
    pad_rope = MLA_NOPE, LANES - MLA_QK
    w_kpe = jnp.pad(w[:, o + Q_LORA + KV_LORA:], ((0, 0), pad_rope))
    uq = jnp.pad(w_uq[l].reshape(Q_LORA, MLA_HEADS, MLA_QK), ((0, 0), (0, 0), (0, LANES - MLA_QK)))
    ukv = w_ukv[l].reshape(KV_LORA, MLA_HEADS, MLA_NOPE + MLA_V)
    uk = jnp.pad(ukv[:, :, :MLA_NOPE], ((0, 0), (0, 0), (0, LANES - MLA_NOPE)))
    head_gain = lambda g: jnp.pad(g.astype(F32), (0, LANES - MLA_QK)).reshape(1, LANES)
    half = MLA_ROPE // 2

    def swap_rope_halves(a):
        z = jnp.zeros_like(a)
        x1, x2 = a[..., MLA_NOPE:MLA_NOPE + half], a[..., MLA_NOPE + half:MLA_QK]
        return jnp.concatenate([z[..., :MLA_NOPE], x2, x1, z[..., MLA_QK:]], axis=-1)

    wr = jnp.pad(w_router[l].astype(F32), ((0, 0), (0, LANES - N_EXPERTS)))
    wr_hi = wr.astype(BF16)
    return dict(
        attn_norm=row(attn_norm[l]),
        w_qkv=w[:, :o].astype(BF16), w_cq=w[:, o:o + Q_LORA].astype(BF16),
        w_ckv=w[:, o + Q_LORA:o + Q_LORA + KV_LORA].astype(BF16), w_kpe=w_kpe.astype(BF16),
        w_uq=uq.reshape(Q_LORA, MLA_W).astype(BF16), w_uk=uk.reshape(KV_LORA, MLA_W).astype(BF16),
        w_uq_sw=swap_rope_halves(uq).reshape(Q_LORA, MLA_W).astype(BF16),
        mq_gain_sw=swap_rope_halves(head_gain(mla_q_norm[l])),
        w_uv=ukv[:, :, MLA_NOPE:].reshape(KV_LORA, MLA_HEADS * MLA_V).astype(BF16),
        dq_gain=row(jnp.tile(diff_q_norm[l], 2)), dk_gain=row(jnp.tile(diff_k_norm[l], 2)),
        cq_gain=row(mla_q_latent_norm[l]), ckv_gain=row(mla_kv_latent_norm[l]),
        mq_gain=head_gain(mla_q_norm[l]), mk_gain=head_gain(mla_k_norm[l]),
        subln=row(diff_subln[l]), lq1=row(lambda_q1[l]), lk1=row(lambda_k1[l]),
        lq2=row(lambda_q2[l]), lk2=row(lambda_k2[l]),
        rel_bias_t=rel_bias.astype(F32).T,
        diff_bound=_score_bound(DIFF_DK, diff_q_norm[l], diff_k_norm[l], jnp.max(jnp.abs(rel_bias))),
        mla_bound=_score_bound(MLA_QK, mla_q_norm[l], mla_k_norm[l], 0.0),
        w_o1=w_out[l][:DIFF_W].astype(BF16), w_o2=w_out[l][DIFF_W:].astype(BF16),
        ffn_norm=row(ffn_norm[l]), w_r_hi=wr_hi, w_r_lo=(wr - wr_hi.astype(F32)).astype(BF16),
    )


def _rope_tables(seq):
    half = MLA_ROPE // 2
    inv = 1.0 / (ROPE_BASE ** (jnp.arange(half, dtype=F32) / half))
    ang = jnp.arange(seq, dtype=jnp.int32).astype(F32)[:, None] * inv[None, :]
    cos, sin = jnp.cos(ang), jnp.sin(ang)
    ones = jnp.ones((seq, MLA_NOPE), F32)
    tail = LANES - MLA_QK
    cos_t = jnp.concatenate([ones, cos, cos, jnp.ones((seq, tail), F32)], axis=1)
    sin_t = jnp.concatenate([0 * ones, -sin, sin, jnp.zeros((seq, tail), F32)], axis=1)
    return cos_t, sin_t


def _layer(x, p, bias, rope, w_gate, w_up, w_down, lam_init):
    b, s, d = x.shape
    t = b * s
    cap = CAPACITY_FACTOR * t // N_EXPERTS
    x2d = x.reshape(t, d)

    qd, kd, vd, qm, km, vm = _prep(x2d, s, rope[0], rope[1], p)
    shp = lambda a: a.reshape(b, s, a.shape[-1])
    keys_t = lambda a: jnp.swapaxes(shp(a), 1, 2)
    od = _diff_attention(shp(qd), keys_t(kd), shp(vd), bias, p["rel_bias_t"], p, lam_init)
    om = _mla_attention(shp(qm), keys_t(km), shp(vm), p["mla_bound"])
    x1, h2, aff = _outproj(x2d, od.reshape(t, -1), om.reshape(t, -1), p)

    groups = LANES // N_EXPERTS
    pos_p, gate_p, csx_p = _route(aff.reshape(t // groups, LANES), cap)
    pos_tok = pos_p.reshape(t, N_EXPERTS)
    gate_tok = gate_p.reshape(t, N_EXPERTS)
    csx_tok = csx_p.reshape(t, N_EXPERTS)

    def tile_slots(tile):
        first = csx_tok[::tile]
        return first, jnp.concatenate([first[1:], jnp.full((1, N_EXPERTS), cap, I32)], axis=0)

    g0, gn = tile_slots(GATHER_TG)
    pos_t = pos_tok.T.reshape(N_EXPERTS, t // GATHER_TG, GATHER_TG)
    xg = _gather(g0.T, gn.T, pos_t, h2, cap)
    y = _ffn(xg, w_gate, w_up, w_down)
    c0, cn = tile_slots(COMB_TT)
    out = _combine(c0, cn, x1, pos_tok, gate_tok, y, cap)
    return out.reshape(b, s, d)


def kernel(x_prompt, x_sample, rel_bias, attn_norm, w_in, diff_q_norm, diff_k_norm, lambda_q1, lambda_k1, lambda_q2, lambda_k2, diff_subln, mla_q_latent_norm, mla_kv_latent_norm, w_uq, w_ukv, mla_q_norm, mla_k_norm, w_out, ffn_norm, w_router, w_gate, w_up, w_down):
    params = [_layer_params(l, rel_bias, attn_norm, w_in, diff_q_norm, diff_k_norm, lambda_q1, lambda_k1,
                            lambda_q2, lambda_k2, diff_subln, mla_q_latent_norm, mla_kv_latent_norm, w_uq,
                            w_ukv, mla_q_norm, mla_k_norm, w_out, ffn_norm, w_router) for l in range(DEPTH)]
    biases = [_bias_tiles(p["rel_bias_t"], ATT_T) for p in params]
    rope = _rope_tables(max(x_prompt.shape[1], x_sample.shape[1]))
    outs = []
    for x in (x_prompt, x_sample):
        for l in range(DEPTH):
            lam_init = 0.8 - 0.6 * math.exp(-0.3 * l)
            x = _layer(x, params[l], biases[l], rope, w_gate[l], w_up[l], w_down[l], lam_init)
        outs.append(x)
    return tuple(outs)
```

```python
import functools
import math

import jax
import jax.numpy as jnp
from jax import lax
from jax.experimental import pallas as pl
from jax.experimental.pallas import tpu as pltpu

F32 = jnp.float32
BF16 = jnp.bfloat16
I32 = jnp.int32

D_MODEL = 1024
DEPTH = 1
DIFF_HEADS = 4
DIFF_DK = 64
DIFF_DV = 128
MLA_HEADS = 8
MLA_NOPE = 64
MLA_ROPE = 32
MLA_QK = MLA_NOPE + MLA_ROPE
MLA_V = 64
Q_LORA = 256
KV_LORA = 128
ROPE_BASE = 10000.0
N_BUCKETS = 32
MAX_DISTANCE = 128
N_EXPERTS = 16
CAPACITY_FACTOR = 2
D_FF = 2816
EPS = 1e-6

LANES = 128
SUBLANES = 8
BF16_ROWS = 16
DIFF_W = DIFF_HEADS * DIFF_DV
MLA_W = MLA_HEADS * LANES
NEG_BIG = -1e30
LOG2E = math.log2(math.e)
VMEM_LIMIT = 56 * 1024 * 1024
F32_EXP_RANGE = 120.0
BOUND_SLACK = 1.02

PREP_TM = 512
ATT_T = 512
DIFF_TK = 4096
BIAS_TILES = 5
MLA_TQ = 512
MLA_TK = 4096
OUT_TM = 512
ROUTE_RB = 256
GATHER_TG = 1024
GATHER_W = 192
GATHER_NE = 4
FFN_TF = 256
COMB_TT = 512
COMB_W = 128
COMB_GROUP = 4


def _cparams(sem, vmem=VMEM_LIMIT):
    return pltpu.CompilerParams(dimension_semantics=sem, vmem_limit_bytes=vmem)


def _rms_rows(x, gain):
    ms = jnp.mean(x * x, axis=-1, keepdims=True)
    return x * lax.rsqrt(ms + EPS) * gain


def _prep_kernel(x_ref, cos_ref, sin_ref, an_ref, wqkv_ref, wcq_ref, wckv_ref, wkpe_ref,
                 wuq_ref, wuqs_ref, wuk_ref, wuv_ref, dqg_ref, dkg_ref, cqg_ref, ckvg_ref, mqg_ref, mqgs_ref,
                 mkg_ref, qd_ref, kd_ref, vd_ref, qm_ref, km_ref, vm_ref):
    tm = x_ref.shape[0]
    hb = _rms_rows(x_ref[...], an_ref[...]).astype(BF16)

    lane = lax.broadcasted_iota(I32, (tm, LANES), 1)
    lo = lane < DIFF_DK

    cq = _rms_rows(jnp.dot(hb, wcq_ref[...], preferred_element_type=F32), cqg_ref[...]).astype(BF16)
    ckv = _rms_rows(jnp.dot(hb, wckv_ref[...], preferred_element_type=F32), ckvg_ref[...]).astype(BF16)
    kpe = jnp.dot(hb, wkpe_ref[...], preferred_element_type=F32)
    q = jnp.dot(cq, wuq_ref[...], preferred_element_type=F32)
    q_sw = jnp.dot(cq, wuqs_ref[...], preferred_element_type=F32)
    k = jnp.dot(ckv, wuk_ref[...], preferred_element_type=F32)
    vm_ref[...] = jnp.dot(ckv, wuv_ref[...], preferred_element_type=F32).astype(BF16)
    qkv = jnp.dot(hb, wqkv_ref[...], preferred_element_type=F32)

    cosb = cos_ref[...]
    sinb = sin_ref[...]
    mla_scale = MLA_QK ** -0.5 * LOG2E
    first_half = (lane >= MLA_NOPE) & (lane < MLA_NOPE + MLA_ROPE // 2)
    kpe_g = kpe * mkg_ref[...]
    kpe_rot = kpe_g * cosb + sinb * jnp.where(first_half,
                                              pltpu.roll(kpe_g, LANES - MLA_ROPE // 2, 1),
                                              pltpu.roll(kpe_g, MLA_ROPE // 2, 1))
    kpe_ss = jnp.sum(kpe * kpe, axis=-1, keepdims=True)

    for hd in range(MLA_HEADS):
        sl = slice(hd * LANES, (hd + 1) * LANES)
        qb = q[:, sl]
        rq = lax.rsqrt(jnp.sum(qb * qb, axis=-1, keepdims=True) / MLA_QK + EPS) * mla_scale
        qm_ref[:, sl] = (rq * (qb * mqg_ref[...] * cosb + q_sw[:, sl] * mqgs_ref[...] * sinb)).astype(BF16)
        kb = k[:, sl]
        rk = lax.rsqrt((jnp.sum(kb * kb, axis=-1, keepdims=True) + kpe_ss) / MLA_QK + EPS)
        kn = rk * (kb * mkg_ref[...] + kpe_rot)
        km_ref[:, sl] = jnp.where(lane == MLA_QK, 1.0, kn).astype(BF16)

    diff_scale = DIFF_DK ** -0.5 * LOG2E
    for hd in range(DIFF_HEADS):
        for off, g_ref, o_ref, scale in ((0, dqg_ref, qd_ref, diff_scale), (DIFF_W, dkg_ref, kd_ref, None)):
            blk = qkv[:, off + hd * LANES: off + (hd + 1) * LANES]
            sq = blk * blk
            s_lo = jnp.sum(jnp.where(lo, sq, 0.0), axis=-1, keepdims=True)
            s_hi = jnp.sum(jnp.where(lo, 0.0, sq), axis=-1, keepdims=True)
            r = jnp.where(lo, lax.rsqrt(s_lo / DIFF_DK + EPS), lax.rsqrt(s_hi / DIFF_DK + EPS))
            y = blk * r * g_ref[...]
            if scale is not None:
                y = y * scale
            o_ref[:, hd * LANES:(hd + 1) * LANES] = y.astype(BF16)
    vd_ref[...] = qkv[:, 2 * DIFF_W:3 * DIFF_W].astype(BF16)


def _prep(x2d, seq, cos_t, sin_t, p):
    t = x2d.shape[0]
    tm = PREP_TM
    nseq = seq // tm
    row = lambda i: (i, 0)
    full = lambda i: (0, 0)
    wspec = lambda a: pl.BlockSpec(a.shape, full)
    weights = (p["attn_norm"], p["w_qkv"], p["w_cq"], p["w_ckv"], p["w_kpe"], p["w_uq"], p["w_uq_sw"], p["w_uk"],
               p["w_uv"], p["dq_gain"], p["dk_gain"], p["cq_gain"], p["ckv_gain"], p["mq_gain"], p["mq_gain_sw"],
               p["mk_gain"])
    out_w = (DIFF_W, DIFF_W, DIFF_W, MLA_W, MLA_W, MLA_HEADS * MLA_V)
    return pl.pallas_call(
        _prep_kernel,
        grid=(t // tm,),
        in_specs=[pl.BlockSpec((tm, D_MODEL), row),
                  pl.BlockSpec((tm, LANES), lambda i: (i % nseq, 0)),
                  pl.BlockSpec((tm, LANES), lambda i: (i % nseq, 0))] + [wspec(w) for w in weights],
        out_specs=[pl.BlockSpec((tm, w), row) for w in out_w],
        out_shape=[jax.ShapeDtypeStruct((t, w), BF16) for w in out_w],
        compiler_params=_cparams(("parallel",)),
        name="prep",
    )(x2d, cos_t, sin_t, *weights)


def _bias_kernel(rb_ref, o_ref):
    hd = pl.program_id(0)
    d = pl.program_id(1)
    tq, tk = o_ref.shape
    row = lax.broadcasted_iota(I32, (tq, tk), 0)
    col = lax.broadcasted_iota(I32, (tq, tk), 1)
    rel = col - row + (d - BIAS_TILES // 2) * tk
    half = N_BUCKETS // 2
    max_exact = half // 2
    n = jnp.abs(rel)
    nf = jnp.maximum(n, 1).astype(F32)
    large = max_exact + (jnp.log(nf / max_exact) / math.log(MAX_DISTANCE / max_exact)
                         * (half - max_exact)).astype(I32)
    large = jnp.minimum(large, half - 1)
    bucket = jnp.where(rel > 0, half, 0) + jnp.where(n < max_exact, n, large)
    acc = jnp.zeros((tq, tk), F32)
    for b in range(N_BUCKETS):
        acc = jnp.where(bucket == b, rb_ref[hd, b], acc)
    o_ref[...] = acc * LOG2E


def _bias_tiles(rel_bias_t, t):
    return pl.pallas_call(
        _bias_kernel,
        grid=(DIFF_HEADS, BIAS_TILES),
        in_specs=[pl.BlockSpec(memory_space=pltpu.SMEM)],
        out_specs=pl.BlockSpec((None, None, t, t), lambda h, d: (h, d, 0, 0)),
        out_shape=jax.ShapeDtypeStruct((DIFF_HEADS, BIAS_TILES, t, t), F32),
        compiler_params=_cparams(("parallel", "parallel")),
        name="bias_tiles",
    )(rel_bias_t)


def _online_update(mi, s, v, m_scr, l_scr, acc_scr):
    blocks = [s[:, i:i + LANES] for i in range(0, s.shape[1], LANES)]
    m_old = m_scr[mi]
    row_max = jnp.max(functools.reduce(jnp.maximum, blocks), axis=-1, keepdims=True)
    m_new = jnp.maximum(m_old, row_max)
    alpha = jnp.exp2(m_old - m_new)
    ps = [jnp.exp2(blk - m_new) for blk in blocks]
    l_scr[mi] = alpha * l_scr[mi] + functools.reduce(jnp.add, ps)
    p = jnp.concatenate([x.astype(BF16) for x in ps], axis=1)
    acc_scr[mi] = alpha * acc_scr[mi] + jnp.dot(p, v, preferred_element_type=F32)
    m_scr[mi] = m_new


def _fixed_ref_update(mi, s, v, l_scr, acc_scr):
    ps = [jnp.exp2(s[:, i:i + LANES]) for i in range(0, s.shape[1], LANES)]
    l_scr[mi] += functools.reduce(jnp.add, ps)
    p = jnp.concatenate([x.astype(BF16) for x in ps], axis=1)
    acc_scr[mi] += jnp.dot(p, v, preferred_element_type=F32)


def _normalised(mi, l_scr, acc_scr):
    return acc_scr[mi] / jnp.sum(l_scr[mi], axis=-1, keepdims=True)


def _diff_attn_kernel(rb_ref, bound_ref, q_ref, k_ref, v_ref, bias_ref, subln_ref, lq1_ref, lk1_ref, lq2_ref,
                      lk2_ref, o_ref, m_scr, l_scr, acc_scr, *, lam_init, tk):
    hd = pl.program_id(1)
    qi = pl.program_id(2)
    t = q_ref.shape[0]
    n = v_ref.shape[0] // t
    r = tk // t
    bound = bound_ref[0]
    fixed_ref = bound_ref[1] > 0.5

    q = q_ref[...]
    lane = lax.broadcasted_iota(I32, q.shape, 1)
    zero = jnp.zeros_like(q)
    qs = (jnp.where(lane < DIFF_DK, q, zero), jnp.where(lane < DIFF_DK, zero, q))

    l_scr[...] = jnp.zeros(l_scr.shape, F32)
    acc_scr[...] = jnp.zeros(acc_scr.shape, F32)

    lo = jnp.maximum(qi - 1, 0) // r
    hi = (jnp.minimum(qi + 2, n) + r - 1) // r
    c_left = rb_ref[hd, N_BUCKETS // 2 - 1] * LOG2E
    c_right = rb_ref[hd, N_BUCKETS - 1] * LOG2E
    n_side = bias_ref.shape[0] // 2

    def sweep(update, after_left, after_near):
        def make_step(near):
            def step(j, carry):
                off = pl.multiple_of(j * tk, tk)
                k = k_ref[:, pl.ds(off, tk)]
                v = v_ref[pl.ds(off, tk), :]
                for mi in range(2):
                    s = jnp.dot(qs[mi], k, preferred_element_type=F32)
                    if near:
                        pieces = []
                        for c in range(r):
                            d = jnp.clip(j * r + c - qi, -n_side, n_side)
                            pieces.append(s[:, c * t:(c + 1) * t] + bias_ref[d + n_side])
                        s = jnp.concatenate(pieces, axis=1)
                    update(mi, s, v)
                return carry
            return step

        lax.fori_loop(0, lo, make_step(False), 0)
        after_left()
        lax.fori_loop(lo, hi, make_step(True), 0)
        after_near()
        lax.fori_loop(hi, n // r, make_step(False), 0)

    @pl.when(fixed_ref)
    def _():
        def rescale(c):
            factor = jnp.exp2(jnp.full((1, LANES), c, F32))
            l_scr[...] = l_scr[...] * factor
            acc_scr[...] = acc_scr[...] * factor

        sweep(lambda mi, s, v: _fixed_ref_update(mi, s - bound, v, l_scr, acc_scr),
              lambda: rescale(c_left), lambda: rescale(-c_right))

    @pl.when(jnp.logical_not(fixed_ref))
    def _():
        m_scr[...] = jnp.full(m_scr.shape, NEG_BIG, F32)

        def shift(c):
            m_scr[...] = m_scr[...] + c

        sweep(lambda mi, s, v: _online_update(mi, s, v, m_scr, l_scr, acc_scr),
              lambda: shift(c_left), lambda: shift(-c_right))

    lam = (jnp.exp(jnp.sum(lq1_ref[...] * lk1_ref[...], axis=-1, keepdims=True))
           - jnp.exp(jnp.sum(lq2_ref[...] * lk2_ref[...], axis=-1, keepdims=True)) + lam_init)
    o = _normalised(0, l_scr, acc_scr) - lam * _normalised(1, l_scr, acc_scr)
    o_ref[...] = (_rms_rows(o, subln_ref[...]) * (1.0 - lam_init)).astype(BF16)


def _diff_attention(qd, kd, vd, bias_tiles, rel_bias_t, p, lam_init):
    b, s, _ = qd.shape
    t = ATT_T
    vec = lambda a: pl.BlockSpec(a.shape, lambda bi, h, i: (0, 0))
    return pl.pallas_call(
        functools.partial(_diff_attn_kernel, lam_init=lam_init, tk=DIFF_TK),
        grid=(b, DIFF_HEADS, s // t),
        in_specs=[pl.BlockSpec(memory_space=pltpu.SMEM), pl.BlockSpec(memory_space=pltpu.SMEM),
                  pl.BlockSpec((None, t, LANES), lambda bi, h, i: (bi, i, h)),
                  pl.BlockSpec((None, LANES, s), lambda bi, h, i: (bi, h, 0)),
                  pl.BlockSpec((None, s, LANES), lambda bi, h, i: (bi, 0, h)),
                  pl.BlockSpec((None, BIAS_TILES, t, t), lambda bi, h, i: (h, 0, 0, 0)),
                  vec(p["subln"]), vec(p["lq1"]), vec(p["lk1"]), vec(p["lq2"]), vec(p["lk2"])],
        out_specs=pl.BlockSpec((None, t, LANES), lambda bi, h, i: (bi, i, h)),
        out_shape=jax.ShapeDtypeStruct((b, s, DIFF_W), BF16),
        scratch_shapes=[pltpu.VMEM((2, t, LANES), F32)] * 3,
        compiler_params=_cparams(("parallel", "parallel", "arbitrary")),
        name="diff_attn",
    )(rel_bias_t, p["diff_bound"], qd, kd, vd, bias_tiles, p["subln"], p["lq1"], p["lk1"], p["lq2"], p["lk2"])


def _mla_attn_kernel(bound_ref, q_ref, k_ref, v_ref, o_ref, m_scr, l_scr, acc_scr, *, tk):
    t = q_ref.shape[0]
    n = v_ref.shape[0] // tk
    bound = bound_ref[0]
    fixed_ref = bound_ref[1] > 0.5

    l_scr[...] = jnp.zeros(l_scr.shape, F32)
    acc_scr[...] = jnp.zeros(acc_scr.shape, F32)

    def sweep(q, update):
        qs = (q[:, :LANES], q[:, LANES:])

        def step(j, carry):
            off = pl.multiple_of(j * tk, tk)
            v = v_ref[pl.ds(off, tk), :]
            for mi in range(2):
                k = k_ref[mi * LANES:(mi + 1) * LANES, pl.ds(off, tk)]
                update(mi, jnp.dot(qs[mi], k, preferred_element_type=F32), v)
            return carry

        lax.fori_loop(0, n, step, 0)

    @pl.when(fixed_ref)
    def _():
        q = q_ref[...]
        lane = lax.broadcasted_iota(I32, q.shape, 1)
        q = jnp.where(lane % LANES == MLA_QK, jnp.full(q.shape, -bound, F32).astype(BF16), q)
        sweep(q, lambda mi, s, v: _fixed_ref_update(mi, s, v, l_scr, acc_scr))

    @pl.when(jnp.logical_not(fixed_ref))
    def _():
        m_scr[...] = jnp.full(m_scr.shape, NEG_BIG, F32)
        sweep(q_ref[...], lambda mi, s, v: _online_update(mi, s, v, m_scr, l_scr, acc_scr))

    lane = lax.broadcasted_iota(I32, (t, LANES), 1)
    o = jnp.where(lane < MLA_V, _normalised(0, l_scr, acc_scr), _normalised(1, l_scr, acc_scr))
    o_ref[...] = o.astype(BF16)


def _mla_attention(qm, km, vm, bound):
    b, s, _ = qm.shape
    t = MLA_TQ
    return pl.pallas_call(
        functools.partial(_mla_attn_kernel, tk=MLA_TK),
        grid=(b, MLA_HEADS // 2, s // t),
        in_specs=[pl.BlockSpec(memory_space=pltpu.SMEM),
                  pl.BlockSpec((None, t, 2 * LANES), lambda bi, h, i: (bi, i, h)),
                  pl.BlockSpec((None, 2 * LANES, s), lambda bi, h, i: (bi, h, 0)),
                  pl.BlockSpec((None, s, LANES), lambda bi, h, i: (bi, 0, h))],
        out_specs=pl.BlockSpec((None, t, LANES), lambda bi, h, i: (bi, i, h)),
        out_shape=jax.ShapeDtypeStruct((b, s, MLA_HEADS * MLA_V), BF16),
        scratch_shapes=[pltpu.VMEM((2, t, LANES), F32)] * 3,
        compiler_params=_cparams(("parallel", "parallel", "arbitrary")),
        name="mla_attn",
    )(bound, qm, km, vm)


def _split_bf16(x):
    hi = x.astype(BF16)
    lo = (x - hi.astype(F32)).astype(BF16)
    return hi, lo


def _outproj_kernel(x_ref, od_ref, om_ref, wo1_ref, wo2_ref, fn_ref, wr_ref,
                    x1_ref, h2_ref, aff_ref):
    x1 = (x_ref[...]
          + jnp.dot(od_ref[...], wo1_ref[...], preferred_element_type=F32)
          + jnp.dot(om_ref[...], wo2_ref[...], preferred_element_type=F32))
    x1_ref[...] = x1
    h = _rms_rows(x1, fn_ref[...])
    hh, hl = _split_bf16(h)
    h2_ref[...] = hh
    both = jnp.dot(hh, wr_ref[...], preferred_element_type=F32)
    logits = both[:, :LANES] + both[:, LANES:] + jnp.dot(hl, wr_ref[:, :LANES], preferred_element_type=F32)
    lane = lax.broadcasted_iota(I32, logits.shape, 1)
    logits = jnp.where(lane < N_EXPERTS, logits, NEG_BIG)
    e = jnp.exp(logits - jnp.max(logits, axis=-1, keepdims=True))
    aff = e / jnp.sum(e, axis=-1, keepdims=True)
    aff_ref[...] = aff[:, :N_EXPERTS]


def _outproj(x2d, od, om, p):
    t = x2d.shape[0]
    tm = OUT_TM
    row = lambda i: (i, 0)
    wspec = lambda a: pl.BlockSpec(a.shape, lambda i: (0, 0))
    weights = (p["w_o1"], p["w_o2"], p["ffn_norm"], p["w_r"])
    return pl.pallas_call(
        _outproj_kernel,
        grid=(t // tm,),
        in_specs=[pl.BlockSpec((tm, D_MODEL), row), pl.BlockSpec((tm, DIFF_W), row),
                  pl.BlockSpec((tm, MLA_HEADS * MLA_V), row)] + [wspec(w) for w in weights],
        out_specs=[pl.BlockSpec((tm, D_MODEL), row), pl.BlockSpec((tm, D_MODEL), row),
                   pl.BlockSpec((tm, N_EXPERTS), row)],
        out_shape=[jax.ShapeDtypeStruct((t, D_MODEL), F32), jax.ShapeDtypeStruct((t, D_MODEL), BF16),
                   jax.ShapeDtypeStruct((t, N_EXPERTS), F32)],
        compiler_params=_cparams(("parallel",)),
        name="outproj",
    )(x2d, od, om, *weights)


def _route_kernel(aff_ref, pos_ref, gate_ref, csx_ref, *, cap):
    rows = aff_ref.shape[0]
    groups = LANES // N_EXPERTS
    aff = aff_ref[...]
    bits = pltpu.bitcast(aff, I32)

    def expert_total(v):
        for sh in (N_EXPERTS, 2 * N_EXPERTS, 4 * N_EXPERTS):
            v = v + pltpu.roll(v, sh, 1)
        return v

    def count(mask):
        return expert_total(jnp.sum(mask.astype(I32), axis=0, keepdims=True))

    def search(i, thr):
        cand = thr | jnp.left_shift(jnp.int32(1), 30 - i)
        return jnp.where(count(bits >= cand) >= cap, cand, thr)

    thr = lax.fori_loop(0, 31, search, jnp.zeros((1, LANES), I32))

    a = lax.broadcasted_iota(I32, (LANES, LANES), 0)
    b = lax.broadcasted_iota(I32, (LANES, LANES), 1)
    same = (a % N_EXPERTS) == (b % N_EXPERTS)
    q_all = same.astype(BF16)
    q_before = (same & (a // N_EXPERTS < b // N_EXPERTS)).astype(BF16)
    rb = ROUTE_RB
    ra = lax.broadcasted_iota(I32, (rb, rb), 0)
    ca = lax.broadcasted_iota(I32, (rb, rb), 1)
    tri = (ca < ra).astype(BF16)

    def prefix(mask):
        mb = mask.astype(BF16)
        outs = []
        offset = jnp.zeros((1, LANES), F32)
        for r0 in range(0, rows, rb):
            blk = mb[r0:r0 + rb]
            tot = jnp.dot(blk, q_all, preferred_element_type=F32)
            within = jnp.dot(blk, q_before, preferred_element_type=F32)
            above = jnp.dot(tri, tot.astype(BF16), preferred_element_type=F32)
            outs.append(above + within + offset)
            offset = offset + jnp.sum(tot, axis=0, keepdims=True)
        return jnp.concatenate(outs, axis=0)

    gt = bits > thr
    eq = bits == thr
    need = (cap - count(gt)).astype(F32)
    sel = gt | (eq & (prefix(eq) < need))
    csx = prefix(sel)
    pos_ref[...] = jnp.where(sel, csx.astype(I32), -1)
    gate_ref[...] = jnp.where(sel, aff, 0.0)
    csx_ref[...] = csx.astype(I32)


def _route(aff_packed, cap):
    rows = aff_packed.shape[0]
    vm = pl.BlockSpec(memory_space=pltpu.VMEM)
    return pl.pallas_call(
        functools.partial(_route_kernel, cap=cap),
        in_specs=[vm],
        out_specs=[vm, vm, vm],
        out_shape=[jax.ShapeDtypeStruct((rows, LANES), I32), jax.ShapeDtypeStruct((rows, LANES), F32),
                   jax.ShapeDtypeStruct((rows, LANES), I32)],
        compiler_params=pltpu.CompilerParams(vmem_limit_bytes=VMEM_LIMIT),
        name="route",
    )(aff_packed)


def _window_start(first_slot, align, w, cap):
    return jnp.minimum((first_slot // align) * align, cap - w)


def _gather_kernel(p0_ref, pn_ref, pos_ref, h2_ref, xg_ref, *, cap, w):
    ne = xg_ref.shape[0]
    e0 = pl.program_id(0) * ne
    i = pl.program_id(1)
    tg = h2_ref.shape[0]

    @pl.when(i == 0)
    def _():
        xg_ref[...] = jnp.zeros(xg_ref.shape, BF16)

    row_id = lax.broadcasted_iota(I32, (w, tg), 0)
    first = [pl.multiple_of(_window_start(p0_ref[e0 + j, i], BF16_ROWS, w, cap), BF16_ROWS) for j in range(ne)]
    prows = [pos_ref[j, pl.ds(i, 1), :] for j in range(ne)]

    onehot = jnp.concatenate([(prows[j] == row_id + first[j]).astype(BF16) for j in range(ne)], axis=0)
    rows = jnp.dot(onehot, h2_ref[...], preferred_element_type=F32).astype(BF16)
    for j in range(ne):
        xg_ref[j, pl.ds(first[j], w), :] += rows[j * w:(j + 1) * w]

    for j in range(ne):
        n_win = (pn_ref[e0 + j, i] - first[j] + w - 1) // w

        def window(k, carry, j=j):
            nominal = first[j] + k * w
            start = pl.multiple_of(jnp.minimum(nominal, cap - w), BF16_ROWS)
            hit = (prows[j] == row_id + start) & (prows[j] >= nominal)
            more = jnp.dot(hit.astype(BF16), h2_ref[...], preferred_element_type=F32).astype(BF16)
            xg_ref[j, pl.ds(start, w), :] += more
            return carry

        lax.fori_loop(1, n_win, window, 0)


def _gather(p0, pn, pos_t, h2, cap):
    t = h2.shape[0]
    tg, ne = GATHER_TG, GATHER_NE
    return pl.pallas_call(
        functools.partial(_gather_kernel, cap=cap, w=GATHER_W),
        grid_spec=pltpu.PrefetchScalarGridSpec(
            num_scalar_prefetch=2,
            grid=(N_EXPERTS // ne, t // tg),
            in_specs=[pl.BlockSpec((ne, t // tg, tg), lambda e, i, *_: (e, 0, 0)),
                      pl.BlockSpec((tg, D_MODEL), lambda e, i, *_: (i, 0))],
            out_specs=pl.BlockSpec((ne, cap, D_MODEL), lambda e, i, *_: (e, 0, 0))),
        out_shape=jax.ShapeDtypeStruct((N_EXPERTS, cap, D_MODEL), BF16),
        compiler_params=_cparams(("parallel", "arbitrary")),
        name="gather",
    )(p0, pn, pos_t, h2)


def _ffn_kernel(xg_ref, wg_ref, wu_ref, wd_ref, y_ref, acc_scr):
    f = pl.program_id(1)

    @pl.when((pl.program_id(0) == 0) & (f == 0))
    def _():
        acc_scr[...] = jnp.zeros(acc_scr.shape, F32)

    xg = xg_ref[...]
    a = jnp.dot(xg, wg_ref[...].astype(BF16), preferred_element_type=F32)
    b = jnp.dot(xg, wu_ref[...].astype(BF16), preferred_element_type=F32)
    act = (a * jax.nn.sigmoid(a) * b).astype(BF16)
    part = jnp.dot(act, wd_ref[...].astype(BF16), preferred_element_type=F32)
    acc_scr[...] = jnp.where(f == 0, part, acc_scr[...] + part)

    @pl.when(f == pl.num_programs(1) - 1)
    def _():
        y_ref[...] = acc_scr[...].astype(BF16)


def _ffn(xg, w_gate, w_up, w_down):
    _, cap, _ = xg.shape
    tf = FFN_TF
    return pl.pallas_call(
        _ffn_kernel,
        grid=(N_EXPERTS, D_FF // tf),
        in_specs=[pl.BlockSpec((None, cap, D_MODEL), lambda e, f: (e, 0, 0)),
                  pl.BlockSpec((None, D_MODEL, tf), lambda e, f: (e, 0, f)),
                  pl.BlockSpec((None, D_MODEL, tf), lambda e, f: (e, 0, f)),
                  pl.BlockSpec((None, tf, D_MODEL), lambda e, f: (e, f, 0))],
        out_specs=pl.BlockSpec((None, cap, D_MODEL), lambda e, f: (e, 0, 0)),
        out_shape=jax.ShapeDtypeStruct((N_EXPERTS, cap, D_MODEL), BF16),
        scratch_shapes=[pltpu.VMEM((cap, D_MODEL), F32)],
        compiler_params=_cparams(("arbitrary", "arbitrary")),
        name="ffn",
    )(xg, w_gate, w_up, w_down)


def _combine_kernel(p0_ref, pn_ref, x1_ref, pos_ref, gate_ref, y_hbm, o_ref, win, xwin, sem, xsem, *, cap):
    i = pl.program_id(0)
    tt = x1_ref.shape[0]
    w = xwin.shape[0]

    def first_window(e):
        return _window_start(p0_ref[i, e], BF16_ROWS, w, cap)

    def window_copy(e):
        return pltpu.make_async_copy(y_hbm.at[e, pl.ds(first_window(e), w), :],
                                     win.at[pl.ds(e * w, w), :], sem.at[e])

    for e in range(N_EXPERTS):
        window_copy(e).start()

    lane = lax.broadcasted_iota(I32, (tt, w), 1)

    def weights(e, first_slot, lo_slot):
        pcol = pos_ref[:, e:e + 1]
        hit = (pcol - first_slot == lane) & (pcol >= lo_slot)
        return jnp.where(hit, gate_ref[:, e:e + 1], 0.0).astype(BF16)

    acc = x1_ref[...]
    for g0 in range(0, N_EXPERTS, COMB_GROUP):
        group = range(g0, g0 + COMB_GROUP)
        c = jnp.concatenate([weights(e, first_window(e), 0) for e in group], axis=1)
        for e in group:
            window_copy(e).wait()
        acc = acc + jnp.dot(c, win[g0 * w:(g0 + COMB_GROUP) * w, :], preferred_element_type=F32)
    o_ref[...] = acc

    for e in range(N_EXPERTS):
        w0 = first_window(e)
        n_win = (pn_ref[i, e] - w0 + w - 1) // w

        def extra(k, carry, e=e, w0=w0):
            nominal = w0 + k * w
            start = jnp.minimum(nominal, cap - w)
            cp = pltpu.make_async_copy(y_hbm.at[e, pl.ds(start, w), :], xwin, xsem)
            cp.start()
            cp.wait()
            o_ref[...] += jnp.dot(weights(e, start, nominal), xwin[...], preferred_element_type=F32)
            return carry

        lax.fori_loop(1, n_win, extra, 0)


def _combine(p0, pn, x1, pos_tok, gate_tok, y, cap):
    t = x1.shape[0]
    tt, w = COMB_TT, COMB_W
    row = lambda i, *_: (i, 0)
    return pl.pallas_call(
        functools.partial(_combine_kernel, cap=cap),
        grid_spec=pltpu.PrefetchScalarGridSpec(
            num_scalar_prefetch=2,
            grid=(t // tt,),
            in_specs=[pl.BlockSpec((tt, D_MODEL), row), pl.BlockSpec((tt, N_EXPERTS), row),
                      pl.BlockSpec((tt, N_EXPERTS), row), pl.BlockSpec(memory_space=pl.ANY)],
            out_specs=pl.BlockSpec((tt, D_MODEL), row),
            scratch_shapes=[pltpu.VMEM((N_EXPERTS * w, D_MODEL), BF16), pltpu.VMEM((w, D_MODEL), BF16),
                            pltpu.SemaphoreType.DMA((N_EXPERTS,)), pltpu.SemaphoreType.DMA(())]),
        out_shape=jax.ShapeDtypeStruct((t, D_MODEL), F32),
        compiler_params=_cparams(("arbitrary",)),
        name="combine",
    )(p0, pn, x1, pos_tok, gate_tok, y)


def _score_bound(width, q_gain, k_gain, max_bias):
    gmax = lambda g: jnp.max(jnp.abs(g.astype(F32)))
    bound = BOUND_SLACK * (width ** 0.5 * LOG2E * gmax(q_gain) * gmax(k_gain) + LOG2E * max_bias)
    return jnp.stack([bound, (3.0 * bound < F32_EXP_RANGE).astype(F32)]).astype(F32)


def _layer_params(l, rel_bias, attn_norm, w_in, diff_q_norm, diff_k_norm, lambda_q1, lambda_k1, lambda_q2,
                  lambda_k2, diff_subln, mla_q_latent_norm, mla_kv_latent_norm, w_uq, w_ukv, mla_q_norm,
                  mla_k_norm, w_out, ffn_norm, w_router):
    row = lambda v: v.reshape(1, -1).astype(F32)
    w = w_in[l]
    o = 3 * DIFF_W
    pad_rope = MLA_NOPE, LANES - MLA_QK
    w_kpe = jnp.pad(w[:, o + Q_LORA + KV_LORA:], ((0, 0), pad_rope))
    uq = jnp.pad(w_uq[l].reshape(Q_LORA, MLA_HEADS, MLA_QK), ((0, 0), (0, 0), (0, LANES - MLA_QK)))
    ukv = w_ukv[l].reshape(KV_LORA, MLA_HEADS, MLA_NOPE + MLA_V)
    uk = jnp.pad(ukv[:, :, :MLA_NOPE], ((0, 0), (0, 0), (0, LANES - MLA_NOPE)))
    head_gain = lambda g: jnp.pad(g.astype(F32), (0, LANES - MLA_QK)).reshape(1, LANES)
    half = MLA_ROPE // 2

    def swap_rope_halves(a):
        z = jnp.zeros_like(a)
        x1, x2 = a[..., MLA_NOPE:MLA_NOPE + half], a[..., MLA_NOPE + half:MLA_QK]
        return jnp.concatenate([z[..., :MLA_NOPE], x2, x1, z[..., MLA_QK:]], axis=-1)

    wr = jnp.pad(w_router[l].astype(F32), ((0, 0), (0, LANES - N_EXPERTS)))
    wr_hi = wr.astype(BF16)
    return dict(
        attn_norm=row(attn_norm[l]),
        w_qkv=w[:, :o].astype(BF16), w_cq=w[:, o:o + Q_LORA].astype(BF16),
        w_ckv=w[:, o + Q_LORA:o + Q_LORA + KV_LORA].astype(BF16), w_kpe=w_kpe.astype(BF16),
        w_uq=uq.reshape(Q_LORA, MLA_W).astype(BF16), w_uk=uk.reshape(KV_LORA, MLA_W).astype(BF16),
        w_uq_sw=swap_rope_halves(uq).reshape(Q_LORA, MLA_W).astype(BF16),
        mq_gain_sw=swap_rope_halves(head_gain(mla_q_norm[l])),
        w_uv=ukv[:, :, MLA_NOPE:].reshape(KV_LORA, MLA_HEADS * MLA_V).astype(BF16),
        dq_gain=row(jnp.tile(diff_q_norm[l], 2)), dk_gain=row(jnp.tile(diff_k_norm[l], 2)),
        cq_gain=row(mla_q_latent_norm[l]), ckv_gain=row(mla_kv_latent_norm[l]),
        mq_gain=head_gain(mla_q_norm[l]), mk_gain=head_gain(mla_k_norm[l]),
        subln=row(diff_subln[l]), lq1=row(lambda_q1[l]), lk1=row(lambda_k1[l]),
        lq2=row(lambda_q2[l]), lk2=row(lambda_k2[l]),
        rel_bias_t=rel_bias.astype(F32).T,
        diff_bound=_score_bound(DIFF_DK, diff_q_norm[l], diff_k_norm[l], jnp.max(jnp.abs(rel_bias))),
        mla_bound=_score_bound(MLA_QK, mla_q_norm[l], mla_k_norm[l], 0.0),
        w_o1=w_out[l][:DIFF_W].astype(BF16), w_o2=w_out[l][DIFF_W:].astype(BF16),
        ffn_norm=row(ffn_norm[l]),
        w_r=jnp.concatenate([wr_hi, (wr - wr_hi.astype(F32)).astype(BF16)], axis=1),
    )


def _rope_tables(seq):
    half = MLA_ROPE // 2
    inv = 1.0 / (ROPE_BASE ** (jnp.arange(half, dtype=F32) / half))
    ang = jnp.arange(seq, dtype=jnp.int32).astype(F32)[:, None] * inv[None, :]
    cos, sin = jnp.cos(ang), jnp.sin(ang)
    ones = jnp.ones((seq, MLA_NOPE), F32)
    tail = LANES - MLA_QK
    cos_t = jnp.concatenate([ones, cos, cos, jnp.ones((seq, tail), F32)], axis=1)
    sin_t = jnp.concatenate([0 * ones, -sin, sin, jnp.zeros((seq, tail), F32)], axis=1)
    return cos_t, sin_t


def _layer(x, p, bias, rope, w_gate, w_up, w_down, lam_init):
    b, s, d = x.shape
    t = b * s
    cap = CAPACITY_FACTOR * t // N_EXPERTS
    x2d = x.reshape(t, d)

    qd, kd, vd, qm, km, vm = _prep(x2d, s, rope[0], rope[1], p)
    shp = lambda a: a.reshape(b, s, a.shape[-1])
    keys_t = lambda a: jnp.swapaxes(shp(a), 1, 2)
    od = _diff_attention(shp(qd), keys_t(kd), shp(vd), bias, p["rel_bias_t"], p, lam_init)
    om = _mla_attention(shp(qm), keys_t(km), shp(vm), p["mla_bound"])
    x1, h2, aff = _outproj(x2d, od.reshape(t, -1), om.reshape(t, -1), p)

    groups = LANES // N_EXPERTS
    pos_p, gate_p, csx_p = _route(aff.reshape(t // groups, LANES), cap)
    pos_tok = pos_p.reshape(t, N_EXPERTS)
    gate_tok = gate_p.reshape(t, N_EXPERTS)
    csx_tok = csx_p.reshape(t, N_EXPERTS)

    def tile_slots(tile):
        first = csx_tok[::tile]
        return first, jnp.concatenate([first[1:], jnp.full((1, N_EXPERTS), cap, I32)], axis=0)

    g0, gn = tile_slots(GATHER_TG)
    pos_t = pos_tok.T.reshape(N_EXPERTS, t // GATHER_TG, GATHER_TG)
    xg = _gather(g0.T, gn.T, pos_t, h2, cap)
    y = _ffn(xg, w_gate, w_up, w_down)
    c0, cn = tile_slots(COMB_TT)
    out = _combine(c0, cn, x1, pos_tok, gate_tok, y, cap)
    return out.reshape(b, s, d)


def kernel(x_prompt, x_sample, rel_bias, attn_norm, w_in, diff_q_norm, diff_k_norm, lambda_q1, lambda_k1, lambda_q2, lambda_k2, diff_subln, mla_q_latent_norm, mla_kv_latent_norm, w_uq, w_ukv, mla_q_norm, mla_k_norm, w_out, ffn_norm, w_router, w_gate, w_up, w_down):
    params = [_layer_params(l, rel_bias, attn_norm, w_in, diff_q_norm, diff_k_norm, lambda_q1, lambda_k1,
                            lambda_q2, lambda_k2, diff_subln, mla_q_latent_norm, mla_kv_latent_norm, w_uq,
                            w_ukv, mla_q_norm, mla_k_norm, w_out, ffn_norm, w_router) for l in range(DEPTH)]
    biases = [_bias_tiles(p["rel_bias_t"], ATT_T) for p in params]
    rope = _rope_tables(max(x_prompt.shape[1], x_sample.shape[1]))
    outs = []
    for x in (x_prompt, x_sample):
        for l in range(DEPTH):
            lam_init = 0.8 - 0.6 * math.exp(-0.3 * l)
            x = _layer(x, params[l], biases[l], rope, w_gate[l], w_up[l], w_down[l], lam_init)
        outs.append(x)
    return tuple(outs)
```

```python
import functools
import math

import jax
import jax.numpy as jnp
from jax import lax
from jax.experimental import pallas as pl
from jax.experimental.pallas import tpu as pltpu

F32 = jnp.float32
BF16 = jnp.bfloat16
I32 = jnp.int32

D_MODEL = 1024
DEPTH = 1
DIFF_HEADS = 4
DIFF_DK = 64
DIFF_DV = 128
MLA_HEADS = 8
MLA_NOPE = 64
MLA_ROPE = 32
MLA_QK = MLA_NOPE + MLA_ROPE
MLA_V = 64
Q_LORA = 256
KV_LORA = 128
ROPE_BASE = 10000.0
N_BUCKETS = 32
MAX_DISTANCE = 128
N_EXPERTS = 16
CAPACITY_FACTOR = 2
D_FF = 2816
EPS = 1e-6

LANES = 128
SUBLANES = 8
BF16_ROWS = 16
DIFF_W = DIFF_HEADS * DIFF_DV
MLA_W = MLA_HEADS * LANES
NEG_BIG = -1e30
LOG2E = math.log2(math.e)
VMEM_LIMIT = 56 * 1024 * 1024
F32_EXP_RANGE = 120.0
BOUND_SLACK = 1.02

PREP_TM = 512
ATT_T = 512
DIFF_TK = 4096
BIAS_TILES = 5
MLA_TQ = 512
MLA_TK = 4096
OUT_TM = 512
ROUTE_RB = 256
GATHER_TG = 1024
GATHER_W = 192
GATHER_NE = 4
FFN_TF = 256
COMB_TT = 512
COMB_W = 128
COMB_GROUP = 4


def _cparams(sem, vmem=VMEM_LIMIT):
    return pltpu.CompilerParams(dimension_semantics=sem, vmem_limit_bytes=vmem)


def _rms_rows(x, gain):
    ms = jnp.mean(x * x, axis=-1, keepdims=True)
    return x * lax.rsqrt(ms + EPS) * gain


def _prep_kernel(x_ref, cos_ref, sin_ref, an_ref, wqkv_ref, wcq_ref, wckv_ref, wkpe_ref,
                 wuq_ref, wuqs_ref, wuk_ref, wuv_ref, dqg_ref, dkg_ref, cqg_ref, ckvg_ref, mqg_ref, mqgs_ref,
                 mkg_ref, qd_ref, kd_ref, vd_ref, qm_ref, km_ref, vm_ref):
    tm = x_ref.shape[0]
    hb = _rms_rows(x_ref[...], an_ref[...]).astype(BF16)

    lane = lax.broadcasted_iota(I32, (tm, LANES), 1)
    lo = lane < DIFF_DK

    cq = _rms_rows(jnp.dot(hb, wcq_ref[...], preferred_element_type=F32), cqg_ref[...]).astype(BF16)
    ckv = _rms_rows(jnp.dot(hb, wckv_ref[...], preferred_element_type=F32), ckvg_ref[...]).astype(BF16)
    kpe = jnp.dot(hb, wkpe_ref[...], preferred_element_type=F32)
    q = jnp.dot(cq, wuq_ref[...], preferred_element_type=F32)
    q_sw = jnp.dot(cq, wuqs_ref[...], preferred_element_type=F32)
    k = jnp.dot(ckv, wuk_ref[...], preferred_element_type=F32)
    vm_ref[...] = jnp.dot(ckv, wuv_ref[...], preferred_element_type=F32).astype(BF16)
    qkv = jnp.dot(hb, wqkv_ref[...], preferred_element_type=F32)

    cosb = cos_ref[...]
    sinb = sin_ref[...]
    mla_scale = MLA_QK ** -0.5 * LOG2E
    first_half = (lane >= MLA_NOPE) & (lane < MLA_NOPE + MLA_ROPE // 2)
    kpe_g = kpe * mkg_ref[...]
    kpe_rot = kpe_g * cosb + sinb * jnp.where(first_half,
                                              pltpu.roll(kpe_g, LANES - MLA_ROPE // 2, 1),
                                              pltpu.roll(kpe_g, MLA_ROPE // 2, 1))
    kpe_ss = jnp.sum(kpe * kpe, axis=-1, keepdims=True)

    for hd in range(MLA_HEADS):
        sl = slice(hd * LANES, (hd + 1) * LANES)
        qb = q[:, sl]
        rq = lax.rsqrt(jnp.sum(qb * qb, axis=-1, keepdims=True) / MLA_QK + EPS) * mla_scale
        qm_ref[:, sl] = (rq * (qb * mqg_ref[...] * cosb + q_sw[:, sl] * mqgs_ref[...] * sinb)).astype(BF16)
        kb = k[:, sl]
        rk = lax.rsqrt((jnp.sum(kb * kb, axis=-1, keepdims=True) + kpe_ss) / MLA_QK + EPS)
        kn = rk * (kb * mkg_ref[...] + kpe_rot)
        km_ref[:, sl] = jnp.where(lane == MLA_QK, 1.0, kn).astype(BF16)

    diff_scale = DIFF_DK ** -0.5 * LOG2E
    for hd in range(DIFF_HEADS):
        for off, g_ref, o_ref, scale in ((0, dqg_ref, qd_ref, diff_scale), (DIFF_W, dkg_ref, kd_ref, None)):
            blk = qkv[:, off + hd * LANES: off + (hd + 1) * LANES]
            sq = blk * blk
            s_lo = jnp.sum(jnp.where(lo, sq, 0.0), axis=-1, keepdims=True)
            s_hi = jnp.sum(jnp.where(lo, 0.0, sq), axis=-1, keepdims=True)
            r = jnp.where(lo, lax.rsqrt(s_lo / DIFF_DK + EPS), lax.rsqrt(s_hi / DIFF_DK + EPS))
            y = blk * r * g_ref[...]
            if scale is not None:
                y = y * scale
            o_ref[:, hd * LANES:(hd + 1) * LANES] = y.astype(BF16)
    vd_ref[...] = qkv[:, 2 * DIFF_W:3 * DIFF_W].astype(BF16)


def _prep(x2d, seq, cos_t, sin_t, p):
    t = x2d.shape[0]
    tm = PREP_TM
    nseq = seq // tm
    row = lambda i: (i, 0)
    full = lambda i: (0, 0)
    wspec = lambda a: pl.BlockSpec(a.shape, full)
    weights = (p["attn_norm"], p["w_qkv"], p["w_cq"], p["w_ckv"], p["w_kpe"], p["w_uq"], p["w_uq_sw"], p["w_uk"],
               p["w_uv"], p["dq_gain"], p["dk_gain"], p["cq_gain"], p["ckv_gain"], p["mq_gain"], p["mq_gain_sw"],
               p["mk_gain"])
    out_w = (DIFF_W, DIFF_W, DIFF_W, MLA_W, MLA_W, MLA_HEADS * MLA_V)
    return pl.pallas_call(
        _prep_kernel,
        grid=(t // tm,),
        in_specs=[pl.BlockSpec((tm, D_MODEL), row),
                  pl.BlockSpec((tm, LANES), lambda i: (i % nseq, 0)),
                  pl.BlockSpec((tm, LANES), lambda i: (i % nseq, 0))] + [wspec(w) for w in weights],
        out_specs=[pl.BlockSpec((tm, w), row) for w in out_w],
        out_shape=[jax.ShapeDtypeStruct((t, w), BF16) for w in out_w],
        compiler_params=_cparams(("parallel",)),
        name="prep",
    )(x2d, cos_t, sin_t, *weights)


def _bias_kernel(rb_ref, o_ref):
    hd = pl.program_id(0)
    d = pl.program_id(1)
    tq, tk = o_ref.shape
    row = lax.broadcasted_iota(I32, (tq, tk), 0)
    col = lax.broadcasted_iota(I32, (tq, tk), 1)
    rel = col - row + (d - BIAS_TILES // 2) * tk
    half = N_BUCKETS // 2
    max_exact = half // 2
    n = jnp.abs(rel)
    nf = jnp.maximum(n, 1).astype(F32)
    large = max_exact + (jnp.log(nf / max_exact) / math.log(MAX_DISTANCE / max_exact)
                         * (half - max_exact)).astype(I32)
    large = jnp.minimum(large, half - 1)
    bucket = jnp.where(rel > 0, half, 0) + jnp.where(n < max_exact, n, large)
    acc = jnp.zeros((tq, tk), F32)
    for b in range(N_BUCKETS):
        acc = jnp.where(bucket == b, rb_ref[hd, b], acc)
    o_ref[...] = acc * LOG2E


def _bias_tiles(rel_bias_t, t):
    return pl.pallas_call(
        _bias_kernel,
        grid=(DIFF_HEADS, BIAS_TILES),
        in_specs=[pl.BlockSpec(memory_space=pltpu.SMEM)],
        out_specs=pl.BlockSpec((None, None, t, t), lambda h, d: (h, d, 0, 0)),
        out_shape=jax.ShapeDtypeStruct((DIFF_HEADS, BIAS_TILES, t, t), F32),
        compiler_params=_cparams(("parallel", "parallel")),
        name="bias_tiles",
    )(rel_bias_t)


def _online_update(mi, s, v, m_scr, l_scr, acc_scr):
    blocks = [s[:, i:i + LANES] for i in range(0, s.shape[1], LANES)]
    m_old = m_scr[mi]
    row_max = jnp.max(functools.reduce(jnp.maximum, blocks), axis=-1, keepdims=True)
    m_new = jnp.maximum(m_old, row_max)
    alpha = jnp.exp2(m_old - m_new)
    ps = [jnp.exp2(blk - m_new) for blk in blocks]
    l_scr[mi] = alpha * l_scr[mi] + functools.reduce(jnp.add, ps)
    p = jnp.concatenate([x.astype(BF16) for x in ps], axis=1)
    acc_scr[mi] = alpha * acc_scr[mi] + jnp.dot(p, v, preferred_element_type=F32)
    m_scr[mi] = m_new


def _fixed_ref_update(mi, s, v, l_scr, acc_scr):
    ps = [jnp.exp2(s[:, i:i + LANES]) for i in range(0, s.shape[1], LANES)]
    l_scr[mi] += functools.reduce(jnp.add, ps)
    p = jnp.concatenate([x.astype(BF16) for x in ps], axis=1)
    acc_scr[mi] += jnp.dot(p, v, preferred_element_type=F32)


def _normalised(mi, l_scr, acc_scr):
    return acc_scr[mi] / jnp.sum(l_scr[mi], axis=-1, keepdims=True)


def _diff_attn_kernel(rb_ref, bound_ref, q_ref, k_ref, v_ref, bias_ref, subln_ref, lq1_ref, lk1_ref, lq2_ref,
                      lk2_ref, o_ref, m_scr, l_scr, acc_scr, *, lam_init, tk):
    hd = pl.program_id(1)
    qi = pl.program_id(2)
    t = q_ref.shape[0]
    n = v_ref.shape[0] // t
    r = tk // t
    bound = bound_ref[0]
    fixed_ref = bound_ref[1] > 0.5

    q = q_ref[...]
    lane = lax.broadcasted_iota(I32, q.shape, 1)
    zero = jnp.zeros_like(q)
    qs = (jnp.where(lane < DIFF_DK, q, zero), jnp.where(lane < DIFF_DK, zero, q))

    l_scr[...] = jnp.zeros(l_scr.shape, F32)
    acc_scr[...] = jnp.zeros(acc_scr.shape, F32)

    lo = jnp.maximum(qi - 1, 0) // r
    hi = (jnp.minimum(qi + 2, n) + r - 1) // r
    c_left = rb_ref[hd, N_BUCKETS // 2 - 1] * LOG2E
    c_right = rb_ref[hd, N_BUCKETS - 1] * LOG2E
    n_side = bias_ref.shape[0] // 2
    LEFT, BAND, RIGHT = range(3)

    def sweep(update, after_left=lambda: None, after_band=lambda: None):
        def make_step(phase):
            def step(j, carry):
                off = pl.multiple_of(j * tk, tk)
                k = k_ref[:, pl.ds(off, tk)]
                v = v_ref[pl.ds(off, tk), :]
                for mi in range(2):
                    s = jnp.dot(qs[mi], k, preferred_element_type=F32)
                    if phase == BAND:
                        pieces = []
                        for c in range(r):
                            d = jnp.clip(j * r + c - qi, -n_side, n_side)
                            pieces.append(s[:, c * t:(c + 1) * t] + bias_ref[d + n_side])
                        s = jnp.concatenate(pieces, axis=1)
                    update(phase, mi, s, v)
                return carry
            return step

        lax.fori_loop(0, lo, make_step(LEFT), 0)
        after_left()
        lax.fori_loop(lo, hi, make_step(BAND), 0)
        after_band()
        lax.fori_loop(hi, n // r, make_step(RIGHT), 0)

    @pl.when(fixed_ref)
    def _():
        ref = (bound, bound + c_left, bound + c_left - c_right)
        sweep(lambda phase, mi, s, v: _fixed_ref_update(mi, s - ref[phase], v, l_scr, acc_scr))

    @pl.when(jnp.logical_not(fixed_ref))
    def _():
        m_scr[...] = jnp.full(m_scr.shape, NEG_BIG, F32)

        def shift(c):
            m_scr[...] = m_scr[...] + c

        sweep(lambda phase, mi, s, v: _online_update(mi, s, v, m_scr, l_scr, acc_scr),
              lambda: shift(c_left), lambda: shift(-c_right))

    lam = (jnp.exp(jnp.sum(lq1_ref[...] * lk1_ref[...], axis=-1, keepdims=True))
           - jnp.exp(jnp.sum(lq2_ref[...] * lk2_ref[...], axis=-1, keepdims=True)) + lam_init)
    o = _normalised(0, l_scr, acc_scr) - lam * _normalised(1, l_scr, acc_scr)
    o_ref[...] = (_rms_rows(o, subln_ref[...]) * (1.0 - lam_init)).astype(BF16)


def _diff_attention(qd, kd, vd, bias_tiles, rel_bias_t, p, lam_init):
    b, s, _ = qd.shape
    t = ATT_T
    vec = lambda a: pl.BlockSpec(a.shape, lambda bi, h, i: (0, 0))
    return pl.pallas_call(
        functools.partial(_diff_attn_kernel, lam_init=lam_init, tk=DIFF_TK),
        grid=(b, DIFF_HEADS, s // t),
        in_specs=[pl.BlockSpec(memory_space=pltpu.SMEM), pl.BlockSpec(memory_space=pltpu.SMEM),
                  pl.BlockSpec((None, t, LANES), lambda bi, h, i: (bi, i, h)),
                  pl.BlockSpec((None, LANES, s), lambda bi, h, i: (bi, h, 0)),
                  pl.BlockSpec((None, s, LANES), lambda bi, h, i: (bi, 0, h)),
                  pl.BlockSpec((None, BIAS_TILES, t, t), lambda bi, h, i: (h, 0, 0, 0)),
                  vec(p["subln"]), vec(p["lq1"]), vec(p["lk1"]), vec(p["lq2"]), vec(p["lk2"])],
        out_specs=pl.BlockSpec((None, t, LANES), lambda bi, h, i: (bi, i, h)),
        out_shape=jax.ShapeDtypeStruct((b, s, DIFF_W), BF16),
        scratch_shapes=[pltpu.VMEM((2, t, LANES), F32)] * 3,
        compiler_params=_cparams(("parallel", "parallel", "arbitrary")),
        name="diff_attn",
    )(rel_bias_t, p["diff_bound"], qd, kd, vd, bias_tiles, p["subln"], p["lq1"], p["lk1"], p["lq2"], p["lk2"])


def _mla_attn_kernel(bound_ref, q_ref, k_ref, v_ref, o_ref, m_scr, l_scr, acc_scr, *, tk):
    t = q_ref.shape[0]
    n = v_ref.shape[0] // tk
    bound = bound_ref[0]
    fixed_ref = bound_ref[1] > 0.5

    l_scr[...] = jnp.zeros(l_scr.shape, F32)
    acc_scr[...] = jnp.zeros(acc_scr.shape, F32)

    def sweep(q, update):
        qs = (q[:, :LANES], q[:, LANES:])

        def step(j, carry):
            off = pl.multiple_of(j * tk, tk)
            v = v_ref[pl.ds(off, tk), :]
            for mi in range(2):
                k = k_ref[mi * LANES:(mi + 1) * LANES, pl.ds(off, tk)]
                update(mi, jnp.dot(qs[mi], k, preferred_element_type=F32), v)
            return carry

        lax.fori_loop(0, n, step, 0)

    @pl.when(fixed_ref)
    def _():
        q = q_ref[...]
        lane = lax.broadcasted_iota(I32, q.shape, 1)
        q = jnp.where(lane % LANES == MLA_QK, jnp.full(q.shape, -bound, F32).astype(BF16), q)
        sweep(q, lambda mi, s, v: _fixed_ref_update(mi, s, v, l_scr, acc_scr))

    @pl.when(jnp.logical_not(fixed_ref))
    def _():
        m_scr[...] = jnp.full(m_scr.shape, NEG_BIG, F32)
        sweep(q_ref[...], lambda mi, s, v: _online_update(mi, s, v, m_scr, l_scr, acc_scr))

    lane = lax.broadcasted_iota(I32, (t, LANES), 1)
    o = jnp.where(lane < MLA_V, _normalised(0, l_scr, acc_scr), _normalised(1, l_scr, acc_scr))
    o_ref[...] = o.astype(BF16)


def _mla_attention(qm, km, vm, bound):
    b, s, _ = qm.shape
    t = MLA_TQ
    return pl.pallas_call(
        functools.partial(_mla_attn_kernel, tk=MLA_TK),
        grid=(b, MLA_HEADS // 2, s // t),
        in_specs=[pl.BlockSpec(memory_space=pltpu.SMEM),
                  pl.BlockSpec((None, t, 2 * LANES), lambda bi, h, i: (bi, i, h)),
                  pl.BlockSpec((None, 2 * LANES, s), lambda bi, h, i: (bi, h, 0)),
                  pl.BlockSpec((None, s, LANES), lambda bi, h, i: (bi, 0, h))],
        out_specs=pl.BlockSpec((None, t, LANES), lambda bi, h, i: (bi, i, h)),
        out_shape=jax.ShapeDtypeStruct((b, s, MLA_HEADS * MLA_V), BF16),
        scratch_shapes=[pltpu.VMEM((2, t, LANES), F32)] * 3,
        compiler_params=_cparams(("parallel", "parallel", "arbitrary")),
        name="mla_attn",
    )(bound, qm, km, vm)


def _split_bf16(x):
    hi = x.astype(BF16)
    lo = (x - hi.astype(F32)).astype(BF16)
    return hi, lo


def _outproj_kernel(x_ref, od_ref, om_ref, wo1_ref, wo2_ref, fn_ref, wr_ref,
                    x1_ref, h2_ref, aff_ref):
    x1 = (x_ref[...]
          + jnp.dot(od_ref[...], wo1_ref[...], preferred_element_type=F32)
          + jnp.dot(om_ref[...], wo2_ref[...], preferred_element_type=F32))
    x1_ref[...] = x1
    h = _rms_rows(x1, fn_ref[...])
    hh, hl = _split_bf16(h)
    h2_ref[...] = hh
    both = jnp.dot(hh, wr_ref[...], preferred_element_type=F32)
    logits = both[:, :LANES] + both[:, LANES:] + jnp.dot(hl, wr_ref[:, :LANES], preferred_element_type=F32)
    lane = lax.broadcasted_iota(I32, logits.shape, 1)
    logits = jnp.where(lane < N_EXPERTS, logits, NEG_BIG)
    e = jnp.exp(logits - jnp.max(logits, axis=-1, keepdims=True))
    aff = e / jnp.sum(e, axis=-1, keepdims=True)
    aff_ref[...] = aff[:, :N_EXPERTS]


def _outproj(x2d, od, om, p):
    t = x2d.shape[0]
    tm = OUT_TM
    row = lambda i: (i, 0)
    wspec = lambda a: pl.BlockSpec(a.shape, lambda i: (0, 0))
    weights = (p["w_o1"], p["w_o2"], p["ffn_norm"], p["w_r"])
    return pl.pallas_call(
        _outproj_kernel,
        grid=(t // tm,),
        in_specs=[pl.BlockSpec((tm, D_MODEL), row), pl.BlockSpec((tm, DIFF_W), row),
                  pl.BlockSpec((tm, MLA_HEADS * MLA_V), row)] + [wspec(w) for w in weights],
        out_specs=[pl.BlockSpec((tm, D_MODEL), row), pl.BlockSpec((tm, D_MODEL), row),
                   pl.BlockSpec((tm, N_EXPERTS), row)],
        out_shape=[jax.ShapeDtypeStruct((t, D_MODEL), F32), jax.ShapeDtypeStruct((t, D_MODEL), BF16),
                   jax.ShapeDtypeStruct((t, N_EXPERTS), F32)],
        compiler_params=_cparams(("parallel",)),
        name="outproj",
    )(x2d, od, om, *weights)


def _route_kernel(aff_ref, pos_ref, gate_ref, csx_ref, *, cap):
    rows = aff_ref.shape[0]
    groups = LANES // N_EXPERTS
    aff = aff_ref[...]
    bits = pltpu.bitcast(aff, I32)

    def expert_total(v):
        for sh in (N_EXPERTS, 2 * N_EXPERTS, 4 * N_EXPERTS):
            v = v + pltpu.roll(v, sh, 1)
        return v

    def count(mask):
        return expert_total(jnp.sum(mask.astype(I32), axis=0, keepdims=True))

    def search(i, thr):
        cand = thr | jnp.left_shift(jnp.int32(1), 30 - i)
        return jnp.where(count(bits >= cand) >= cap, cand, thr)

    thr = lax.fori_loop(0, 31, search, jnp.zeros((1, LANES), I32))

    a = lax.broadcasted_iota(I32, (LANES, LANES), 0)
    b = lax.broadcasted_iota(I32, (LANES, LANES), 1)
    same = (a % N_EXPERTS) == (b % N_EXPERTS)
    q_all = same.astype(BF16)
    q_before = (same & (a // N_EXPERTS < b // N_EXPERTS)).astype(BF16)
    rb = ROUTE_RB
    ra = lax.broadcasted_iota(I32, (rb, rb), 0)
    ca = lax.broadcasted_iota(I32, (rb, rb), 1)
    tri = (ca < ra).astype(BF16)

    def prefix(mask):
        mb = mask.astype(BF16)
        outs = []
        offset = jnp.zeros((1, LANES), F32)
        for r0 in range(0, rows, rb):
            blk = mb[r0:r0 + rb]
            tot = jnp.dot(blk, q_all, preferred_element_type=F32)
            within = jnp.dot(blk, q_before, preferred_element_type=F32)
            above = jnp.dot(tri, tot.astype(BF16), preferred_element_type=F32)
            outs.append(above + within + offset)
            offset = offset + jnp.sum(tot, axis=0, keepdims=True)
        return jnp.concatenate(outs, axis=0)

    gt = bits > thr
    eq = bits == thr
    need = (cap - count(gt)).astype(F32)
    sel = gt | (eq & (prefix(eq) < need))
    csx = prefix(sel)
    pos_ref[...] = jnp.where(sel, csx.astype(I32), -1)
    gate_ref[...] = jnp.where(sel, aff, 0.0)
    csx_ref[...] = csx.astype(I32)


def _route(aff_packed, cap):
    rows = aff_packed.shape[0]
    vm = pl.BlockSpec(memory_space=pltpu.VMEM)
    return pl.pallas_call(
        functools.partial(_route_kernel, cap=cap),
        in_specs=[vm],
        out_specs=[vm, vm, vm],
        out_shape=[jax.ShapeDtypeStruct((rows, LANES), I32), jax.ShapeDtypeStruct((rows, LANES), F32),
                   jax.ShapeDtypeStruct((rows, LANES), I32)],
        compiler_params=pltpu.CompilerParams(vmem_limit_bytes=VMEM_LIMIT),
        name="route",
    )(aff_packed)


def _window_start(first_slot, align, w, cap):
    return jnp.minimum((first_slot // align) * align, cap - w)


def _gather_kernel(p0_ref, pn_ref, pos_ref, h2_ref, xg_ref, *, cap, w):
    ne = xg_ref.shape[0]
    e0 = pl.program_id(0) * ne
    i = pl.program_id(1)
    tg = h2_ref.shape[0]

    @pl.when(i == 0)
    def _():
        xg_ref[...] = jnp.zeros(xg_ref.shape, BF16)

    row_id = lax.broadcasted_iota(I32, (w, tg), 0)
    first = [pl.multiple_of(_window_start(p0_ref[e0 + j, i], BF16_ROWS, w, cap), BF16_ROWS) for j in range(ne)]
    prows = [pos_ref[j, pl.ds(i, 1), :] for j in range(ne)]

    onehot = jnp.concatenate([(prows[j] == row_id + first[j]).astype(BF16) for j in range(ne)], axis=0)
    rows = jnp.dot(onehot, h2_ref[...], preferred_element_type=F32).astype(BF16)
    for j in range(ne):
        xg_ref[j, pl.ds(first[j], w), :] += rows[j * w:(j + 1) * w]

    for j in range(ne):
        n_win = (pn_ref[e0 + j, i] - first[j] + w - 1) // w

        def window(k, carry, j=j):
            nominal = first[j] + k * w
            start = pl.multiple_of(jnp.minimum(nominal, cap - w), BF16_ROWS)
            hit = (prows[j] == row_id + start) & (prows[j] >= nominal)
            more = jnp.dot(hit.astype(BF16), h2_ref[...], preferred_element_type=F32).astype(BF16)
            xg_ref[j, pl.ds(start, w), :] += more
            return carry

        lax.fori_loop(1, n_win, window, 0)


def _gather(p0, pn, pos_t, h2, cap):
    t = h2.shape[0]
    tg, ne = GATHER_TG, GATHER_NE
    return pl.pallas_call(
        functools.partial(_gather_kernel, cap=cap, w=GATHER_W),
        grid_spec=pltpu.PrefetchScalarGridSpec(
            num_scalar_prefetch=2,
            grid=(N_EXPERTS // ne, t // tg),
            in_specs=[pl.BlockSpec((ne, t // tg, tg), lambda e, i, *_: (e, 0, 0)),
                      pl.BlockSpec((tg, D_MODEL), lambda e, i, *_: (i, 0))],
            out_specs=pl.BlockSpec((ne, cap, D_MODEL), lambda e, i, *_: (e, 0, 0))),
        out_shape=jax.ShapeDtypeStruct((N_EXPERTS, cap, D_MODEL), BF16),
        compiler_params=_cparams(("parallel", "arbitrary")),
        name="gather",
    )(p0, pn, pos_t, h2)


def _ffn_kernel(xg_ref, wg_ref, wu_ref, wd_ref, y_ref, acc_scr):
    f = pl.program_id(1)

    @pl.when((pl.program_id(0) == 0) & (f == 0))
    def _():
        acc_scr[...] = jnp.zeros(acc_scr.shape, F32)

    xg = xg_ref[...]
    a = jnp.dot(xg, wg_ref[...].astype(BF16), preferred_element_type=F32)
    b = jnp.dot(xg, wu_ref[...].astype(BF16), preferred_element_type=F32)
    act = (a * jax.nn.sigmoid(a) * b).astype(BF16)
    part = jnp.dot(act, wd_ref[...].astype(BF16), preferred_element_type=F32)
    acc_scr[...] = jnp.where(f == 0, part, acc_scr[...] + part)

    @pl.when(f == pl.num_programs(1) - 1)
    def _():
        y_ref[...] = acc_scr[...].astype(BF16)


def _ffn(xg, w_gate, w_up, w_down):
    _, cap, _ = xg.shape
    tf = FFN_TF
    return pl.pallas_call(
        _ffn_kernel,
        grid=(N_EXPERTS, D_FF // tf),
        in_specs=[pl.BlockSpec((None, cap, D_MODEL), lambda e, f: (e, 0, 0)),
                  pl.BlockSpec((None, D_MODEL, tf), lambda e, f: (e, 0, f)),
                  pl.BlockSpec((None, D_MODEL, tf), lambda e, f: (e, 0, f)),
                  pl.BlockSpec((None, tf, D_MODEL), lambda e, f: (e, f, 0))],
        out_specs=pl.BlockSpec((None, cap, D_MODEL), lambda e, f: (e, 0, 0)),
        out_shape=jax.ShapeDtypeStruct((N_EXPERTS, cap, D_MODEL), BF16),
        scratch_shapes=[pltpu.VMEM((cap, D_MODEL), F32)],
        compiler_params=_cparams(("arbitrary", "arbitrary")),
        name="ffn",
    )(xg, w_gate, w_up, w_down)


def _combine_kernel(p0_ref, pn_ref, x1_ref, pos_ref, gate_ref, y_hbm, o_ref, win, xwin, sem, xsem, *, cap):
    i = pl.program_id(0)
    tt = x1_ref.shape[0]
    w = xwin.shape[0]

    def first_window(e):
        return _window_start(p0_ref[i, e], BF16_ROWS, w, cap)

    def window_copy(e):
        return pltpu.make_async_copy(y_hbm.at[e, pl.ds(first_window(e), w), :],
                                     win.at[pl.ds(e * w, w), :], sem.at[e])

    for e in range(N_EXPERTS):
        window_copy(e).start()

    lane = lax.broadcasted_iota(I32, (tt, w), 1)

    def weights(e, first_slot, lo_slot):
        pcol = pos_ref[:, e:e + 1]
        hit = (pcol - first_slot == lane) & (pcol >= lo_slot)
        return jnp.where(hit, gate_ref[:, e:e + 1], 0.0).astype(BF16)

    acc = x1_ref[...]
    for g0 in range(0, N_EXPERTS, COMB_GROUP):
        group = range(g0, g0 + COMB_GROUP)
        c = jnp.concatenate([weights(e, first_window(e), 0) for e in group], axis=1)
        for e in group:
            window_copy(e).wait()
        acc = acc + jnp.dot(c, win[g0 * w:(g0 + COMB_GROUP) * w, :], preferred_element_type=F32)
    o_ref[...] = acc

    for e in range(N_EXPERTS):
        w0 = first_window(e)
        n_win = (pn_ref[i, e] - w0 + w - 1) // w

        def extra(k, carry, e=e, w0=w0):
            nominal = w0 + k * w
            start = jnp.minimum(nominal, cap - w)
            cp = pltpu.make_async_copy(y_hbm.at[e, pl.ds(start, w), :], xwin, xsem)
            cp.start()
            cp.wait()
            o_ref[...] += jnp.dot(weights(e, start, nominal), xwin[...], preferred_element_type=F32)
            return carry

        lax.fori_loop(1, n_win, extra, 0)


def _combine(p0, pn, x1, pos_tok, gate_tok, y, cap):
    t = x1.shape[0]
    tt, w = COMB_TT, COMB_W
    row = lambda i, *_: (i, 0)
    return pl.pallas_call(
        functools.partial(_combine_kernel, cap=cap),
        grid_spec=pltpu.PrefetchScalarGridSpec(
            num_scalar_prefetch=2,
            grid=(t // tt,),
            in_specs=[pl.BlockSpec((tt, D_MODEL), row), pl.BlockSpec((tt, N_EXPERTS), row),
                      pl.BlockSpec((tt, N_EXPERTS), row), pl.BlockSpec(memory_space=pl.ANY)],
            out_specs=pl.BlockSpec((tt, D_MODEL), row),
            scratch_shapes=[pltpu.VMEM((N_EXPERTS * w, D_MODEL), BF16), pltpu.VMEM((w, D_MODEL), BF16),
                            pltpu.SemaphoreType.DMA((N_EXPERTS,)), pltpu.SemaphoreType.DMA(())]),
        out_shape=jax.ShapeDtypeStruct((t, D_MODEL), F32),
        compiler_params=_cparams(("arbitrary",)),
        name="combine",
    )(p0, pn, x1, pos_tok, gate_tok, y)


def _score_bound(width, q_gain, k_gain, max_bias):
    gmax = lambda g: jnp.max(jnp.abs(g.astype(F32)))
    bound = BOUND_SLACK * (width ** 0.5 * LOG2E * gmax(q_gain) * gmax(k_gain) + LOG2E * max_bias)
    return jnp.stack([bound, (3.0 * bound < F32_EXP_RANGE).astype(F32)]).astype(F32)


def _layer_params(l, rel_bias, attn_norm, w_in, diff_q_norm, diff_k_norm, lambda_q1, lambda_k1, lambda_q2,
                  lambda_k2, diff_subln, mla_q_latent_norm, mla_kv_latent_norm, w_uq, w_ukv, mla_q_norm,
                  mla_k_norm, w_out, ffn_norm, w_router):
    row = lambda v: v.reshape(1, -1).astype(F32)
    w = w_in[l]
    o = 3 * DIFF_W
    pad_rope = MLA_NOPE, LANES - MLA_QK
    w_kpe = jnp.pad(w[:, o + Q_LORA + KV_LORA:], ((0, 0), pad_rope))
    uq = jnp.pad(w_uq[l].reshape(Q_LORA, MLA_HEADS, MLA_QK), ((0, 0), (0, 0), (0, LANES - MLA_QK)))
    ukv = w_ukv[l].reshape(KV_LORA, MLA_HEADS, MLA_NOPE + MLA_V)
    uk = jnp.pad(ukv[:, :, :MLA_NOPE], ((0, 0), (0, 0), (0, LANES - MLA_NOPE)))
    head_gain = lambda g: jnp.pad(g.astype(F32), (0, LANES - MLA_QK)).reshape(1, LANES)
    half = MLA_ROPE // 2

    def swap_rope_halves(a):
        z = jnp.zeros_like(a)
        x1, x2 = a[..., MLA_NOPE:MLA_NOPE + half], a[..., MLA_NOPE + half:MLA_QK]
        return jnp.concatenate([z[..., :MLA_NOPE], x2, x1, z[..., MLA_QK:]], axis=-1)

    wr = jnp.pad(w_router[l].astype(F32), ((0, 0), (0, LANES - N_EXPERTS)))
    wr_hi = wr.astype(BF16)
    return dict(
        attn_norm=row(attn_norm[l]),
        w_qkv=w[:, :o].astype(BF16), w_cq=w[:, o:o + Q_LORA].astype(BF16),
        w_ckv=w[:, o + Q_LORA:o + Q_LORA + KV_LORA].astype(BF16), w_kpe=w_kpe.astype(BF16),
        w_uq=uq.reshape(Q_LORA, MLA_W).astype(BF16), w_uk=uk.reshape(KV_LORA, MLA_W).astype(BF16),
        w_uq_sw=swap_rope_halves(uq).reshape(Q_LORA, MLA_W).astype(BF16),
        mq_gain_sw=swap_rope_halves(head_gain(mla_q_norm[l])),
        w_uv=ukv[:, :, MLA_NOPE:].reshape(KV_LORA, MLA_HEADS * MLA_V).astype(BF16),
        dq_gain=row(jnp.tile(diff_q_norm[l], 2)), dk_gain=row(jnp.tile(diff_k_norm[l], 2)),
        cq_gain=row(mla_q_latent_norm[l]), ckv_gain=row(mla_kv_latent_norm[l]),
        mq_gain=head_gain(mla_q_norm[l]), mk_gain=head_gain(mla_k_norm[l]),
        subln=row(diff_subln[l]), lq1=row(lambda_q1[l]), lk1=row(lambda_k1[l]),
        lq2=row(lambda_q2[l]), lk2=row(lambda_k2[l]),
        rel_bias_t=rel_bias.astype(F32).T,
        diff_bound=_score_bound(DIFF_DK, diff_q_norm[l], diff_k_norm[l], jnp.max(jnp.abs(rel_bias))),
        mla_bound=_score_bound(MLA_QK, mla_q_norm[l], mla_k_norm[l], 0.0),
        w_o1=w_out[l][:DIFF_W].astype(BF16), w_o2=w_out[l][DIFF_W:].astype(BF16),
        ffn_norm=row(ffn_norm[l]),
        w_r=jnp.concatenate([wr_hi, (wr - wr_hi.astype(F32)).astype(BF16)], axis=1),
    )


def _rope_tables(seq):
    half = MLA_ROPE // 2
    inv = 1.0 / (ROPE_BASE ** (jnp.arange(half, dtype=F32) / half))
    ang = jnp.arange(seq, dtype=jnp.int32).astype(F32)[:, None] * inv[None, :]
    cos, sin = jnp.cos(ang), jnp.sin(ang)
    ones = jnp.ones((seq, MLA_NOPE), F32)
    tail = LANES - MLA_QK
    cos_t = jnp.concatenate([ones, cos, cos, jnp.ones((seq, tail), F32)], axis=1)
    sin_t = jnp.concatenate([0 * ones, -sin, sin, jnp.zeros((seq, tail), F32)], axis=1)
    return cos_t, sin_t


def _layer(x, p, bias, rope, w_gate, w_up, w_down, lam_init):
    b, s, d = x.shape
    t = b * s
    cap = CAPACITY_FACTOR * t // N_EXPERTS
    x2d = x.reshape(t, d)

    qd, kd, vd, qm, km, vm = _prep(x2d, s, rope[0], rope[1], p)
    shp = lambda a: a.reshape(b, s, a.shape[-1])
    keys_t = lambda a: jnp.swapaxes(shp(a), 1, 2)
    od = _diff_attention(shp(qd), keys_t(kd), shp(vd), bias, p["rel_bias_t"], p, lam_init)
    om = _mla_attention(shp(qm), keys_t(km), shp(vm), p["mla_bound"])
    x1, h2, aff = _outproj(x2d, od.reshape(t, -1), om.reshape(t, -1), p)

    groups = LANES // N_EXPERTS
    pos_p, gate_p, csx_p = _route(aff.reshape(t // groups, LANES), cap)
    pos_tok = pos_p.reshape(t, N_EXPERTS)
    gate_tok = gate_p.reshape(t, N_EXPERTS)
    csx_tok = csx_p.reshape(t, N_EXPERTS)

    def tile_slots(tile):
        first = csx_tok[::tile]
        return first, jnp.concatenate([first[1:], jnp.full((1, N_EXPERTS), cap, I32)], axis=0)

    g0, gn = tile_slots(GATHER_TG)
    pos_t = pos_tok.T.reshape(N_EXPERTS, t // GATHER_TG, GATHER_TG)
    xg = _gather(g0.T, gn.T, pos_t, h2, cap)
    y = _ffn(xg, w_gate, w_up, w_down)
    c0, cn = tile_slots(COMB_TT)
    out = _combine(c0, cn, x1, pos_tok, gate_tok, y, cap)
    return out.reshape(b, s, d)


def kernel(x_prompt, x_sample, rel_bias, attn_norm, w_in, diff_q_norm, diff_k_norm, lambda_q1, lambda_k1, lambda_q2, lambda_k2, diff_subln, mla_q_latent_norm, mla_kv_latent_norm, w_uq, w_ukv, mla_q_norm, mla_k_norm, w_out, ffn_norm, w_router, w_gate, w_up, w_down):
    params = [_layer_params(l, rel_bias, attn_norm, w_in, diff_q_norm, diff_k_norm, lambda_q1, lambda_k1,
                            lambda_q2, lambda_k2, diff_subln, mla_q_latent_norm, mla_kv_latent_norm, w_uq,
                            w_ukv, mla_q_norm, mla_k_norm, w_out, ffn_norm, w_router) for l in range(DEPTH)]
    biases = [_bias_tiles(p["rel_bias_t"], ATT_T) for p in params]
    rope = _rope_tables(max(x_prompt.shape[1], x_sample.shape[1]))
    outs = []
    for x in (x_prompt, x_sample):
        for l in range(DEPTH):
            lam_init = 0.8 - 0.6 * math.exp(-0.3 * l)
            x = _layer(x, params[l], biases[l], rope, w_gate[l], w_up[l], w_down[l], lam_init)
        outs.append(x)
    return tuple(outs)
```

```python
import functools
import math

import jax
import jax.numpy as jnp
from jax import lax
from jax.experimental import pallas as pl
from jax.experimental.pallas import tpu as pltpu

F32 = jnp.float32
BF16 = jnp.bfloat16
I32 = jnp.int32

D_MODEL = 1024
DEPTH = 1
DIFF_HEADS = 4
DIFF_DK = 64
DIFF_DV = 128
MLA_HEADS = 8
MLA_NOPE = 64
MLA_ROPE = 32
MLA_QK = MLA_NOPE + MLA_ROPE
MLA_V = 64
Q_LORA = 256
KV_LORA = 128
ROPE_BASE = 10000.0
N_BUCKETS = 32
MAX_DISTANCE = 128
N_EXPERTS = 16
CAPACITY_FACTOR = 2
D_FF = 2816
EPS = 1e-6

LANES = 128
SUBLANES = 8
BF16_ROWS = 16
DIFF_W = DIFF_HEADS * DIFF_DV
MLA_W = MLA_HEADS * LANES
NEG_BIG = -1e30
LOG2E = math.log2(math.e)
VMEM_LIMIT = 56 * 1024 * 1024
F32_EXP_RANGE = 120.0
BOUND_SLACK = 1.02

PREP_TM = 512
ATT_T = 512
DIFF_TK = 4096
BIAS_TILES = 5
MLA_TQ = 512
MLA_TK = 4096
OUT_TM = 512
ROUTE_RB = 256
GATHER_TG = 1024
GATHER_W = 192
GATHER_NE = 4
FFN_TF = 256
COMB_TT = 512
COMB_W = 128
COMB_GROUP = 4


def _cparams(sem, vmem=VMEM_LIMIT):
    return pltpu.CompilerParams(dimension_semantics=sem, vmem_limit_bytes=vmem)


def _rms_rows(x, gain):
    ms = jnp.mean(x * x, axis=-1, keepdims=True)
    return x * lax.rsqrt(ms + EPS) * gain


def _prep_kernel(x_ref, cos_ref, sin_ref, an_ref, wqkv_ref, wcq_ref, wckv_ref, wkpe_ref,
                 wuq_ref, wuqs_ref, wuk_ref, wuv_ref, dqg_ref, dkg_ref, cqg_ref, ckvg_ref, mqg_ref, mqgs_ref,
                 mkg_ref, qd_ref, kd_ref, vd_ref, qm_ref, km_ref, vm_ref):
    tm = x_ref.shape[0]
    hb = _rms_rows(x_ref[...], an_ref[...]).astype(BF16)

    lane = lax.broadcasted_iota(I32, (tm, LANES), 1)
    lo = lane < DIFF_DK

    cq = _rms_rows(jnp.dot(hb, wcq_ref[...], preferred_element_type=F32), cqg_ref[...]).astype(BF16)
    ckv = _rms_rows(jnp.dot(hb, wckv_ref[...], preferred_element_type=F32), ckvg_ref[...]).astype(BF16)
    kpe = jnp.dot(hb, wkpe_ref[...], preferred_element_type=F32)
    q = jnp.dot(cq, wuq_ref[...], preferred_element_type=F32)
    q_sw = jnp.dot(cq, wuqs_ref[...], preferred_element_type=F32)
    k = jnp.dot(ckv, wuk_ref[...], preferred_element_type=F32)
    vm_ref[...] = jnp.dot(ckv, wuv_ref[...], preferred_element_type=F32).astype(BF16)
    qkv = jnp.dot(hb, wqkv_ref[...], preferred_element_type=F32)

    cosb = cos_ref[...]
    sinb = sin_ref[...]
    mla_scale = MLA_QK ** -0.5 * LOG2E
    first_half = (lane >= MLA_NOPE) & (lane < MLA_NOPE + MLA_ROPE // 2)
    kpe_g = kpe * mkg_ref[...]
    kpe_rot = kpe_g * cosb + sinb * jnp.where(first_half,
                                              pltpu.roll(kpe_g, LANES - MLA_ROPE // 2, 1),
                                              pltpu.roll(kpe_g, MLA_ROPE // 2, 1))
    kpe_ss = jnp.sum(kpe * kpe, axis=-1, keepdims=True)

    for hd in range(MLA_HEADS):
        sl = slice(hd * LANES, (hd + 1) * LANES)
        qb = q[:, sl]
        rq = lax.rsqrt(jnp.sum(qb * qb, axis=-1, keepdims=True) / MLA_QK + EPS) * mla_scale
        qm_ref[:, sl] = (rq * (qb * mqg_ref[...] * cosb + q_sw[:, sl] * mqgs_ref[...] * sinb)).astype(BF16)
        kb = k[:, sl]
        rk = lax.rsqrt((jnp.sum(kb * kb, axis=-1, keepdims=True) + kpe_ss) / MLA_QK + EPS)
        kn = rk * (kb * mkg_ref[...] + kpe_rot)
        km_ref[:, sl] = jnp.where(lane == MLA_QK, 1.0, kn).astype(BF16)

    diff_scale = DIFF_DK ** -0.5 * LOG2E
    for hd in range(DIFF_HEADS):
        for off, g_ref, o_ref, scale in ((0, dqg_ref, qd_ref, diff_scale), (DIFF_W, dkg_ref, kd_ref, None)):
            blk = qkv[:, off + hd * LANES: off + (hd + 1) * LANES]
            sq = blk * blk
            s_lo = jnp.sum(jnp.where(lo, sq, 0.0), axis=-1, keepdims=True)
            s_hi = jnp.sum(jnp.where(lo, 0.0, sq), axis=-1, keepdims=True)
            r = jnp.where(lo, lax.rsqrt(s_lo / DIFF_DK + EPS), lax.rsqrt(s_hi / DIFF_DK + EPS))
            y = blk * r * g_ref[...]
            if scale is not None:
                y = y * scale
            o_ref[:, hd * LANES:(hd + 1) * LANES] = y.astype(BF16)
    vd_ref[...] = qkv[:, 2 * DIFF_W:3 * DIFF_W].astype(BF16)


def _prep(x2d, seq, cos_t, sin_t, p):
    t = x2d.shape[0]
    tm = PREP_TM
    nseq = seq // tm
    row = lambda i: (i, 0)
    full = lambda i: (0, 0)
    wspec = lambda a: pl.BlockSpec(a.shape, full)
    weights = (p["attn_norm"], p["w_qkv"], p["w_cq"], p["w_ckv"], p["w_kpe"], p["w_uq"], p["w_uq_sw"], p["w_uk"],
               p["w_uv"], p["dq_gain"], p["dk_gain"], p["cq_gain"], p["ckv_gain"], p["mq_gain"], p["mq_gain_sw"],
               p["mk_gain"])
    out_w = (DIFF_W, DIFF_W, DIFF_W, MLA_W, MLA_W, MLA_HEADS * MLA_V)
    return pl.pallas_call(
        _prep_kernel,
        grid=(t // tm,),
        in_specs=[pl.BlockSpec((tm, D_MODEL), row),
                  pl.BlockSpec((tm, LANES), lambda i: (i % nseq, 0)),
                  pl.BlockSpec((tm, LANES), lambda i: (i % nseq, 0))] + [wspec(w) for w in weights],
        out_specs=[pl.BlockSpec((tm, w), row) for w in out_w],
        out_shape=[jax.ShapeDtypeStruct((t, w), BF16) for w in out_w],
        compiler_params=_cparams(("parallel",)),
        name="prep",
    )(x2d, cos_t, sin_t, *weights)


def _bias_kernel(rb_ref, o_ref):
    hd = pl.program_id(0)
    d = pl.program_id(1)
    tq, tk = o_ref.shape
    row = lax.broadcasted_iota(I32, (tq, tk), 0)
    col = lax.broadcasted_iota(I32, (tq, tk), 1)
    rel = col - row + (d - BIAS_TILES // 2) * tk
    half = N_BUCKETS // 2
    max_exact = half // 2
    n = jnp.abs(rel)
    nf = jnp.maximum(n, 1).astype(F32)
    large = max_exact + (jnp.log(nf / max_exact) / math.log(MAX_DISTANCE / max_exact)
                         * (half - max_exact)).astype(I32)
    large = jnp.minimum(large, half - 1)
    bucket = jnp.where(rel > 0, half, 0) + jnp.where(n < max_exact, n, large)
    acc = jnp.zeros((tq, tk), F32)
    for b in range(N_BUCKETS):
        acc = jnp.where(bucket == b, rb_ref[hd, b], acc)
    o_ref[...] = acc * LOG2E


def _bias_tiles(rel_bias_t, t):
    return pl.pallas_call(
        _bias_kernel,
        grid=(DIFF_HEADS, BIAS_TILES),
        in_specs=[pl.BlockSpec(memory_space=pltpu.SMEM)],
        out_specs=pl.BlockSpec((None, None, t, t), lambda h, d: (h, d, 0, 0)),
        out_shape=jax.ShapeDtypeStruct((DIFF_HEADS, BIAS_TILES, t, t), F32),
        compiler_params=_cparams(("parallel", "parallel")),
        name="bias_tiles",
    )(rel_bias_t)


def _online_update(mi, s, v, m_scr, l_scr, acc_scr):
    blocks = [s[:, i:i + LANES] for i in range(0, s.shape[1], LANES)]
    m_old = m_scr[mi]
    row_max = jnp.max(functools.reduce(jnp.maximum, blocks), axis=-1, keepdims=True)
    m_new = jnp.maximum(m_old, row_max)
    alpha = jnp.exp2(m_old - m_new)
    ps = [jnp.exp2(blk - m_new) for blk in blocks]
    l_scr[mi] = alpha * l_scr[mi] + functools.reduce(jnp.add, ps)
    p = jnp.concatenate([x.astype(BF16) for x in ps], axis=1)
    acc_scr[mi] = alpha * acc_scr[mi] + jnp.dot(p, v, preferred_element_type=F32)
    m_scr[mi] = m_new


def _fixed_ref_update(mi, s, v, l_scr, acc_scr):
    ps = [jnp.exp2(s[:, i:i + LANES]) for i in range(0, s.shape[1], LANES)]
    l_scr[mi] += functools.reduce(jnp.add, ps)
    p = jnp.concatenate([x.astype(BF16) for x in ps], axis=1)
    acc_scr[mi] += jnp.dot(p, v, preferred_element_type=F32)


def _normalised(mi, l_scr, acc_scr):
    return acc_scr[mi] / jnp.sum(l_scr[mi], axis=-1, keepdims=True)


def _diff_attn_kernel(rb_ref, bound_ref, q_ref, k_ref, v_ref, bias_ref, subln_ref, lq1_ref, lk1_ref, lq2_ref,
                      lk2_ref, o_ref, m_scr, l_scr, acc_scr, *, lam_init, tk):
    hd = pl.program_id(1)
    qi = pl.program_id(2)
    t = q_ref.shape[0]
    n = v_ref.shape[0] // t
    r = tk // t
    bound = bound_ref[0]
    fixed_ref = bound_ref[1] > 0.5

    q = q_ref[...]
    lane = lax.broadcasted_iota(I32, q.shape, 1)
    zero = jnp.zeros_like(q)
    qs = (jnp.where(lane < DIFF_DK, q, zero), jnp.where(lane < DIFF_DK, zero, q))

    l_scr[...] = jnp.zeros(l_scr.shape, F32)
    acc_scr[...] = jnp.zeros(acc_scr.shape, F32)

    lo = jnp.maximum(qi - 1, 0) // r
    hi = (jnp.minimum(qi + 2, n) + r - 1) // r
    c_left = rb_ref[hd, N_BUCKETS // 2 - 1] * LOG2E
    c_right = rb_ref[hd, N_BUCKETS - 1] * LOG2E
    n_side = bias_ref.shape[0] // 2
    LEFT, BAND, RIGHT = range(3)

    def sweep(update, after_left=lambda: None, after_band=lambda: None):
        def make_step(phase):
            def step(j, carry):
                off = pl.multiple_of(j * tk, tk)
                k = k_ref[:, pl.ds(off, tk)]
                v = v_ref[pl.ds(off, tk), :]
                for mi in range(2):
                    s = jnp.dot(qs[mi], k, preferred_element_type=F32)
                    if phase == BAND:
                        pieces = []
                        for c in range(r):
                            d = jnp.clip(j * r + c - qi, -n_side, n_side)
                            pieces.append(s[:, c * t:(c + 1) * t] + bias_ref[d + n_side])
                        s = jnp.concatenate(pieces, axis=1)
                    update(phase, mi, s, v)
                return carry
            return step

        lax.fori_loop(0, lo, make_step(LEFT), 0)
        after_left()
        lax.fori_loop(lo, hi, make_step(BAND), 0)
        after_band()
        lax.fori_loop(hi, n // r, make_step(RIGHT), 0)

    @pl.when(fixed_ref)
    def _():
        ref = (bound, bound + c_left, bound + c_left - c_right)
        sweep(lambda phase, mi, s, v: _fixed_ref_update(mi, s - ref[phase], v, l_scr, acc_scr))

    @pl.when(jnp.logical_not(fixed_ref))
    def _():
        m_scr[...] = jnp.full(m_scr.shape, NEG_BIG, F32)

        def shift(c):
            m_scr[...] = m_scr[...] + c

        sweep(lambda phase, mi, s, v: _online_update(mi, s, v, m_scr, l_scr, acc_scr),
              lambda: shift(c_left), lambda: shift(-c_right))

    lam = (jnp.exp(jnp.sum(lq1_ref[...] * lk1_ref[...], axis=-1, keepdims=True))
           - jnp.exp(jnp.sum(lq2_ref[...] * lk2_ref[...], axis=-1, keepdims=True)) + lam_init)
    o = _normalised(0, l_scr, acc_scr) - lam * _normalised(1, l_scr, acc_scr)
    o_ref[...] = (_rms_rows(o, subln_ref[...]) * (1.0 - lam_init)).astype(BF16)


def _diff_attention(qd, kd, vd, bias_tiles, rel_bias_t, p, lam_init):
    b, s, _ = qd.shape
    t = ATT_T
    vec = lambda a: pl.BlockSpec(a.shape, lambda bi, h, i: (0, 0))
    return pl.pallas_call(
        functools.partial(_diff_attn_kernel, lam_init=lam_init, tk=DIFF_TK),
        grid=(b, DIFF_HEADS, s // t),
        in_specs=[pl.BlockSpec(memory_space=pltpu.SMEM), pl.BlockSpec(memory_space=pltpu.SMEM),
                  pl.BlockSpec((None, t, LANES), lambda bi, h, i: (bi, i, h)),
                  pl.BlockSpec((None, LANES, s), lambda bi, h, i: (bi, h, 0)),
                  pl.BlockSpec((None, s, LANES), lambda bi, h, i: (bi, 0, h)),
                  pl.BlockSpec((None, BIAS_TILES, t, t), lambda bi, h, i: (h, 0, 0, 0)),
                  vec(p["subln"]), vec(p["lq1"]), vec(p["lk1"]), vec(p["lq2"]), vec(p["lk2"])],
        out_specs=pl.BlockSpec((None, t, LANES), lambda bi, h, i: (bi, i, h)),
        out_shape=jax.ShapeDtypeStruct((b, s, DIFF_W), BF16),
        scratch_shapes=[pltpu.VMEM((2, t, LANES), F32)] * 3,
        compiler_params=_cparams(("parallel", "parallel", "arbitrary")),
        name="diff_attn",
    )(rel_bias_t, p["diff_bound"], qd, kd, vd, bias_tiles, p["subln"], p["lq1"], p["lk1"], p["lq2"], p["lk2"])


def _mla_attn_kernel(bound_ref, q_ref, k_ref, v_ref, o_ref, m_scr, l_scr, acc_scr, *, tk):
    t = q_ref.shape[0]
    n = v_ref.shape[0] // tk
    bound = bound_ref[0]
    fixed_ref = bound_ref[1] > 0.5

    l_scr[...] = jnp.zeros(l_scr.shape, F32)
    acc_scr[...] = jnp.zeros(acc_scr.shape, F32)

    def sweep(q, update, unroll=False):
        qs = (q[:, :LANES], q[:, LANES:])

        def step(j, carry):
            off = pl.multiple_of(j * tk, tk)
            v = v_ref[pl.ds(off, tk), :]
            for mi in range(2):
                k = k_ref[mi * LANES:(mi + 1) * LANES, pl.ds(off, tk)]
                update(mi, jnp.dot(qs[mi], k, preferred_element_type=F32), v)
            return carry

        lax.fori_loop(0, n, step, 0, unroll=unroll)

    @pl.when(fixed_ref)
    def _():
        q = q_ref[...]
        lane = lax.broadcasted_iota(I32, q.shape, 1)
        q = jnp.where(lane % LANES == MLA_QK, jnp.full(q.shape, -bound, F32).astype(BF16), q)
        sweep(q, lambda mi, s, v: _fixed_ref_update(mi, s, v, l_scr, acc_scr), unroll=True)

    @pl.when(jnp.logical_not(fixed_ref))
    def _():
        m_scr[...] = jnp.full(m_scr.shape, NEG_BIG, F32)
        sweep(q_ref[...], lambda mi, s, v: _online_update(mi, s, v, m_scr, l_scr, acc_scr))

    lane = lax.broadcasted_iota(I32, (t, LANES), 1)
    o = jnp.where(lane < MLA_V, _normalised(0, l_scr, acc_scr), _normalised(1, l_scr, acc_scr))
    o_ref[...] = o.astype(BF16)


def _mla_attention(qm, km, vm, bound):
    b, s, _ = qm.shape
    t = MLA_TQ
    return pl.pallas_call(
        functools.partial(_mla_attn_kernel, tk=MLA_TK),
        grid=(b, MLA_HEADS // 2, s // t),
        in_specs=[pl.BlockSpec(memory_space=pltpu.SMEM),
                  pl.BlockSpec((None, t, 2 * LANES), lambda bi, h, i: (bi, i, h)),
                  pl.BlockSpec((None, 2 * LANES, s), lambda bi, h, i: (bi, h, 0)),
                  pl.BlockSpec((None, s, LANES), lambda bi, h, i: (bi, 0, h))],
        out_specs=pl.BlockSpec((None, t, LANES), lambda bi, h, i: (bi, i, h)),
        out_shape=jax.ShapeDtypeStruct((b, s, MLA_HEADS * MLA_V), BF16),
        scratch_shapes=[pltpu.VMEM((2, t, LANES), F32)] * 3,
        compiler_params=_cparams(("parallel", "parallel", "arbitrary")),
        name="mla_attn",
    )(bound, qm, km, vm)


def _split_bf16(x):
    hi = x.astype(BF16)
    lo = (x - hi.astype(F32)).astype(BF16)
    return hi, lo


def _outproj_kernel(x_ref, od_ref, om_ref, wo1_ref, wo2_ref, fn_ref, wr_ref,
                    x1_ref, h2_ref, aff_ref):
    x1 = (x_ref[...]
          + jnp.dot(od_ref[...], wo1_ref[...], preferred_element_type=F32)
          + jnp.dot(om_ref[...], wo2_ref[...], preferred_element_type=F32))
    x1_ref[...] = x1
    h = _rms_rows(x1, fn_ref[...])
    hh, hl = _split_bf16(h)
    h2_ref[...] = hh
    both = jnp.dot(hh, wr_ref[...], preferred_element_type=F32)
    logits = both[:, :LANES] + both[:, LANES:] + jnp.dot(hl, wr_ref[:, :LANES], preferred_element_type=F32)
    lane = lax.broadcasted_iota(I32, logits.shape, 1)
    logits = jnp.where(lane < N_EXPERTS, logits, NEG_BIG)
    e = jnp.exp(logits - jnp.max(logits, axis=-1, keepdims=True))
    aff = e / jnp.sum(e, axis=-1, keepdims=True)
    aff_ref[...] = aff[:, :N_EXPERTS]


def _outproj(x2d, od, om, p):
    t = x2d.shape[0]
    tm = OUT_TM
    row = lambda i: (i, 0)
    wspec = lambda a: pl.BlockSpec(a.shape, lambda i: (0, 0))
    weights = (p["w_o1"], p["w_o2"], p["ffn_norm"], p["w_r"])
    return pl.pallas_call(
        _outproj_kernel,
        grid=(t // tm,),
        in_specs=[pl.BlockSpec((tm, D_MODEL), row), pl.BlockSpec((tm, DIFF_W), row),
                  pl.BlockSpec((tm, MLA_HEADS * MLA_V), row)] + [wspec(w) for w in weights],
        out_specs=[pl.BlockSpec((tm, D_MODEL), row), pl.BlockSpec((tm, D_MODEL), row),
                   pl.BlockSpec((tm, N_EXPERTS), row)],
        out_shape=[jax.ShapeDtypeStruct((t, D_MODEL), F32), jax.ShapeDtypeStruct((t, D_MODEL), BF16),
                   jax.ShapeDtypeStruct((t, N_EXPERTS), F32)],
        compiler_params=_cparams(("parallel",)),
        name="outproj",
    )(x2d, od, om, *weights)


def _route_kernel(aff_ref, pos_ref, gate_ref, csx_ref, *, cap):
    rows = aff_ref.shape[0]
    groups = LANES // N_EXPERTS
    aff = aff_ref[...]
    bits = pltpu.bitcast(aff, I32)

    def expert_total(v):
        for sh in (N_EXPERTS, 2 * N_EXPERTS, 4 * N_EXPERTS):
            v = v + pltpu.roll(v, sh, 1)
        return v

    def count(mask):
        return expert_total(jnp.sum(mask.astype(I32), axis=0, keepdims=True))

    def search(i, thr):
        cand = thr | jnp.left_shift(jnp.int32(1), 30 - i)
        return jnp.where(count(bits >= cand) >= cap, cand, thr)

    thr = lax.fori_loop(0, 31, search, jnp.zeros((1, LANES), I32))

    a = lax.broadcasted_iota(I32, (LANES, LANES), 0)
    b = lax.broadcasted_iota(I32, (LANES, LANES), 1)
    same = (a % N_EXPERTS) == (b % N_EXPERTS)
    q_all = same.astype(BF16)
    q_before = (same & (a // N_EXPERTS < b // N_EXPERTS)).astype(BF16)
    rb = ROUTE_RB
    ra = lax.broadcasted_iota(I32, (rb, rb), 0)
    ca = lax.broadcasted_iota(I32, (rb, rb), 1)
    tri = (ca < ra).astype(BF16)

    def prefix(mask):
        mb = mask.astype(BF16)
        outs = []
        offset = jnp.zeros((1, LANES), F32)
        for r0 in range(0, rows, rb):
            blk = mb[r0:r0 + rb]
            tot = jnp.dot(blk, q_all, preferred_element_type=F32)
            within = jnp.dot(blk, q_before, preferred_element_type=F32)
            above = jnp.dot(tri, tot.astype(BF16), preferred_element_type=F32)
            outs.append(above + within + offset)
            offset = offset + jnp.sum(tot, axis=0, keepdims=True)
        return jnp.concatenate(outs, axis=0)

    gt = bits > thr
    eq = bits == thr
    need = (cap - count(gt)).astype(F32)
    sel = gt | (eq & (prefix(eq) < need))
    csx = prefix(sel)
    pos_ref[...] = jnp.where(sel, csx.astype(I32), -1)
    gate_ref[...] = jnp.where(sel, aff, 0.0)
    csx_ref[...] = csx.astype(I32)


def _route(aff_packed, cap):
    rows = aff_packed.shape[0]
    vm = pl.BlockSpec(memory_space=pltpu.VMEM)
    return pl.pallas_call(
        functools.partial(_route_kernel, cap=cap),
        in_specs=[vm],
        out_specs=[vm, vm, vm],
        out_shape=[jax.ShapeDtypeStruct((rows, LANES), I32), jax.ShapeDtypeStruct((rows, LANES), F32),
                   jax.ShapeDtypeStruct((rows, LANES), I32)],
        compiler_params=pltpu.CompilerParams(vmem_limit_bytes=VMEM_LIMIT),
        name="route",
    )(aff_packed)


def _window_start(first_slot, align, w, cap):
    return jnp.minimum((first_slot // align) * align, cap - w)


def _gather_kernel(p0_ref, pn_ref, pos_ref, h2_ref, xg_ref, *, cap, w):
    ne = xg_ref.shape[0]
    e0 = pl.program_id(0) * ne
    i = pl.program_id(1)
    tg = h2_ref.shape[0]

    @pl.when(i == 0)
    def _():
        xg_ref[...] = jnp.zeros(xg_ref.shape, BF16)

    row_id = lax.broadcasted_iota(I32, (w, tg), 0)
    first = [pl.multiple_of(_window_start(p0_ref[e0 + j, i], BF16_ROWS, w, cap), BF16_ROWS) for j in range(ne)]
    prows = [pos_ref[j, pl.ds(i, 1), :] for j in range(ne)]

    onehot = jnp.concatenate([(prows[j] == row_id + first[j]).astype(BF16) for j in range(ne)], axis=0)
    rows = jnp.dot(onehot, h2_ref[...], preferred_element_type=F32).astype(BF16)
    for j in range(ne):
        xg_ref[j, pl.ds(first[j], w), :] += rows[j * w:(j + 1) * w]

    for j in range(ne):
        n_win = (pn_ref[e0 + j, i] - first[j] + w - 1) // w

        def window(k, carry, j=j):
            nominal = first[j] + k * w
            start = pl.multiple_of(jnp.minimum(nominal, cap - w), BF16_ROWS)
            hit = (prows[j] == row_id + start) & (prows[j] >= nominal)
            more = jnp.dot(hit.astype(BF16), h2_ref[...], preferred_element_type=F32).astype(BF16)
            xg_ref[j, pl.ds(start, w), :] += more
            return carry

        lax.fori_loop(1, n_win, window, 0)


def _gather(p0, pn, pos_t, h2, cap):
    t = h2.shape[0]
    tg, ne = GATHER_TG, GATHER_NE
    return pl.pallas_call(
        functools.partial(_gather_kernel, cap=cap, w=GATHER_W),
        grid_spec=pltpu.PrefetchScalarGridSpec(
            num_scalar_prefetch=2,
            grid=(N_EXPERTS // ne, t // tg),
            in_specs=[pl.BlockSpec((ne, t // tg, tg), lambda e, i, *_: (e, 0, 0)),
                      pl.BlockSpec((tg, D_MODEL), lambda e, i, *_: (i, 0))],
            out_specs=pl.BlockSpec((ne, cap, D_MODEL), lambda e, i, *_: (e, 0, 0))),
        out_shape=jax.ShapeDtypeStruct((N_EXPERTS, cap, D_MODEL), BF16),
        compiler_params=_cparams(("parallel", "arbitrary")),
        name="gather",
    )(p0, pn, pos_t, h2)


def _ffn_kernel(xg_ref, wg_ref, wu_ref, wd_ref, y_ref, acc_scr):
    f = pl.program_id(1)

    @pl.when((pl.program_id(0) == 0) & (f == 0))
    def _():
        acc_scr[...] = jnp.zeros(acc_scr.shape, F32)

    xg = xg_ref[...]
    a = jnp.dot(xg, wg_ref[...].astype(BF16), preferred_element_type=F32)
    b = jnp.dot(xg, wu_ref[...].astype(BF16), preferred_element_type=F32)
    act = (a * jax.nn.sigmoid(a) * b).astype(BF16)
    part = jnp.dot(act, wd_ref[...].astype(BF16), preferred_element_type=F32)
    acc_scr[...] = jnp.where(f == 0, part, acc_scr[...] + part)

    @pl.when(f == pl.num_programs(1) - 1)
    def _():
        y_ref[...] = acc_scr[...].astype(BF16)


def _ffn(xg, w_gate, w_up, w_down):
    _, cap, _ = xg.shape
    tf = FFN_TF
    return pl.pallas_call(
        _ffn_kernel,
        grid=(N_EXPERTS, D_FF // tf),
        in_specs=[pl.BlockSpec((None, cap, D_MODEL), lambda e, f: (e, 0, 0)),
                  pl.BlockSpec((None, D_MODEL, tf), lambda e, f: (e, 0, f)),
                  pl.BlockSpec((None, D_MODEL, tf), lambda e, f: (e, 0, f)),
                  pl.BlockSpec((None, tf, D_MODEL), lambda e, f: (e, f, 0))],
        out_specs=pl.BlockSpec((None, cap, D_MODEL), lambda e, f: (e, 0, 0)),
        out_shape=jax.ShapeDtypeStruct((N_EXPERTS, cap, D_MODEL), BF16),
        scratch_shapes=[pltpu.VMEM((cap, D_MODEL), F32)],
        compiler_params=_cparams(("arbitrary", "arbitrary")),
        name="ffn",
    )(xg, w_gate, w_up, w_down)


def _combine_kernel(p0_ref, pn_ref, x1_ref, pos_ref, gate_ref, y_hbm, o_ref, win, xwin, sem, xsem, *, cap):
    i = pl.program_id(0)
    tt = x1_ref.shape[0]
    w = xwin.shape[0]

    def first_window(e):
        return _window_start(p0_ref[i, e], BF16_ROWS, w, cap)

    def window_copy(e):
        return pltpu.make_async_copy(y_hbm.at[e, pl.ds(first_window(e), w), :],
                                     win.at[pl.ds(e * w, w), :], sem.at[e])

    for e in range(N_EXPERTS):
        window_copy(e).start()

    lane = lax.broadcasted_iota(I32, (tt, w), 1)

    def weights(e, first_slot, lo_slot):
        pcol = pos_ref[:, e:e + 1]
        hit = (pcol - first_slot == lane) & (pcol >= lo_slot)
        return jnp.where(hit, gate_ref[:, e:e + 1], 0.0).astype(BF16)

    acc = x1_ref[...]
    for g0 in range(0, N_EXPERTS, COMB_GROUP):
        group = range(g0, g0 + COMB_GROUP)
        c = jnp.concatenate([weights(e, first_window(e), 0) for e in group], axis=1)
        for e in group:
            window_copy(e).wait()
        acc = acc + jnp.dot(c, win[g0 * w:(g0 + COMB_GROUP) * w, :], preferred_element_type=F32)
    o_ref[...] = acc

    for e in range(N_EXPERTS):
        w0 = first_window(e)
        n_win = (pn_ref[i, e] - w0 + w - 1) // w

        def extra(k, carry, e=e, w0=w0):
            nominal = w0 + k * w
            start = jnp.minimum(nominal, cap - w)
            cp = pltpu.make_async_copy(y_hbm.at[e, pl.ds(start, w), :], xwin, xsem)
            cp.start()
            cp.wait()
            o_ref[...] += jnp.dot(weights(e, start, nominal), xwin[...], preferred_element_type=F32)
            return carry

        lax.fori_loop(1, n_win, extra, 0)


def _combine(p0, pn, x1, pos_tok, gate_tok, y, cap):
    t = x1.shape[0]
    tt, w = COMB_TT, COMB_W
    row = lambda i, *_: (i, 0)
    return pl.pallas_call(
        functools.partial(_combine_kernel, cap=cap),
        grid_spec=pltpu.PrefetchScalarGridSpec(
            num_scalar_prefetch=2,
            grid=(t // tt,),
            in_specs=[pl.BlockSpec((tt, D_MODEL), row), pl.BlockSpec((tt, N_EXPERTS), row),
                      pl.BlockSpec((tt, N_EXPERTS), row), pl.BlockSpec(memory_space=pl.ANY)],
            out_specs=pl.BlockSpec((tt, D_MODEL), row),
            scratch_shapes=[pltpu.VMEM((N_EXPERTS * w, D_MODEL), BF16), pltpu.VMEM((w, D_MODEL), BF16),
                            pltpu.SemaphoreType.DMA((N_EXPERTS,)), pltpu.SemaphoreType.DMA(())]),
        out_shape=jax.ShapeDtypeStruct((t, D_MODEL), F32),
        compiler_params=_cparams(("arbitrary",)),
        name="combine",
    )(p0, pn, x1, pos_tok, gate_tok, y)


def _score_bound(width, q_gain, k_gain, max_bias):
    gmax = lambda g: jnp.max(jnp.abs(g.astype(F32)))
    bound = BOUND_SLACK * (width ** 0.5 * LOG2E * gmax(q_gain) * gmax(k_gain) + LOG2E * max_bias)
    return jnp.stack([bound, (3.0 * bound < F32_EXP_RANGE).astype(F32)]).astype(F32)


def _layer_params(l, rel_bias, attn_norm, w_in, diff_q_norm, diff_k_norm, lambda_q1, lambda_k1, lambda_q2,
                  lambda_k2, diff_subln, mla_q_latent_norm, mla_kv_latent_norm, w_uq, w_ukv, mla_q_norm,
                  mla_k_norm, w_out, ffn_norm, w_router):
    row = lambda v: v.reshape(1, -1).astype(F32)
    w = w_in[l]
    o = 3 * DIFF_W
    pad_rope = MLA_NOPE, LANES - MLA_QK
    w_kpe = jnp.pad(w[:, o + Q_LORA + KV_LORA:], ((0, 0), pad_rope))
    uq = jnp.pad(w_uq[l].reshape(Q_LORA, MLA_HEADS, MLA_QK), ((0, 0), (0, 0), (0, LANES - MLA_QK)))
    ukv = w_ukv[l].reshape(KV_LORA, MLA_HEADS, MLA_NOPE + MLA_V)
    uk = jnp.pad(ukv[:, :, :MLA_NOPE], ((0, 0), (0, 0), (0, LANES - MLA_NOPE)))
    head_gain = lambda g: jnp.pad(g.astype(F32), (0, LANES - MLA_QK)).reshape(1, LANES)
    half = MLA_ROPE // 2

    def swap_rope_halves(a):
        z = jnp.zeros_like(a)
        x1, x2 = a[..., MLA_NOPE:MLA_NOPE + half], a[..., MLA_NOPE + half:MLA_QK]
        return jnp.concatenate([z[..., :MLA_NOPE], x2, x1, z[..., MLA_QK:]], axis=-1)

    wr = jnp.pad(w_router[l].astype(F32), ((0, 0), (0, LANES - N_EXPERTS)))
    wr_hi = wr.astype(BF16)
    return dict(
        attn_norm=row(attn_norm[l]),
        w_qkv=w[:, :o].astype(BF16), w_cq=w[:, o:o + Q_LORA].astype(BF16),
        w_ckv=w[:, o + Q_LORA:o + Q_LORA + KV_LORA].astype(BF16), w_kpe=w_kpe.astype(BF16),
        w_uq=uq.reshape(Q_LORA, MLA_W).astype(BF16), w_uk=uk.reshape(KV_LORA, MLA_W).astype(BF16),
        w_uq_sw=swap_rope_halves(uq).reshape(Q_LORA, MLA_W).astype(BF16),
        mq_gain_sw=swap_rope_halves(head_gain(mla_q_norm[l])),
        w_uv=ukv[:, :, MLA_NOPE:].reshape(KV_LORA, MLA_HEADS * MLA_V).astype(BF16),
        dq_gain=row(jnp.tile(diff_q_norm[l], 2)), dk_gain=row(jnp.tile(diff_k_norm[l], 2)),
        cq_gain=row(mla_q_latent_norm[l]), ckv_gain=row(mla_kv_latent_norm[l]),
        mq_gain=head_gain(mla_q_norm[l]), mk_gain=head_gain(mla_k_norm[l]),
        subln=row(diff_subln[l]), lq1=row(lambda_q1[l]), lk1=row(lambda_k1[l]),
        lq2=row(lambda_q2[l]), lk2=row(lambda_k2[l]),
        rel_bias_t=rel_bias.astype(F32).T,
        diff_bound=_score_bound(DIFF_DK, diff_q_norm[l], diff_k_norm[l], jnp.max(jnp.abs(rel_bias))),
        mla_bound=_score_bound(MLA_QK, mla_q_norm[l], mla_k_norm[l], 0.0),
        w_o1=w_out[l][:DIFF_W].astype(BF16), w_o2=w_out[l][DIFF_W:].astype(BF16),
        ffn_norm=row(ffn_norm[l]),
        w_r=jnp.concatenate([wr_hi, (wr - wr_hi.astype(F32)).astype(BF16)], axis=1),
    )


def _rope_tables(seq):
    half = MLA_ROPE // 2
    inv = 1.0 / (ROPE_BASE ** (jnp.arange(half, dtype=F32) / half))
    ang = jnp.arange(seq, dtype=jnp.int32).astype(F32)[:, None] * inv[None, :]
    cos, sin = jnp.cos(ang), jnp.sin(ang)
    ones = jnp.ones((seq, MLA_NOPE), F32)
    tail = LANES - MLA_QK
    cos_t = jnp.concatenate([ones, cos, cos, jnp.ones((seq, tail), F32)], axis=1)
    sin_t = jnp.concatenate([0 * ones, -sin, sin, jnp.zeros((seq, tail), F32)], axis=1)
    return cos_t, sin_t


def _layer(x, p, bias, rope, w_gate, w_up, w_down, lam_init):
    b, s, d = x.shape
    t = b * s
    cap = CAPACITY_FACTOR * t // N_EXPERTS
    x2d = x.reshape(t, d)

    qd, kd, vd, qm, km, vm = _prep(x2d, s, rope[0], rope[1], p)
    shp = lambda a: a.reshape(b, s, a.shape[-1])
    keys_t = lambda a: jnp.swapaxes(shp(a), 1, 2)
    od = _diff_attention(shp(qd), keys_t(kd), shp(vd), bias, p["rel_bias_t"], p, lam_init)
    om = _mla_attention(shp(qm), keys_t(km), shp(vm), p["mla_bound"])
    x1, h2, aff = _outproj(x2d, od.reshape(t, -1), om.reshape(t, -1), p)

    groups = LANES // N_EXPERTS
    pos_p, gate_p, csx_p = _route(aff.reshape(t // groups, LANES), cap)
    pos_tok = pos_p.reshape(t, N_EXPERTS)
    gate_tok = gate_p.reshape(t, N_EXPERTS)
    csx_tok = csx_p.reshape(t, N_EXPERTS)

    def tile_slots(tile):
        first = csx_tok[::tile]
        return first, jnp.concatenate([first[1:], jnp.full((1, N_EXPERTS), cap, I32)], axis=0)

    g0, gn = tile_slots(GATHER_TG)
    pos_t = pos_tok.T.reshape(N_EXPERTS, t // GATHER_TG, GATHER_TG)
    xg = _gather(g0.T, gn.T, pos_t, h2, cap)
    y = _ffn(xg, w_gate, w_up, w_down)
    c0, cn = tile_slots(COMB_TT)
    out = _combine(c0, cn, x1, pos_tok, gate_tok, y, cap)
    return out.reshape(b, s, d)


def kernel(x_prompt, x_sample, rel_bias, attn_norm, w_in, diff_q_norm, diff_k_norm, lambda_q1, lambda_k1, lambda_q2, lambda_k2, diff_subln, mla_q_latent_norm, mla_kv_latent_norm, w_uq, w_ukv, mla_q_norm, mla_k_norm, w_out, ffn_norm, w_router, w_gate, w_up, w_down):
    params = [_layer_params(l, rel_bias, attn_norm, w_in, diff_q_norm, diff_k_norm, lambda_q1, lambda_k1,
                            lambda_q2, lambda_k2, diff_subln, mla_q_latent_norm, mla_kv_latent_norm, w_uq,
                            w_ukv, mla_q_norm, mla_k_norm, w_out, ffn_norm, w_router) for l in range(DEPTH)]
    biases = [_bias_tiles(p["rel_bias_t"], ATT_T) for p in params]
    rope = _rope_tables(max(x_prompt.shape[1], x_sample.shape[1]))
    outs = []
    for x in (x_prompt, x_sample):
        for l in range(DEPTH):
            lam_init = 0.8 - 0.6 * math.exp(-0.3 * l)
            x = _layer(x, params[l], biases[l], rope, w_gate[l], w_up[l], w_down[l], lam_init)
        outs.append(x)
    return tuple(outs)
```

```python
import functools
import math

import jax
import jax.numpy as jnp
from jax import lax
from jax.experimental import pallas as pl
from jax.experimental.pallas import tpu as pltpu

F32 = jnp.float32
BF16 = jnp.bfloat16
I32 = jnp.int32

D_MODEL = 1024
DEPTH = 1
DIFF_HEADS = 4
DIFF_DK = 64
DIFF_DV = 128
MLA_HEADS = 8
MLA_NOPE = 64
MLA_ROPE = 32
MLA_QK = MLA_NOPE + MLA_ROPE
MLA_V = 64
Q_LORA = 256
KV_LORA = 128
ROPE_BASE = 10000.0
N_BUCKETS = 32
MAX_DISTANCE = 128
N_EXPERTS = 16
CAPACITY_FACTOR = 2
D_FF = 2816
EPS = 1e-6

LANES = 128
SUBLANES = 8
BF16_ROWS = 16
DIFF_W = DIFF_HEADS * DIFF_DV
MLA_W = MLA_HEADS * LANES
NEG_BIG = -1e30
LOG2E = math.log2(math.e)
VMEM_LIMIT = 56 * 1024 * 1024
F32_EXP_RANGE = 120.0
BOUND_SLACK = 1.02

PREP_TM = 512
ATT_T = 512
DIFF_TK = 4096
BIAS_TILES = 5
MLA_TQ = 512
MLA_TK = 4096
OUT_TM = 512
ROUTE_RB = 256
GATHER_TG = 1024
GATHER_W = 192
GATHER_NE = 4
FFN_TF = 256
COMB_TT = 512
COMB_W = 128
COMB_GROUP = 4


def _cparams(sem, vmem=VMEM_LIMIT):
    return pltpu.CompilerParams(dimension_semantics=sem, vmem_limit_bytes=vmem)


def _rms_rows(x, gain):
    ms = jnp.mean(x * x, axis=-1, keepdims=True)
    return x * lax.rsqrt(ms + EPS) * gain


def _prep_kernel(x_ref, cos_ref, sin_ref, an_ref, wqkv_ref, wcq_ref, wckv_ref, wkpe_ref,
                 wuq_ref, wuqs_ref, wuk_ref, wuv_ref, dqg_ref, dkg_ref, cqg_ref, ckvg_ref, mqg_ref, mqgs_ref,
                 mkg_ref, qd_ref, kd_ref, vd_ref, qm_ref, km_ref, vm_ref):
    tm = x_ref.shape[0]
    hb = _rms_rows(x_ref[...], an_ref[...]).astype(BF16)

    lane = lax.broadcasted_iota(I32, (tm, LANES), 1)
    lo = lane < DIFF_DK

    cq = _rms_rows(jnp.dot(hb, wcq_ref[...], preferred_element_type=F32), cqg_ref[...]).astype(BF16)
    ckv = _rms_rows(jnp.dot(hb, wckv_ref[...], preferred_element_type=F32), ckvg_ref[...]).astype(BF16)
    kpe = jnp.dot(hb, wkpe_ref[...], preferred_element_type=F32)
    q = jnp.dot(cq, wuq_ref[...], preferred_element_type=F32)
    q_sw = jnp.dot(cq, wuqs_ref[...], preferred_element_type=F32)
    k = jnp.dot(ckv, wuk_ref[...], preferred_element_type=F32)
    vm_ref[...] = jnp.dot(ckv, wuv_ref[...], preferred_element_type=F32).astype(BF16)
    qkv = jnp.dot(hb, wqkv_ref[...], preferred_element_type=F32)

    cosb = cos_ref[...]
    sinb = sin_ref[...]
    mla_scale = MLA_QK ** -0.5 * LOG2E
    first_half = (lane >= MLA_NOPE) & (lane < MLA_NOPE + MLA_ROPE // 2)
    kpe_g = kpe * mkg_ref[...]
    kpe_rot = kpe_g * cosb + sinb * jnp.where(first_half,
                                              pltpu.roll(kpe_g, LANES - MLA_ROPE // 2, 1),
                                              pltpu.roll(kpe_g, MLA_ROPE // 2, 1))
    kpe_ss = jnp.sum(kpe * kpe, axis=-1, keepdims=True)

    for hd in range(MLA_HEADS):
        sl = slice(hd * LANES, (hd + 1) * LANES)
        qb = q[:, sl]
        rq = lax.rsqrt(jnp.sum(qb * qb, axis=-1, keepdims=True) / MLA_QK + EPS) * mla_scale
        qm_ref[:, sl] = (rq * (qb * mqg_ref[...] * cosb + q_sw[:, sl] * mqgs_ref[...] * sinb)).astype(BF16)
        kb = k[:, sl]
        rk = lax.rsqrt((jnp.sum(kb * kb, axis=-1, keepdims=True) + kpe_ss) / MLA_QK + EPS)
        kn = rk * (kb * mkg_ref[...] + kpe_rot)
        km_ref[:, sl] = jnp.where(lane == MLA_QK, 1.0, kn).astype(BF16)

    diff_scale = DIFF_DK ** -0.5 * LOG2E
    for hd in range(DIFF_HEADS):
        for off, g_ref, o_ref, scale in ((0, dqg_ref, qd_ref, diff_scale), (DIFF_W, dkg_ref, kd_ref, None)):
            blk = qkv[:, off + hd * LANES: off + (hd + 1) * LANES]
            sq = blk * blk
            s_lo = jnp.sum(jnp.where(lo, sq, 0.0), axis=-1, keepdims=True)
            s_hi = jnp.sum(jnp.where(lo, 0.0, sq), axis=-1, keepdims=True)
            r = jnp.where(lo, lax.rsqrt(s_lo / DIFF_DK + EPS), lax.rsqrt(s_hi / DIFF_DK + EPS))
            y = blk * r * g_ref[...]
            if scale is not None:
                y = y * scale
            o_ref[:, hd * LANES:(hd + 1) * LANES] = y.astype(BF16)
    vd_ref[...] = qkv[:, 2 * DIFF_W:3 * DIFF_W].astype(BF16)


def _prep(x2d, seq, cos_t, sin_t, p):
    t = x2d.shape[0]
    tm = PREP_TM
    nseq = seq // tm
    row = lambda i: (i, 0)
    full = lambda i: (0, 0)
    wspec = lambda a: pl.BlockSpec(a.shape, full)
    weights = (p["attn_norm"], p["w_qkv"], p["w_cq"], p["w_ckv"], p["w_kpe"], p["w_uq"], p["w_uq_sw"], p["w_uk"],
               p["w_uv"], p["dq_gain"], p["dk_gain"], p["cq_gain"], p["ckv_gain"], p["mq_gain"], p["mq_gain_sw"],
               p["mk_gain"])
    out_w = (DIFF_W, DIFF_W, DIFF_W, MLA_W, MLA_W, MLA_HEADS * MLA_V)
    return pl.pallas_call(
        _prep_kernel,
        grid=(t // tm,),
        in_specs=[pl.BlockSpec((tm, D_MODEL), row),
                  pl.BlockSpec((tm, LANES), lambda i: (i % nseq, 0)),
                  pl.BlockSpec((tm, LANES), lambda i: (i % nseq, 0))] + [wspec(w) for w in weights],
        out_specs=[pl.BlockSpec((tm, w), row) for w in out_w],
        out_shape=[jax.ShapeDtypeStruct((t, w), BF16) for w in out_w],
        compiler_params=_cparams(("parallel",)),
        name="prep",
    )(x2d, cos_t, sin_t, *weights)


def _bias_kernel(rb_ref, o_ref):
    hd = pl.program_id(0)
    d = pl.program_id(1)
    tq, tk = o_ref.shape
    row = lax.broadcasted_iota(I32, (tq, tk), 0)
    col = lax.broadcasted_iota(I32, (tq, tk), 1)
    rel = col - row + (d - BIAS_TILES // 2) * tk
    half = N_BUCKETS // 2
    max_exact = half // 2
    n = jnp.abs(rel)
    nf = jnp.maximum(n, 1).astype(F32)
    large = max_exact + (jnp.log(nf / max_exact) / math.log(MAX_DISTANCE / max_exact)
                         * (half - max_exact)).astype(I32)
    large = jnp.minimum(large, half - 1)
    bucket = jnp.where(rel > 0, half, 0) + jnp.where(n < max_exact, n, large)
    acc = jnp.zeros((tq, tk), F32)
    for b in range(N_BUCKETS):
        acc = jnp.where(bucket == b, rb_ref[hd, b], acc)
    o_ref[...] = acc * LOG2E


def _bias_tiles(rel_bias_t, t):
    return pl.pallas_call(
        _bias_kernel,
        grid=(DIFF_HEADS, BIAS_TILES),
        in_specs=[pl.BlockSpec(memory_space=pltpu.SMEM)],
        out_specs=pl.BlockSpec((None, None, t, t), lambda h, d: (h, d, 0, 0)),
        out_shape=jax.ShapeDtypeStruct((DIFF_HEADS, BIAS_TILES, t, t), F32),
        compiler_params=_cparams(("parallel", "parallel")),
        name="bias_tiles",
    )(rel_bias_t)


def _online_update(mi, s, v, m_scr, l_scr, acc_scr):
    blocks = [s[:, i:i + LANES] for i in range(0, s.shape[1], LANES)]
    m_old = m_scr[mi]
    row_max = jnp.max(functools.reduce(jnp.maximum, blocks), axis=-1, keepdims=True)
    m_new = jnp.maximum(m_old, row_max)
    alpha = jnp.exp2(m_old - m_new)
    ps = [jnp.exp2(blk - m_new) for blk in blocks]
    l_scr[mi] = alpha * l_scr[mi] + functools.reduce(jnp.add, ps)
    p = jnp.concatenate([x.astype(BF16) for x in ps], axis=1)
    acc_scr[mi] = alpha * acc_scr[mi] + jnp.dot(p, v, preferred_element_type=F32)
    m_scr[mi] = m_new


def _fixed_ref_update(mi, s, v, l_scr, acc_scr):
    ps = [jnp.exp2(s[:, i:i + LANES]) for i in range(0, s.shape[1], LANES)]
    l_scr[mi] += functools.reduce(jnp.add, ps)
    p = jnp.concatenate([x.astype(BF16) for x in ps], axis=1)
    acc_scr[mi] += jnp.dot(p, v, preferred_element_type=F32)


def _normalised(mi, l_scr, acc_scr):
    return acc_scr[mi] / jnp.sum(l_scr[mi], axis=-1, keepdims=True)


def _diff_attn_kernel(rb_ref, bound_ref, q_ref, k_ref, v_ref, bias_ref, subln_ref, lq1_ref, lk1_ref, lq2_ref,
                      lk2_ref, o_ref, m_scr, l_scr, acc_scr, *, lam_init, tk):
    hd = pl.program_id(1)
    qi = pl.program_id(2)
    t = q_ref.shape[0]
    n = v_ref.shape[0] // t
    r = tk // t
    bound = bound_ref[0]
    fixed_ref = bound_ref[1] > 0.5

    q = q_ref[...]
    lane = lax.broadcasted_iota(I32, q.shape, 1)
    zero = jnp.zeros_like(q)
    qs = (jnp.where(lane < DIFF_DK, q, zero), jnp.where(lane < DIFF_DK, zero, q))

    l_scr[...] = jnp.zeros(l_scr.shape, F32)
    acc_scr[...] = jnp.zeros(acc_scr.shape, F32)

    lo = jnp.maximum(qi - 1, 0) // r
    hi = (jnp.minimum(qi + 2, n) + r - 1) // r
    c_left = rb_ref[hd, N_BUCKETS // 2 - 1] * LOG2E
    c_right = rb_ref[hd, N_BUCKETS - 1] * LOG2E
    n_side = bias_ref.shape[0] // 2
    LEFT, BAND, RIGHT = range(3)

    def sweep(update, after_left=lambda: None, after_band=lambda: None):
        def make_step(phase):
            def step(j, carry):
                off = pl.multiple_of(j * tk, tk)
                k = k_ref[:, pl.ds(off, tk)]
                v = v_ref[pl.ds(off, tk), :]
                for mi in range(2):
                    s = jnp.dot(qs[mi], k, preferred_element_type=F32)
                    if phase == BAND:
                        pieces = []
                        for c in range(r):
                            d = jnp.clip(j * r + c - qi, -n_side, n_side)
                            pieces.append(s[:, c * t:(c + 1) * t] + bias_ref[d + n_side])
                        s = jnp.concatenate(pieces, axis=1)
                    update(phase, mi, s, v)
                return carry
            return step

        lax.fori_loop(0, lo, make_step(LEFT), 0)
        after_left()
        lax.fori_loop(lo, hi, make_step(BAND), 0)
        after_band()
        lax.fori_loop(hi, n // r, make_step(RIGHT), 0)

    @pl.when(fixed_ref)
    def _():
        ref = (bound, bound + c_left, bound + c_left - c_right)
        sweep(lambda phase, mi, s, v: _fixed_ref_update(mi, s - ref[phase], v, l_scr, acc_scr))

    @pl.when(jnp.logical_not(fixed_ref))
    def _():
        m_scr[...] = jnp.full(m_scr.shape, NEG_BIG, F32)

        def shift(c):
            m_scr[...] = m_scr[...] + c

        sweep(lambda phase, mi, s, v: _online_update(mi, s, v, m_scr, l_scr, acc_scr),
              lambda: shift(c_left), lambda: shift(-c_right))

    lam = (jnp.exp(jnp.sum(lq1_ref[...] * lk1_ref[...], axis=-1, keepdims=True))
           - jnp.exp(jnp.sum(lq2_ref[...] * lk2_ref[...], axis=-1, keepdims=True)) + lam_init)
    o = _normalised(0, l_scr, acc_scr) - lam * _normalised(1, l_scr, acc_scr)
    o_ref[...] = (_rms_rows(o, subln_ref[...]) * (1.0 - lam_init)).astype(BF16)


def _diff_attention(qd, kd, vd, bias_tiles, rel_bias_t, p, lam_init):
    b, s, _ = qd.shape
    t = ATT_T
    vec = lambda a: pl.BlockSpec(a.shape, lambda bi, h, i: (0, 0))
    return pl.pallas_call(
        functools.partial(_diff_attn_kernel, lam_init=lam_init, tk=DIFF_TK),
        grid=(b, DIFF_HEADS, s // t),
        in_specs=[pl.BlockSpec(memory_space=pltpu.SMEM), pl.BlockSpec(memory_space=pltpu.SMEM),
                  pl.BlockSpec((None, t, LANES), lambda bi, h, i: (bi, i, h)),
                  pl.BlockSpec((None, LANES, s), lambda bi, h, i: (bi, h, 0)),
                  pl.BlockSpec((None, s, LANES), lambda bi, h, i: (bi, 0, h)),
                  pl.BlockSpec((None, BIAS_TILES, t, t), lambda bi, h, i: (h, 0, 0, 0)),
                  vec(p["subln"]), vec(p["lq1"]), vec(p["lk1"]), vec(p["lq2"]), vec(p["lk2"])],
        out_specs=pl.BlockSpec((None, t, LANES), lambda bi, h, i: (bi, i, h)),
        out_shape=jax.ShapeDtypeStruct((b, s, DIFF_W), BF16),
        scratch_shapes=[pltpu.VMEM((2, t, LANES), F32)] * 3,
        compiler_params=_cparams(("parallel", "parallel", "arbitrary")),
        name="diff_attn",
    )(rel_bias_t, p["diff_bound"], qd, kd, vd, bias_tiles, p["subln"], p["lq1"], p["lk1"], p["lq2"], p["lk2"])


def _mla_attn_kernel(bound_ref, q_ref, k_ref, v_ref, o_ref, m_scr, l_scr, acc_scr, *, tk):
    t = q_ref.shape[0]
    n = v_ref.shape[0] // tk
    bound = bound_ref[0]
    fixed_ref = bound_ref[1] > 0.5

    l_scr[...] = jnp.zeros(l_scr.shape, F32)
    acc_scr[...] = jnp.zeros(acc_scr.shape, F32)

    def sweep(q, update, unroll=False):
        qs = (q[:, :LANES], q[:, LANES:])

        def step(j, carry):
            off = pl.multiple_of(j * tk, tk)
            v = v_ref[pl.ds(off, tk), :]
            for mi in range(2):
                k = k_ref[mi * LANES:(mi + 1) * LANES, pl.ds(off, tk)]
                update(mi, jnp.dot(qs[mi], k, preferred_element_type=F32), v)
            return carry

        lax.fori_loop(0, n, step, 0, unroll=unroll)

    @pl.when(fixed_ref)
    def _():
        q = q_ref[...]
        lane = lax.broadcasted_iota(I32, q.shape, 1)
        q = jnp.where(lane % LANES == MLA_QK, jnp.full(q.shape, -bound, F32).astype(BF16), q)
        sweep(q, lambda mi, s, v: _fixed_ref_update(mi, s, v, l_scr, acc_scr), unroll=True)

    @pl.when(jnp.logical_not(fixed_ref))
    def _():
        m_scr[...] = jnp.full(m_scr.shape, NEG_BIG, F32)
        sweep(q_ref[...], lambda mi, s, v: _online_update(mi, s, v, m_scr, l_scr, acc_scr))

    lane = lax.broadcasted_iota(I32, (t, LANES), 1)
    o = jnp.where(lane < MLA_V, _normalised(0, l_scr, acc_scr), _normalised(1, l_scr, acc_scr))
    o_ref[...] = o.astype(BF16)


def _mla_attention(qm, km, vm, bound):
    b, s, _ = qm.shape
    t = MLA_TQ
    return pl.pallas_call(
        functools.partial(_mla_attn_kernel, tk=MLA_TK),
        grid=(b, MLA_HEADS // 2, s // t),
        in_specs=[pl.BlockSpec(memory_space=pltpu.SMEM),
                  pl.BlockSpec((None, t, 2 * LANES), lambda bi, h, i: (bi, i, h)),
                  pl.BlockSpec((None, 2 * LANES, s), lambda bi, h, i: (bi, h, 0)),
                  pl.BlockSpec((None, s, LANES), lambda bi, h, i: (bi, 0, h))],
        out_specs=pl.BlockSpec((None, t, LANES), lambda bi, h, i: (bi, i, h)),
        out_shape=jax.ShapeDtypeStruct((b, s, MLA_HEADS * MLA_V), BF16),
        scratch_shapes=[pltpu.VMEM((2, t, LANES), F32)] * 3,
        compiler_params=_cparams(("parallel", "parallel", "arbitrary")),
        name="mla_attn",
    )(bound, qm, km, vm)


def _split_bf16(x):
    hi = x.astype(BF16)
    lo = (x - hi.astype(F32)).astype(BF16)
    return hi, lo


def _outproj_kernel(x_ref, od_ref, om_ref, wo1_ref, wo2_ref, fn_ref, wr_ref,
                    x1_ref, h2_ref, aff_ref):
    x1 = (x_ref[...]
          + jnp.dot(od_ref[...], wo1_ref[...], preferred_element_type=F32)
          + jnp.dot(om_ref[...], wo2_ref[...], preferred_element_type=F32))
    x1_ref[...] = x1
    h = _rms_rows(x1, fn_ref[...])
    hh, hl = _split_bf16(h)
    h2_ref[...] = hh
    both = jnp.dot(hh, wr_ref[...], preferred_element_type=F32)
    logits = both[:, :LANES] + both[:, LANES:] + jnp.dot(hl, wr_ref[:, :LANES], preferred_element_type=F32)
    lane = lax.broadcasted_iota(I32, logits.shape, 1)
    logits = jnp.where(lane < N_EXPERTS, logits, NEG_BIG)
    e = jnp.exp(logits - jnp.max(logits, axis=-1, keepdims=True))
    aff = e / jnp.sum(e, axis=-1, keepdims=True)
    aff_ref[...] = aff[:, :N_EXPERTS]


def _outproj(x2d, od, om, p):
    t = x2d.shape[0]
    tm = OUT_TM
    row = lambda i: (i, 0)
    wspec = lambda a: pl.BlockSpec(a.shape, lambda i: (0, 0))
    weights = (p["w_o1"], p["w_o2"], p["ffn_norm"], p["w_r"])
    return pl.pallas_call(
        _outproj_kernel,
        grid=(t // tm,),
        in_specs=[pl.BlockSpec((tm, D_MODEL), row), pl.BlockSpec((tm, DIFF_W), row),
                  pl.BlockSpec((tm, MLA_HEADS * MLA_V), row)] + [wspec(w) for w in weights],
        out_specs=[pl.BlockSpec((tm, D_MODEL), row), pl.BlockSpec((tm, D_MODEL), row),
                   pl.BlockSpec((tm, N_EXPERTS), row)],
        out_shape=[jax.ShapeDtypeStruct((t, D_MODEL), F32), jax.ShapeDtypeStruct((t, D_MODEL), BF16),
                   jax.ShapeDtypeStruct((t, N_EXPERTS), F32)],
        compiler_params=_cparams(("parallel",)),
        name="outproj",
    )(x2d, od, om, *weights)


def _route_kernel(aff_ref, pos_ref, gate_ref, csx_ref, *, cap):
    rows = aff_ref.shape[0]
    groups = LANES // N_EXPERTS
    aff = aff_ref[...]
    bits = pltpu.bitcast(aff, I32)

    def expert_total(v):
        for sh in (N_EXPERTS, 2 * N_EXPERTS, 4 * N_EXPERTS):
            v = v + pltpu.roll(v, sh, 1)
        return v

    def count(mask):
        return expert_total(jnp.sum(mask.astype(I32), axis=0, keepdims=True))

    def search(i, thr):
        cand = thr | jnp.left_shift(jnp.int32(1), 30 - i)
        return jnp.where(count(bits >= cand) >= cap, cand, thr)

    thr = lax.fori_loop(0, 31, search, jnp.zeros((1, LANES), I32))

    a = lax.broadcasted_iota(I32, (LANES, LANES), 0)
    b = lax.broadcasted_iota(I32, (LANES, LANES), 1)
    same = (a % N_EXPERTS) == (b % N_EXPERTS)
    q_all = same.astype(BF16)
    q_before = (same & (a // N_EXPERTS < b // N_EXPERTS)).astype(BF16)
    rb = ROUTE_RB
    ra = lax.broadcasted_iota(I32, (rb, rb), 0)
    ca = lax.broadcasted_iota(I32, (rb, rb), 1)
    tri = (ca < ra).astype(BF16)

    def prefix(mask):
        mb = mask.astype(BF16)
        outs = []
        offset = jnp.zeros((1, LANES), F32)
        for r0 in range(0, rows, rb):
            blk = mb[r0:r0 + rb]
            tot = jnp.dot(blk, q_all, preferred_element_type=F32)
            within = jnp.dot(blk, q_before, preferred_element_type=F32)
            above = jnp.dot(tri, tot.astype(BF16), preferred_element_type=F32)
            outs.append(above + within + offset)
            offset = offset + jnp.sum(tot, axis=0, keepdims=True)
        return jnp.concatenate(outs, axis=0)

    gt = bits > thr
    eq = bits == thr
    need = (cap - count(gt)).astype(F32)
    sel = gt | (eq & (prefix(eq) < need))
    csx = prefix(sel)
    pos_ref[...] = jnp.where(sel, csx.astype(I32), -1)
    gate_ref[...] = jnp.where(sel, aff, 0.0)
    csx_ref[...] = csx.astype(I32)


def _route(aff_packed, cap):
    rows = aff_packed.shape[0]
    vm = pl.BlockSpec(memory_space=pltpu.VMEM)
    return pl.pallas_call(
        functools.partial(_route_kernel, cap=cap),
        in_specs=[vm],
        out_specs=[vm, vm, vm],
        out_shape=[jax.ShapeDtypeStruct((rows, LANES), I32), jax.ShapeDtypeStruct((rows, LANES), F32),
                   jax.ShapeDtypeStruct((rows, LANES), I32)],
        compiler_params=pltpu.CompilerParams(vmem_limit_bytes=VMEM_LIMIT),
        name="route",
    )(aff_packed)


def _window_start(first_slot, align, w, cap):
    return jnp.minimum((first_slot // align) * align, cap - w)


def _gather_kernel(p0_ref, pn_ref, pos_ref, h2_ref, xg_ref, *, cap, w):
    ne = xg_ref.shape[0]
    e0 = pl.program_id(0) * ne
    i = pl.program_id(1)
    tg = h2_ref.shape[0]

    @pl.when(i == 0)
    def _():
        xg_ref[...] = jnp.zeros(xg_ref.shape, BF16)

    row_id = lax.broadcasted_iota(I32, (w, tg), 0)
    first = [pl.multiple_of(_window_start(p0_ref[e0 + j, i], BF16_ROWS, w, cap), BF16_ROWS) for j in range(ne)]
    prows = [pos_ref[j, pl.ds(i, 1), :] for j in range(ne)]

    onehot = jnp.concatenate([(prows[j] == row_id + first[j]).astype(BF16) for j in range(ne)], axis=0)
    rows = jnp.dot(onehot, h2_ref[...], preferred_element_type=F32).astype(BF16)
    for j in range(ne):
        xg_ref[j, pl.ds(first[j], w), :] += rows[j * w:(j + 1) * w]

    for j in range(ne):
        n_win = (pn_ref[e0 + j, i] - first[j] + w - 1) // w

        def window(k, carry, j=j):
            nominal = first[j] + k * w
            start = pl.multiple_of(jnp.minimum(nominal, cap - w), BF16_ROWS)
            hit = (prows[j] == row_id + start) & (prows[j] >= nominal)
            more = jnp.dot(hit.astype(BF16), h2_ref[...], preferred_element_type=F32).astype(BF16)
            xg_ref[j, pl.ds(start, w), :] += more
            return carry

        lax.fori_loop(1, n_win, window, 0)


def _gather(p0, pn, pos_t, h2, cap):
    t = h2.shape[0]
    tg, ne = GATHER_TG, GATHER_NE
    return pl.pallas_call(
        functools.partial(_gather_kernel, cap=cap, w=GATHER_W),
        grid_spec=pltpu.PrefetchScalarGridSpec(
            num_scalar_prefetch=2,
            grid=(N_EXPERTS // ne, t // tg),
            in_specs=[pl.BlockSpec((ne, t // tg, tg), lambda e, i, *_: (e, 0, 0)),
                      pl.BlockSpec((tg, D_MODEL), lambda e, i, *_: (i, 0))],
            out_specs=pl.BlockSpec((ne, cap, D_MODEL), lambda e, i, *_: (e, 0, 0))),
        out_shape=jax.ShapeDtypeStruct((N_EXPERTS, cap, D_MODEL), BF16),
        compiler_params=_cparams(("parallel", "arbitrary")),
        name="gather",
    )(p0, pn, pos_t, h2)


def _ffn_up_kernel(xg_ref, wg_ref, wu_ref, act_ref):
    xg = xg_ref[...]
    a = jnp.dot(xg, wg_ref[...].astype(BF16), preferred_element_type=F32)
    b = jnp.dot(xg, wu_ref[...].astype(BF16), preferred_element_type=F32)
    act_ref[...] = (a * jax.nn.sigmoid(a) * b).astype(BF16)


def _ffn_down_kernel(act_ref, wd_ref, y_ref):
    y_ref[...] = jnp.dot(act_ref[...], wd_ref[...].astype(BF16), preferred_element_type=F32).astype(BF16)


def _ffn(xg, w_gate, w_up, w_down):
    _, cap, _ = xg.shape
    tf = FFN_TF
    act = pl.pallas_call(
        _ffn_up_kernel,
        grid=(N_EXPERTS, D_FF // tf),
        in_specs=[pl.BlockSpec((None, cap, D_MODEL), lambda e, f: (e, 0, 0)),
                  pl.BlockSpec((None, D_MODEL, tf), lambda e, f: (e, 0, f)),
                  pl.BlockSpec((None, D_MODEL, tf), lambda e, f: (e, 0, f))],
        out_specs=pl.BlockSpec((None, cap, tf), lambda e, f: (e, 0, f)),
        out_shape=jax.ShapeDtypeStruct((N_EXPERTS, cap, D_FF), BF16),
        compiler_params=_cparams(("parallel", "parallel")),
        name="ffn_up",
    )(xg, w_gate, w_up)
    return pl.pallas_call(
        _ffn_down_kernel,
        grid=(N_EXPERTS, D_MODEL // tf),
        in_specs=[pl.BlockSpec((None, cap, D_FF), lambda e, c: (e, 0, 0)),
                  pl.BlockSpec((None, D_FF, tf), lambda e, c: (e, 0, c))],
        out_specs=pl.BlockSpec((None, cap, tf), lambda e, c: (e, 0, c)),
        out_shape=jax.ShapeDtypeStruct((N_EXPERTS, cap, D_MODEL), BF16),
        compiler_params=_cparams(("parallel", "parallel")),
        name="ffn_down",
    )(act, w_down)


def _combine_kernel(p0_ref, pn_ref, x1_ref, pos_ref, gate_ref, y_hbm, o_ref, win, xwin, sem, xsem, *, cap):
    i = pl.program_id(0)
    tt = x1_ref.shape[0]
    w = xwin.shape[0]

    def first_window(e):
        return _window_start(p0_ref[i, e], BF16_ROWS, w, cap)

    def window_copy(e):
        return pltpu.make_async_copy(y_hbm.at[e, pl.ds(first_window(e), w), :],
                                     win.at[pl.ds(e * w, w), :], sem.at[e])

    for e in range(N_EXPERTS):
        window_copy(e).start()

    lane = lax.broadcasted_iota(I32, (tt, w), 1)

    def weights(e, first_slot, lo_slot):
        pcol = pos_ref[:, e:e + 1]
        hit = (pcol - first_slot == lane) & (pcol >= lo_slot)
        return jnp.where(hit, gate_ref[:, e:e + 1], 0.0).astype(BF16)

    acc = x1_ref[...]
    for g0 in range(0, N_EXPERTS, COMB_GROUP):
        group = range(g0, g0 + COMB_GROUP)
        c = jnp.concatenate([weights(e, first_window(e), 0) for e in group], axis=1)
        for e in group:
            window_copy(e).wait()
        acc = acc + jnp.dot(c, win[g0 * w:(g0 + COMB_GROUP) * w, :], preferred_element_type=F32)
    o_ref[...] = acc

    for e in range(N_EXPERTS):
        w0 = first_window(e)
        n_win = (pn_ref[i, e] - w0 + w - 1) // w

        def extra(k, carry, e=e, w0=w0):
            nominal = w0 + k * w
            start = jnp.minimum(nominal, cap - w)
            cp = pltpu.make_async_copy(y_hbm.at[e, pl.ds(start, w), :], xwin, xsem)
            cp.start()
            cp.wait()
            o_ref[...] += jnp.dot(weights(e, start, nominal), xwin[...], preferred_element_type=F32)
            return carry

        lax.fori_loop(1, n_win, extra, 0)


def _combine(p0, pn, x1, pos_tok, gate_tok, y, cap):
    t = x1.shape[0]
    tt, w = COMB_TT, COMB_W
    row = lambda i, *_: (i, 0)
    return pl.pallas_call(
        functools.partial(_combine_kernel, cap=cap),
        grid_spec=pltpu.PrefetchScalarGridSpec(
            num_scalar_prefetch=2,
            grid=(t // tt,),
            in_specs=[pl.BlockSpec((tt, D_MODEL), row), pl.BlockSpec((tt, N_EXPERTS), row),
                      pl.BlockSpec((tt, N_EXPERTS), row), pl.BlockSpec(memory_space=pl.ANY)],
            out_specs=pl.BlockSpec((tt, D_MODEL), row),
            scratch_shapes=[pltpu.VMEM((N_EXPERTS * w, D_MODEL), BF16), pltpu.VMEM((w, D_MODEL), BF16),
                            pltpu.SemaphoreType.DMA((N_EXPERTS,)), pltpu.SemaphoreType.DMA(())]),
        out_shape=jax.ShapeDtypeStruct((t, D_MODEL), F32),
        compiler_params=_cparams(("arbitrary",)),
        name="combine",
    )(p0, pn, x1, pos_tok, gate_tok, y)


def _score_bound(width, q_gain, k_gain, max_bias):
    gmax = lambda g: jnp.max(jnp.abs(g.astype(F32)))
    bound = BOUND_SLACK * (width ** 0.5 * LOG2E * gmax(q_gain) * gmax(k_gain) + LOG2E * max_bias)
    return jnp.stack([bound, (3.0 * bound < F32_EXP_RANGE).astype(F32)]).astype(F32)


def _layer_params(l, rel_bias, attn_norm, w_in, diff_q_norm, diff_k_norm, lambda_q1, lambda_k1, lambda_q2,
                  lambda_k2, diff_subln, mla_q_latent_norm, mla_kv_latent_norm, w_uq, w_ukv, mla_q_norm,
                  mla_k_norm, w_out, ffn_norm, w_router):
    row = lambda v: v.reshape(1, -1).astype(F32)
    w = w_in[l]
    o = 3 * DIFF_W
    pad_rope = MLA_NOPE, LANES - MLA_QK
    w_kpe = jnp.pad(w[:, o + Q_LORA + KV_LORA:], ((0, 0), pad_rope))
    uq = jnp.pad(w_uq[l].reshape(Q_LORA, MLA_HEADS, MLA_QK), ((0, 0), (0, 0), (0, LANES - MLA_QK)))
    ukv = w_ukv[l].reshape(KV_LORA, MLA_HEADS, MLA_NOPE + MLA_V)
    uk = jnp.pad(ukv[:, :, :MLA_NOPE], ((0, 0), (0, 0), (0, LANES - MLA_NOPE)))
    head_gain = lambda g: jnp.pad(g.astype(F32), (0, LANES - MLA_QK)).reshape(1, LANES)
    half = MLA_ROPE // 2

    def swap_rope_halves(a):
        z = jnp.zeros_like(a)
        x1, x2 = a[..., MLA_NOPE:MLA_NOPE + half], a[..., MLA_NOPE + half:MLA_QK]
        return jnp.concatenate([z[..., :MLA_NOPE], x2, x1, z[..., MLA_QK:]], axis=-1)

    wr = jnp.pad(w_router[l].astype(F32), ((0, 0), (0, LANES - N_EXPERTS)))
    wr_hi = wr.astype(BF16)
    return dict(
        attn_norm=row(attn_norm[l]),
        w_qkv=w[:, :o].astype(BF16), w_cq=w[:, o:o + Q_LORA].astype(BF16),
        w_ckv=w[:, o + Q_LORA:o + Q_LORA + KV_LORA].astype(BF16), w_kpe=w_kpe.astype(BF16),
        w_uq=uq.reshape(Q_LORA, MLA_W).astype(BF16), w_uk=uk.reshape(KV_LORA, MLA_W).astype(BF16),
        w_uq_sw=swap_rope_halves(uq).reshape(Q_LORA, MLA_W).astype(BF16),
        mq_gain_sw=swap_rope_halves(head_gain(mla_q_norm[l])),
        w_uv=ukv[:, :, MLA_NOPE:].reshape(KV_LORA, MLA_HEADS * MLA_V).astype(BF16),
        dq_gain=row(jnp.tile(diff_q_norm[l], 2)), dk_gain=row(jnp.tile(diff_k_norm[l], 2)),
        cq_gain=row(mla_q_latent_norm[l]), ckv_gain=row(mla_kv_latent_norm[l]),
        mq_gain=head_gain(mla_q_norm[l]), mk_gain=head_gain(mla_k_norm[l]),
        subln=row(diff_subln[l]), lq1=row(lambda_q1[l]), lk1=row(lambda_k1[l]),
        lq2=row(lambda_q2[l]), lk2=row(lambda_k2[l]),
        rel_bias_t=rel_bias.astype(F32).T,
        diff_bound=_score_bound(DIFF_DK, diff_q_norm[l], diff_k_norm[l], jnp.max(jnp.abs(rel_bias))),
        mla_bound=_score_bound(MLA_QK, mla_q_norm[l], mla_k_norm[l], 0.0),
        w_o1=w_out[l][:DIFF_W].astype(BF16), w_o2=w_out[l][DIFF_W:].astype(BF16),
        ffn_norm=row(ffn_norm[l]),
        w_r=jnp.concatenate([wr_hi, (wr - wr_hi.astype(F32)).astype(BF16)], axis=1),
    )


def _rope_tables(seq):
    half = MLA_ROPE // 2
    inv = 1.0 / (ROPE_BASE ** (jnp.arange(half, dtype=F32) / half))
    ang = jnp.arange(seq, dtype=jnp.int32).astype(F32)[:, None] * inv[None, :]
    cos, sin = jnp.cos(ang), jnp.sin(ang)
    ones = jnp.ones((seq, MLA_NOPE), F32)
    tail = LANES - MLA_QK
    cos_t = jnp.concatenate([ones, cos, cos, jnp.ones((seq, tail), F32)], axis=1)
    sin_t = jnp.concatenate([0 * ones, -sin, sin, jnp.zeros((seq, tail), F32)], axis=1)
    return cos_t, sin_t


def _layer(x, p, bias, rope, w_gate, w_up, w_down, lam_init):
    b, s, d = x.shape
    t = b * s
    cap = CAPACITY_FACTOR * t // N_EXPERTS
    x2d = x.reshape(t, d)

    qd, kd, vd, qm, km, vm = _prep(x2d, s, rope[0], rope[1], p)
    shp = lambda a: a.reshape(b, s, a.shape[-1])
    keys_t = lambda a: jnp.swapaxes(shp(a), 1, 2)
    od = _diff_attention(shp(qd), keys_t(kd), shp(vd), bias, p["rel_bias_t"], p, lam_init)
    om = _mla_attention(shp(qm), keys_t(km), shp(vm), p["mla_bound"])
    x1, h2, aff = _outproj(x2d, od.reshape(t, -1), om.reshape(t, -1), p)

    groups = LANES // N_EXPERTS
    pos_p, gate_p, csx_p = _route(aff.reshape(t // groups, LANES), cap)
    pos_tok = pos_p.reshape(t, N_EXPERTS)
    gate_tok = gate_p.reshape(t, N_EXPERTS)
    csx_tok = csx_p.reshape(t, N_EXPERTS)

    def tile_slots(tile):
        first = csx_tok[::tile]
        return first, jnp.concatenate([first[1:], jnp.full((1, N_EXPERTS), cap, I32)], axis=0)

    g0, gn = tile_slots(GATHER_TG)
    pos_t = pos_tok.T.reshape(N_EXPERTS, t // GATHER_TG, GATHER_TG)
    xg = _gather(g0.T, gn.T, pos_t, h2, cap)
    y = _ffn(xg, w_gate, w_up, w_down)
    c0, cn = tile_slots(COMB_TT)
    out = _combine(c0, cn, x1, pos_tok, gate_tok, y, cap)
    return out.reshape(b, s, d)


def kernel(x_prompt, x_sample, rel_bias, attn_norm, w_in, diff_q_norm, diff_k_norm, lambda_q1, lambda_k1, lambda_q2, lambda_k2, diff_subln, mla_q_latent_norm, mla_kv_latent_norm, w_uq, w_ukv, mla_q_norm, mla_k_norm, w_out, ffn_norm, w_router, w_gate, w_up, w_down):
    params = [_layer_params(l, rel_bias, attn_norm, w_in, diff_q_norm, diff_k_norm, lambda_q1, lambda_k1,
                            lambda_q2, lambda_k2, diff_subln, mla_q_latent_norm, mla_kv_latent_norm, w_uq,
                            w_ukv, mla_q_norm, mla_k_norm, w_out, ffn_norm, w_router) for l in range(DEPTH)]
    biases = [_bias_tiles(p["rel_bias_t"], ATT_T) for p in params]
    rope = _rope_tables(max(x_prompt.shape[1], x_sample.shape[1]))
    outs = []
    for x in (x_prompt, x_sample):
        for l in range(DEPTH):
            lam_init = 0.8 - 0.6 * math.exp(-0.3 * l)
            x = _layer(x, params[l], biases[l], rope, w_gate[l], w_up[l], w_down[l], lam_init)
        outs.append(x)
    return tuple(outs)
```
